```python
import jax
import jax.numpy as jnp
from jax import lax
import numpy as np

D_MODEL = 4096
BATCH = 2
SEQ = 8192
DEPTH = 2

SSD_EXPAND = 2
SSD_D_INNER = SSD_EXPAND * D_MODEL
SSD_HEAD_DIM = 64
SSD_N_HEADS = SSD_D_INNER // SSD_HEAD_DIM
SSD_N_GROUPS = 8
SSD_HEADS_PER_GROUP = SSD_N_HEADS // SSD_N_GROUPS
SSD_D_STATE = 128
SSD_CONV_WIDTH = 4
SSD_CHUNK = 128
SSD_CONV_DIM = SSD_D_INNER + 2 * SSD_N_GROUPS * SSD_D_STATE

MLA_N_HEADS = D_MODEL // 128
MLA_Q_LORA = 1024
MLA_KV_LORA = 512
MLA_QK_NOPE = 128
MLA_QK_ROPE = 64
MLA_QK_DIM = MLA_QK_NOPE + MLA_QK_ROPE
MLA_V_DIM = 128
MLA_WIDTH = MLA_N_HEADS * MLA_V_DIM
ROPE_THETA = 10000.0
Q_BLOCK = 128

N_BRANCHES = 2
NORM_EPS = 1e-6

IN_WIDTHS = (SSD_D_INNER, SSD_CONV_DIM, SSD_N_HEADS, MLA_Q_LORA, MLA_KV_LORA,
             MLA_QK_ROPE, MLA_WIDTH, N_BRANCHES * D_MODEL)
IN_TOTAL = sum(IN_WIDTHS)
IN_SPLITS = tuple(int(v) for v in np.cumsum(IN_WIDTHS)[:-1])

kernel_name = 'hybrid_ssd_mla_gated_parallel'


def rms_norm(x, w):
    xf = x.astype(jnp.float32)
    y = xf * lax.rsqrt(jnp.mean(xf * xf, axis=-1, keepdims=True) + NORM_EPS)
    return (y * w.astype(jnp.float32)).astype(x.dtype)


def grouped_rms_norm(x, w, groups):
    shp = x.shape
    xg = x.astype(jnp.float32).reshape(shp[:-1] + (groups, shp[-1] // groups))
    xg = xg * lax.rsqrt(jnp.mean(xg * xg, axis=-1, keepdims=True) + NORM_EPS)
    return (xg.reshape(shp) * w.astype(jnp.float32)).astype(x.dtype)


def causal_depthwise_conv(u, w, b):
    k, ch = w.shape
    out = lax.conv_general_dilated(
        u, w[:, None, :], window_strides=(1,), padding=[(k - 1, 0)],
        dimension_numbers=('NWC', 'WIO', 'NWC'), feature_group_count=ch)
    return out + b


def rope_tables(positions, dtype):
    inv_freq = ROPE_THETA ** (-(jnp.arange(0, MLA_QK_ROPE, 2, dtype=jnp.float32) / MLA_QK_ROPE))
    ang = positions.astype(jnp.float32)[..., None] * inv_freq
    return jnp.cos(ang).astype(dtype), jnp.sin(ang).astype(dtype)


def apply_rope(x, cos, sin):
    x1, x2 = jnp.split(x, 2, axis=-1)
    return jnp.concatenate([x1 * cos - x2 * sin, x2 * cos + x1 * sin], axis=-1)


def ssd_chunked_scan(xh, dt, a_neg, bm, cm):
    b, s, h, p = xh.shape
    g, n = bm.shape[2], bm.shape[3]
    r = h // g
    L = SSD_CHUNK
    nc = s // L
    xdt = (xh * dt[..., None]).reshape(b, nc, L, g, r, p)
    a = (dt * a_neg).reshape(b, nc, L, g, r)
    bc = bm.reshape(b, nc, L, g, n)
    cc = cm.reshape(b, nc, L, g, n)
    xs = (jnp.moveaxis(xdt, 1, 0), jnp.moveaxis(a, 1, 0),
          jnp.moveaxis(bc, 1, 0), jnp.moveaxis(cc, 1, 0))
    mask = jnp.tril(jnp.ones((L, L), dtype=bool))[None, :, :, None, None]

    def step(state, inp):
        xdt_c, a_c, b_c, c_c = inp
        acs = jnp.cumsum(a_c, axis=1)
        seg = acs[:, :, None] - acs[:, None, :]
        decay = jnp.exp(jnp.where(mask, seg, -jnp.inf))
        cb = jnp.einsum('blgn,bsgn->blsg', c_c, b_c)
        y_diag = jnp.einsum('blsg,blsgr,bsgrp->blgrp', cb, decay, xdt_c)
        y_off = jnp.einsum('blgn,bgrpn,blgr->blgrp', c_c, state, jnp.exp(acs))
        to_end = jnp.exp(acs[:, -1:] - acs)
        new_state = (state * jnp.exp(acs[:, -1])[..., None, None]
                     + jnp.einsum('bsgn,bsgr,bsgrp->bgrpn', b_c, to_end, xdt_c))
        return new_state, y_diag + y_off

    state0 = jnp.zeros((b, g, r, p, n), jnp.float32)
    _, y = lax.scan(step, state0, xs)
    return jnp.moveaxis(y, 0, 1).reshape(b, s, h, p)


def mla_causal_attention(q_nope, q_rope, k_nope, k_rope, v):
    b, s, h, _ = q_nope.shape
    nb = s // Q_BLOCK
    scale = MLA_QK_DIM ** -0.5
    qn = jnp.moveaxis(q_nope.reshape(b, nb, Q_BLOCK, h, MLA_QK_NOPE), 1, 0)
    qr = jnp.moveaxis(q_rope.reshape(b, nb, Q_BLOCK, h, MLA_QK_ROPE), 1, 0)
    key_idx = jnp.arange(s)

    def block(args):
        qn_b, qr_b, i = args
        sc = (jnp.einsum('bqhd,bkhd->bhqk', qn_b, k_nope)
              + jnp.einsum('bqhd,bkd->bhqk', qr_b, k_rope)).astype(jnp.float32) * scale
        q_idx = i * Q_BLOCK + jnp.arange(Q_BLOCK)
        sc = jnp.where(q_idx[:, None] >= key_idx[None, :], sc, -jnp.inf)
        pr = jax.nn.softmax(sc, axis=-1).astype(v.dtype)
        return jnp.einsum('bhqk,bkhd->bqhd', pr, v)

    o = lax.map(block, (qn, qr, jnp.arange(nb)))
    return jnp.moveaxis(o, 0, 1).reshape(b, s, h * MLA_V_DIM)


def hybrid_layer(x, c, cos, sin, ada_w, ada_b, pre_norm_w, post_norm_w, w_in,
                 conv_w, conv_b, dt_bias, a_log, d_skip, ssd_norm_w, q_norm_w,
                 w_q_up, kv_norm_w, w_kv_up, w_ssd_proj, w_mla_proj, w_out):
    b, s, _ = x.shape
    mod = jax.nn.silu(c) @ ada_w + ada_b
    shift, scale, gate = jnp.split(mod, 3, axis=-1)
    h = rms_norm(x, pre_norm_w) * (1.0 + scale[:, None]) + shift[:, None]

    proj = h @ w_in
    z, xbc, dt_raw, cq, ckv, k_rope_raw, mla_gate, merge_logits = jnp.split(proj, IN_SPLITS, axis=-1)

    xbc = jax.nn.silu(causal_depthwise_conv(xbc, conv_w, conv_b))
    xs, bm, cm = jnp.split(xbc, (SSD_D_INNER, SSD_D_INNER + SSD_N_GROUPS * SSD_D_STATE), axis=-1)
    xh = xs.reshape(b, s, SSD_N_HEADS, SSD_HEAD_DIM).astype(jnp.float32)
    dt = jax.nn.softplus(dt_raw.astype(jnp.float32) + dt_bias.astype(jnp.float32))
    a_neg = -jnp.exp(a_log.astype(jnp.float32))
    ys = ssd_chunked_scan(xh, dt, a_neg,
                          bm.reshape(b, s, SSD_N_GROUPS, SSD_D_STATE).astype(jnp.float32),
                          cm.reshape(b, s, SSD_N_GROUPS, SSD_D_STATE).astype(jnp.float32))
    ys = ys + d_skip.astype(jnp.float32)[:, None] * xh
    ys = ys.reshape(b, s, SSD_D_INNER).astype(x.dtype)
    ys = grouped_rms_norm(ys * jax.nn.silu(z), ssd_norm_w, SSD_N_GROUPS)
    y_ssd = ys @ w_ssd_proj

    q = (rms_norm(cq, q_norm_w) @ w_q_up).reshape(b, s, MLA_N_HEADS, MLA_QK_DIM)
    q_nope, q_rope = jnp.split(q, (MLA_QK_NOPE,), axis=-1)
    q_rope = apply_rope(q_rope, cos[:, :, None], sin[:, :, None])
    kv = (rms_norm(ckv, kv_norm_w) @ w_kv_up).reshape(b, s, MLA_N_HEADS, MLA_QK_NOPE + MLA_V_DIM)
    k_nope, v = jnp.split(kv, (MLA_QK_NOPE,), axis=-1)
    k_rope = apply_rope(k_rope_raw, cos, sin)
    o = mla_causal_attention(q_nope, q_rope, k_nope, k_rope, v)
    y_mla = (o * jax.nn.silu(mla_gate)) @ w_mla_proj

    g_ssd, g_mla = jnp.split(jax.nn.sigmoid(merge_logits), N_BRANCHES, axis=-1)
    merged = g_ssd * y_ssd + g_mla * y_mla
    out = rms_norm(merged @ w_out, post_norm_w)
    return x + gate[:, None] * out


def setup_inputs(seed: int = 0) -> dict:
    key = jax.random.key(seed)
    ks = jax.random.split(key, 24)
    f32 = jnp.float32

    def normal(k, shape, sc):
        return jax.random.normal(k, shape, f32) * sc

    x = normal(ks[0], (BATCH, SEQ, D_MODEL), 1.0)
    c = normal(ks[1], (BATCH, D_MODEL), 1.0)
    offset = jax.random.randint(ks[2], (BATCH, 1), 0, 1024, dtype=jnp.int32)
    positions = offset + jnp.arange(SEQ, dtype=jnp.int32)[None, :]
    ada_w = normal(ks[3], (DEPTH, D_MODEL, 3 * D_MODEL), 0.5 * D_MODEL ** -0.5)
    ada_b = normal(ks[4], (DEPTH, 3 * D_MODEL), 0.02)
    pre_norm_w = 1.0 + normal(ks[5], (DEPTH, D_MODEL), 0.05)
    post_norm_w = 1.0 + normal(ks[6], (DEPTH, D_MODEL), 0.05)
    w_in = normal(ks[7], (DEPTH, D_MODEL, IN_TOTAL), D_MODEL ** -0.5)
    conv_w = normal(ks[8], (DEPTH, SSD_CONV_WIDTH, SSD_CONV_DIM), SSD_CONV_WIDTH ** -0.5)
    conv_b = normal(ks[9], (DEPTH, SSD_CONV_DIM), 0.02)
    dt_init = jnp.exp(jax.random.uniform(ks[10], (DEPTH, SSD_N_HEADS), f32,
                                         np.log(1e-3), np.log(1e-1)))
    dt_bias = dt_init + jnp.log(-jnp.expm1(-dt_init))
    a_log = jnp.log(jax.random.uniform(ks[11], (DEPTH, SSD_N_HEADS), f32, 1.0, 16.0))
    d_skip = 1.0 + normal(ks[12], (DEPTH, SSD_N_HEADS), 0.1)
    ssd_norm_w = 1.0 + normal(ks[13], (DEPTH, SSD_D_INNER), 0.05)
    q_norm_w = 1.0 + normal(ks[14], (DEPTH, MLA_Q_LORA), 0.05)
    w_q_up = normal(ks[15], (DEPTH, MLA_Q_LORA, MLA_N_HEADS * MLA_QK_DIM), MLA_Q_LORA ** -0.5)
    kv_norm_w = 1.0 + normal(ks[16], (DEPTH, MLA_KV_LORA), 0.05)
    w_kv_up = normal(ks[17], (DEPTH, MLA_KV_LORA, MLA_N_HEADS * (MLA_QK_NOPE + MLA_V_DIM)),
                     MLA_KV_LORA ** -0.5)
    w_ssd_proj = normal(ks[18], (DEPTH, SSD_D_INNER, D_MODEL), SSD_D_INNER ** -0.5)
    w_mla_proj = normal(ks[19], (DEPTH, MLA_WIDTH, D_MODEL), MLA_WIDTH ** -0.5)
    w_out = normal(ks[20], (DEPTH, D_MODEL, D_MODEL), D_MODEL ** -0.5)
    return {'x': x, 'c': c, 'positions': positions, 'ada_w': ada_w, 'ada_b': ada_b,
            'pre_norm_w': pre_norm_w, 'post_norm_w': post_norm_w, 'w_in': w_in,
            'conv_w': conv_w, 'conv_b': conv_b, 'dt_bias': dt_bias, 'a_log': a_log,
            'd_skip': d_skip, 'ssd_norm_w': ssd_norm_w, 'q_norm_w': q_norm_w,
            'w_q_up': w_q_up, 'kv_norm_w': kv_norm_w, 'w_kv_up': w_kv_up,
            'w_ssd_proj': w_ssd_proj, 'w_mla_proj': w_mla_proj, 'w_out': w_out}


def reference(x, c, positions, ada_w, ada_b, pre_norm_w, post_norm_w, w_in, conv_w,
              conv_b, dt_bias, a_log, d_skip, ssd_norm_w, q_norm_w, w_q_up, kv_norm_w,
              w_kv_up, w_ssd_proj, w_mla_proj, w_out):
    cos, sin = rope_tables(positions, x.dtype)
    for l in range(DEPTH):
        x = hybrid_layer(x, c, cos, sin, ada_w[l], ada_b[l], pre_norm_w[l], post_norm_w[l],
                         w_in[l], conv_w[l], conv_b[l], dt_bias[l], a_log[l], d_skip[l],
                         ssd_norm_w[l], q_norm_w[l], w_q_up[l], kv_norm_w[l], w_kv_up[l],
                         w_ssd_proj[l], w_mla_proj[l], w_out[l])
    return x
```

```python
import functools
import math

import jax
import jax.numpy as jnp
from jax import lax
from jax.experimental import pallas as pl
from jax.experimental.pallas import tpu as pltpu

SSD_HEAD_DIM = 64
SSD_N_GROUPS = 8
SSD_D_STATE = 128
SSD_CHUNK = 128
SSD_CONV_WIDTH = 4
MLA_QK_NOPE = 128
MLA_QK_ROPE = 64
MLA_V_DIM = 128
ROPE_THETA = 10000.0
NORM_EPS = 1e-6

LANES = 128
SUBLANES = 8
VMEM_LIMIT = 56 * 1024 * 1024

F32 = jnp.float32
BF16 = jnp.bfloat16


def _cparams(sem):
    return pltpu.CompilerParams(dimension_semantics=sem, vmem_limit_bytes=VMEM_LIMIT)


def _silu(v):
    return v * jax.nn.sigmoid(v)


def _split3(v):
    hi = v.astype(BF16)
    r1 = v - hi.astype(F32)
    mid = r1.astype(BF16)
    lo = (r1 - mid.astype(F32)).astype(BF16)
    return hi, mid, lo


def _dot(a, b):
    return jnp.dot(a, b, preferred_element_type=F32)


def _ada_kernel(c_ref, w_ref, b_ref, o_ref):
    a = _silu(c_ref[...]).astype(BF16)
    o_ref[...] = _dot(a, w_ref[...].astype(BF16)) + b_ref[...]


def _ada(c_pad, ada_w, ada_b):
    depth, d, n = ada_w.shape
    tn = min(512, n)
    return pl.pallas_call(
        _ada_kernel,
        grid=(depth, n // tn),
        in_specs=[
            pl.BlockSpec((SUBLANES, d), lambda l, j: (0, 0)),
            pl.BlockSpec((None, d, tn), lambda l, j: (l, 0, j)),
            pl.BlockSpec((None, 1, tn), lambda l, j: (l, 0, j)),
        ],
        out_specs=pl.BlockSpec((None, SUBLANES, tn), lambda l, j: (l, 0, j)),
        out_shape=jax.ShapeDtypeStruct((depth, SUBLANES, n), F32),
        compiler_params=_cparams(("parallel", "parallel")),
        name="ada_mod",
    )(c_pad, ada_w, ada_b.reshape(depth, 1, n))


def _prenorm_kernel(x_ref, w_ref, shift_ref, scale_ref, o_ref):
    x = x_ref[...]
    y = x * lax.rsqrt(jnp.mean(x * x, axis=-1, keepdims=True) + NORM_EPS)
    o_ref[...] = (y * w_ref[...] * (1.0 + scale_ref[...]) + shift_ref[...]).astype(BF16)


def _prenorm(x2, w, mod4, seq):
    m, d = x2.shape
    tm = min(512, seq)
    per_b = seq // tm
    return pl.pallas_call(
        _prenorm_kernel,
        grid=(m // tm,),
        in_specs=[
            pl.BlockSpec((tm, d), lambda i: (i, 0)),
            pl.BlockSpec((1, d), lambda i: (0, 0)),
            pl.BlockSpec((None, None, 1, d), lambda i: (i // per_b, 0, 0, 0)),
            pl.BlockSpec((None, None, 1, d), lambda i: (i // per_b, 1, 0, 0)),
        ],
        out_specs=pl.BlockSpec((tm, d), lambda i: (i, 0)),
        out_shape=jax.ShapeDtypeStruct((m, d), BF16),
        compiler_params=_cparams(("parallel",)),
        name="prenorm",
    )(x2, w.reshape(1, d), mod4, mod4)


def _mm_kernel(a_ref, w_ref, o_ref):
    o_ref[...] = _dot(a_ref[...], w_ref[...]).astype(o_ref.dtype)


def _matmul(a, w, out_dtype, tm, tn):
    m, k = a.shape
    n = w.shape[1]
    return pl.pallas_call(
        _mm_kernel,
        grid=(m // tm, n // tn),
        in_specs=[
            pl.BlockSpec((tm, k), lambda i, j: (i, 0)),
            pl.BlockSpec((k, tn), lambda i, j: (0, j)),
        ],
        out_specs=pl.BlockSpec((tm, tn), lambda i, j: (i, j)),
        out_shape=jax.ShapeDtypeStruct((m, n), out_dtype),
        compiler_params=_cparams(("parallel", "parallel")),
        name="in_proj",
    )(a, w)


def _rope_block(a, cos, sin):
    return a * cos + pltpu.roll(a, LANES // 2, axis=1) * sin


def _norm_mm_kernel(a_ref, nw_ref, w_ref, *rest, rope, tn):
    if rope:
        cos_ref, sin_ref, o_ref, an_sc = rest
    else:
        o_ref, an_sc = rest

    @pl.when(pl.program_id(1) == 0)
    def _():
        a = a_ref[...].astype(F32)
        y = a * lax.rsqrt(jnp.mean(a * a, axis=-1, keepdims=True) + NORM_EPS)
        an_sc[...] = (y * nw_ref[...]).astype(BF16)

    acc = _dot(an_sc[...], w_ref[...])
    if rope:
        cos = cos_ref[...]
        sin = sin_ref[...]
        for hh in range(tn // (2 * LANES)):
            c0 = hh * 2 * LANES
            o_ref[:, c0:c0 + LANES] = acc[:, c0:c0 + LANES].astype(o_ref.dtype)
            o_ref[:, c0 + LANES:c0 + 2 * LANES] = _rope_block(
                acc[:, c0 + LANES:c0 + 2 * LANES], cos, sin).astype(o_ref.dtype)
    else:
        o_ref[...] = acc.astype(o_ref.dtype)


def _norm_matmul(proj, col_off, k, norm_w, w, tm, tn, cos=None, sin=None, name="norm_mm"):
    m = proj.shape[0]
    n = w.shape[1]
    rope = cos is not None
    assert col_off % k == 0
    in_specs = [
        pl.BlockSpec((tm, k), lambda i, j: (i, col_off // k)),
        pl.BlockSpec((1, k), lambda i, j: (0, 0)),
        pl.BlockSpec((k, tn), lambda i, j: (0, j)),
    ]
    args = [proj, norm_w.reshape(1, k), w]
    if rope:
        in_specs += [pl.BlockSpec((tm, LANES), lambda i, j: (i, 0))] * 2
        args += [cos, sin]
    return pl.pallas_call(
        functools.partial(_norm_mm_kernel, rope=rope, tn=tn),
        grid=(m // tm, n // tn),
        in_specs=in_specs,
        out_specs=pl.BlockSpec((tm, tn), lambda i, j: (i, j)),
        out_shape=jax.ShapeDtypeStruct((m, n), BF16),
        scratch_shapes=[pltpu.VMEM((tm, k), BF16)],
        compiler_params=_cparams(("parallel", "arbitrary")),
        name=name,
    )(*args)


def _rope_tab_kernel(pos_ref, freq_ref, cos_ref, sin_ref):
    ang = pos_ref[...].astype(F32) * freq_ref[...]
    lane = lax.broadcasted_iota(jnp.int32, ang.shape, 1)
    keep = lane < MLA_QK_ROPE
    cos_ref[...] = jnp.where(keep, jnp.cos(ang), 0.0)
    sin_ref[...] = jnp.where(keep, jnp.sin(ang), 0.0)


def _rope_tables(positions):
    m = positions.size
    tm = min(1024, m)
    half = MLA_QK_ROPE // 2
    inv_freq = ROPE_THETA ** (-(jnp.arange(0, MLA_QK_ROPE, 2, dtype=F32) / MLA_QK_ROPE))
    freq = jnp.concatenate([inv_freq, inv_freq, jnp.zeros((LANES - 2 * half,), F32)]).reshape(1, LANES)
    return pl.pallas_call(
        _rope_tab_kernel,
        grid=(m // tm,),
        in_specs=[pl.BlockSpec((tm, 1), lambda i: (i, 0)),
                  pl.BlockSpec((1, LANES), lambda i: (0, 0))],
        out_specs=[pl.BlockSpec((tm, LANES), lambda i: (i, 0))] * 2,
        out_shape=[jax.ShapeDtypeStruct((m, LANES), F32)] * 2,
        compiler_params=_cparams(("parallel",)),
        name="rope_tables",
    )(positions.reshape(m, 1), freq)


def _krope_kernel(a_ref, cos_ref, sin_ref, o_ref):
    o_ref[...] = _rope_block(a_ref[...].astype(F32), cos_ref[...], sin_ref[...]).astype(BF16)


def _krope(proj, col_off, cos, sin):
    m = proj.shape[0]
    tm = min(1024, m)
    return pl.pallas_call(
        _krope_kernel,
        grid=(m // tm,),
        in_specs=[pl.BlockSpec((tm, LANES), lambda i: (i, col_off // LANES)),
                  pl.BlockSpec((tm, LANES), lambda i: (i, 0)),
                  pl.BlockSpec((tm, LANES), lambda i: (i, 0))],
        out_specs=pl.BlockSpec((tm, LANES), lambda i: (i, 0)),
        out_shape=jax.ShapeDtypeStruct((m, LANES), BF16),
        compiler_params=_cparams(("parallel",)),
        name="k_rope",
    )(proj, cos, sin)


def _ssd_pre_kernel(dt_ref, bias_ref, alog_ref, tri_ref, efull_ref,
                    dtb_ref, eacs_ref, toend_ref, acsrot_ref, acst_ref, decend_ref, *, tp, r):
    L = SSD_CHUNK
    dt = jax.nn.softplus(dt_ref[...].astype(F32) + bias_ref[...])
    dtb_ref[...] = dt.astype(BF16)
    a = dt * (-jnp.exp(alog_ref[...]))
    tri = tri_ref[...]
    lasts = []
    for c in range(tp // L):
        rows = slice(c * L, (c + 1) * L)
        hi, mid, lo = _split3(a[rows])
        acs = _dot(tri, hi) + _dot(tri, mid) + _dot(tri, lo)
        last = acs[L - 1:L, :]
        lasts.append(last)
        eacs_ref[rows, :] = jnp.exp(acs).astype(BF16)
        toend_ref[rows, :] = jnp.exp(last - acs).astype(BF16)
        acst_ref[c] = acs.T
        for g in range(SSD_N_GROUPS):
            shift = (LANES - g * r) % LANES
            acsrot_ref[g, rows, :] = acs if shift == 0 else pltpu.roll(acs, shift, axis=1)
    n_c = tp // L
    pad = [jnp.zeros((SUBLANES - n_c % SUBLANES, LANES), F32)] if n_c % SUBLANES else []
    el = jnp.exp(jnp.concatenate(lasts + pad, axis=0))
    hi, mid, lo = _split3(el)
    e = efull_ref[...]
    dec = _dot(hi, e) + _dot(mid, e) + _dot(lo, e)
    for c in range(n_c):
        decend_ref[c] = dec[c:c + 1, :]


def _ssd_pre(proj, col_off, dt_bias, a_log, d_inner, tp):
    m = proj.shape[0]
    nh = dt_bias.shape[0]
    assert nh == LANES and col_off % LANES == 0
    r = nh // SSD_N_GROUPS
    L = SSD_CHUNK
    tri = (jnp.arange(L)[:, None] >= jnp.arange(L)[None, :]).astype(BF16)
    efull = (jnp.arange(nh)[:, None] == (jnp.arange(d_inner)[None, :] // SSD_HEAD_DIM)).astype(BF16)
    nct = m // L
    return pl.pallas_call(
        functools.partial(_ssd_pre_kernel, tp=tp, r=r),
        grid=(m // tp,),
        in_specs=[
            pl.BlockSpec((tp, LANES), lambda i: (i, col_off // LANES)),
            pl.BlockSpec((1, LANES), lambda i: (0, 0)),
            pl.BlockSpec((1, LANES), lambda i: (0, 0)),
            pl.BlockSpec((L, L), lambda i: (0, 0)),
            pl.BlockSpec((nh, d_inner), lambda i: (0, 0)),
        ],
        out_specs=[
            pl.BlockSpec((tp, LANES), lambda i: (i, 0)),
            pl.BlockSpec((tp, LANES), lambda i: (i, 0)),
            pl.BlockSpec((tp, LANES), lambda i: (i, 0)),
            pl.BlockSpec((SSD_N_GROUPS, tp, LANES), lambda i: (0, i, 0)),
            pl.BlockSpec((tp // L, LANES, L), lambda i: (i, 0, 0)),
            pl.BlockSpec((tp // L, 1, d_inner), lambda i: (i, 0, 0)),
        ],
        out_shape=[
            jax.ShapeDtypeStruct((m, LANES), BF16),
            jax.ShapeDtypeStruct((m, LANES), BF16),
            jax.ShapeDtypeStruct((m, LANES), BF16),
            jax.ShapeDtypeStruct((SSD_N_GROUPS, m, LANES), F32),
            jax.ShapeDtypeStruct((nct, LANES, L), F32),
            jax.ShapeDtypeStruct((nct, 1, d_inner), F32),
        ],
        compiler_params=_cparams(("parallel",)),
        name="ssd_pre",
    )(proj, dt_bias.reshape(1, nh), a_log.reshape(1, nh), tri, efull)


def _ssd_kernel(xs_ref, bm_ref, cm_ref, z_ref, dtb_ref, eacs_ref, toend_ref, acs_ref, acst_ref,
                decend_ref, cwx_ref, cbx_ref, cwb_ref, cbb_ref, cwc_ref, cbc_ref, dskip_ref,
                nw_ref, e_ref, o_ref,
                xpad_sc, bpad_sc, cpad_sc, xc_sc, bc_sc, cc_sc, state_sc, *, t_blk, r):
    L = SSD_CHUNK
    K = SSD_CONV_WIDTH
    P = SSD_HEAD_DIM
    gw = r * P
    halo = SUBLANES

    @pl.when(pl.program_id(2) == 0)
    def _():
        state_sc[...] = jnp.zeros_like(state_sc)
        xpad_sc[0:halo, :] = jnp.zeros((halo, gw), F32)
        bpad_sc[0:halo, :] = jnp.zeros((halo, SSD_D_STATE), F32)
        cpad_sc[0:halo, :] = jnp.zeros((halo, SSD_D_STATE), F32)

    def conv(pad_sc, in_ref, w_ref, b_ref, out_sc):
        pad_sc[halo:halo + t_blk, :] = in_ref[...].astype(F32)
        acc = b_ref[...]
        for k in range(K):
            s0 = halo - (K - 1) + k
            acc = acc + w_ref[k:k + 1, :] * pad_sc[s0:s0 + t_blk, :]
        tail = pad_sc[t_blk:t_blk + halo, :]
        pad_sc[0:halo, :] = tail
        out_sc[...] = _silu(acc).astype(out_sc.dtype)

    conv(xpad_sc, xs_ref, cwx_ref, cbx_ref, xc_sc)
    conv(bpad_sc, bm_ref, cwb_ref, cbb_ref, bc_sc)
    conv(cpad_sc, cm_ref, cwc_ref, cbc_ref, cc_sc)

    li = lax.broadcasted_iota(jnp.int32, (L, L), 0)
    si = lax.broadcasted_iota(jnp.int32, (L, L), 1)
    causal = li >= si
    lane = lax.broadcasted_iota(jnp.int32, (L, 2 * P), 1)
    lo_half = lane < P

    def chunk(c, carry):
        rows = pl.ds(pl.multiple_of(c * L, L), L)
        xc = xc_sc[rows, :]
        b_c = bc_sc[rows, :]
        c_c = cc_sc[rows, :]
        e = e_ref[...]
        dt_e = _dot(dtb_ref[rows, :], e)
        eacs_e = _dot(eacs_ref[rows, :], e)
        toend_e = _dot(toend_ref[rows, :], e)
        xdt = xc * dt_e
        xdt_b = xdt.astype(BF16)
        xw_b = (xdt * toend_e).astype(BF16)
        cb = lax.dot_general(c_c, b_c, (((1,), (1,)), ((), ())), preferred_element_type=F32)
        acs = acs_ref[rows, :]
        acst = acst_ref[c]
        st = state_sc[...]
        y = _dot(c_c, st.astype(BF16)) * eacs_e
        pieces = []
        for jp in range(r // 2):
            xp = xdt_b[:, jp * 2 * P:(jp + 1) * 2 * P]
            acc = None
            for u in range(2):
                j = 2 * jp + u
                colb = jnp.broadcast_to(acs[:, j:j + 1], (L, L))
                rowb = jnp.broadcast_to(acst[j:j + 1, :], (L, L))
                dec = jnp.exp(jnp.where(causal, colb - rowb, -jnp.inf))
                m_h = (cb * dec).astype(BF16)
                xm = jnp.where(lo_half if u == 0 else jnp.logical_not(lo_half), xp, jnp.zeros_like(xp))
                d = _dot(m_h, xm)
                acc = d if acc is None else acc + d
            pieces.append(acc)
        y = y + jnp.concatenate(pieces, axis=1)
        upd = lax.dot_general(b_c, xw_b, (((0,), (0,)), ((), ())), preferred_element_type=F32)
        state_sc[...] = st * decend_ref[c] + upd
        y = y + dskip_ref[...] * xc
        yg = y * _silu(z_ref[rows, :].astype(F32))
        ms = jnp.mean(yg * yg, axis=-1, keepdims=True)
        o_ref[rows, :] = (yg * lax.rsqrt(ms + NORM_EPS) * nw_ref[...]).astype(o_ref.dtype)
        return carry

    lax.fori_loop(0, t_blk // L, chunk, 0)


def _ssd(proj, offs, pre, conv_w, conv_b, d_skip, ssd_norm_w, batch, seq, t_blk):
    m = proj.shape[0]
    d_inner = ssd_norm_w.shape[0]
    nh = d_skip.shape[0]
    G = SSD_N_GROUPS
    N = SSD_D_STATE
    L = SSD_CHUNK
    r = nh // G
    gw = r * SSD_HEAD_DIM
    assert gw % LANES == 0 and r % 2 == 0 and r % SUBLANES == 0
    dtb, eacs, toend, acsrot, acst, decend = pre
    nt = seq // t_blk
    ncb = t_blk // L
    e_all = (jnp.arange(LANES)[None, :, None]
             == (jnp.arange(G)[:, None, None] * r + jnp.arange(gw)[None, None, :] // SSD_HEAD_DIM)).astype(BF16)
    dskip_e = jnp.repeat(d_skip.astype(F32), SSD_HEAD_DIM).reshape(1, d_inner)
    cwx, cwb, cwc = conv_w[:, :d_inner], conv_w[:, d_inner:d_inner + G * N], conv_w[:, d_inner + G * N:]
    cb2 = conv_b.reshape(1, -1)
    cbx, cbb, cbc = cb2[:, :d_inner], cb2[:, d_inner:d_inner + G * N], cb2[:, d_inner + G * N:]
    for o in (offs["xs"], offs["z"]):
        assert o % gw == 0
    row = lambda b, g, t: b * nt + t
    in_specs = [
        pl.BlockSpec((t_blk, gw), lambda b, g, t: (row(b, g, t), offs["xs"] // gw + g)),
        pl.BlockSpec((t_blk, N), lambda b, g, t: (row(b, g, t), offs["bm"] // N + g)),
        pl.BlockSpec((t_blk, N), lambda b, g, t: (row(b, g, t), offs["cm"] // N + g)),
        pl.BlockSpec((t_blk, gw), lambda b, g, t: (row(b, g, t), offs["z"] // gw + g)),
        pl.BlockSpec((t_blk, LANES), lambda b, g, t: (row(b, g, t), 0)),
        pl.BlockSpec((t_blk, LANES), lambda b, g, t: (row(b, g, t), 0)),
        pl.BlockSpec((t_blk, LANES), lambda b, g, t: (row(b, g, t), 0)),
        pl.BlockSpec((None, t_blk, LANES), lambda b, g, t: (g, row(b, g, t), 0)),
        pl.BlockSpec((ncb, r, L), lambda b, g, t: (row(b, g, t), g, 0)),
        pl.BlockSpec((ncb, 1, gw), lambda b, g, t: (row(b, g, t), 0, g)),
        pl.BlockSpec((SSD_CONV_WIDTH, gw), lambda b, g, t: (0, g)),
        pl.BlockSpec((1, gw), lambda b, g, t: (0, g)),
        pl.BlockSpec((SSD_CONV_WIDTH, N), lambda b, g, t: (0, g)),
        pl.BlockSpec((1, N), lambda b, g, t: (0, g)),
        pl.BlockSpec((SSD_CONV_WIDTH, N), lambda b, g, t: (0, g)),
        pl.BlockSpec((1, N), lambda b, g, t: (0, g)),
        pl.BlockSpec((1, gw), lambda b, g, t: (0, g)),
        pl.BlockSpec((1, gw), lambda b, g, t: (0, g)),
        pl.BlockSpec((None, LANES, gw), lambda b, g, t: (g, 0, 0)),
    ]
    return pl.pallas_call(
        functools.partial(_ssd_kernel, t_blk=t_blk, r=r),
        grid=(batch, G, nt),
        in_specs=in_specs,
        out_specs=pl.BlockSpec((t_blk, gw), lambda b, g, t: (row(b, g, t), g)),
        out_shape=jax.ShapeDtypeStruct((m, d_inner), BF16),
        scratch_shapes=[
            pltpu.VMEM((t_blk + SUBLANES, gw), F32),
            pltpu.VMEM((t_blk + SUBLANES, N), F32),
            pltpu.VMEM((t_blk + SUBLANES, N), F32),
            pltpu.VMEM((t_blk, gw), F32),
            pltpu.VMEM((t_blk, N), BF16),
            pltpu.VMEM((t_blk, N), BF16),
            pltpu.VMEM((N, gw), F32),
        ],
        compiler_params=_cparams(("parallel", "parallel", "arbitrary")),
        name="ssd_scan",
    )(proj, proj, proj, proj, dtb, eacs, toend, acsrot, acst, decend,
      cwx, cbx, cwb, cbb, cwc, cbc, dskip_e, ssd_norm_w.reshape(1, d_inner), e_all)


def _flash_kernel(q_ref, kn_ref, kr_ref, v_ref, g_ref, o_ref, m_sc, l_sc, acc_sc, *, tq, tk):
    qi = pl.program_id(2)
    n_sub = tq // tk
    m_sc[...] = jnp.full(m_sc.shape, -jnp.inf, F32)
    l_sc[...] = jnp.zeros(l_sc.shape, F32)
    acc_sc[...] = jnp.zeros(acc_sc.shape, F32)

    def step(kb, r0, masked):
        nr = tq - r0
        cols = pl.ds(pl.multiple_of(kb * tk, tk), tk)
        k = jnp.concatenate([kn_ref[cols, :], kr_ref[cols, :]], axis=1)
        q = q_ref[r0:tq, :]
        s = lax.dot_general(q, k, (((1,), (1,)), ((), ())), preferred_element_type=F32)
        if masked:
            ri = lax.broadcasted_iota(jnp.int32, (nr, tk), 0)
            ci = lax.broadcasted_iota(jnp.int32, (nr, tk), 1)
            s = jnp.where(ri >= ci, s, -jnp.inf)
        m_prev = m_sc[r0:tq, :]
        m_new = jnp.maximum(m_prev, jnp.max(s, axis=1, keepdims=True))
        alpha = jnp.exp2(m_prev - m_new)
        p = jnp.exp2(s - pltpu.repeat(m_new, tk // LANES, axis=1))
        l_sc[r0:tq, :] = alpha * l_sc[r0:tq, :] + jnp.sum(p, axis=1, keepdims=True)
        acc_sc[r0:tq, :] = acc_sc[r0:tq, :] * alpha + _dot(p.astype(BF16), v_ref[cols, :])
        m_sc[r0:tq, :] = m_new

    def full_body(kb, carry):
        step(kb, 0, False)
        return carry

    lax.fori_loop(0, qi * n_sub, full_body, 0)
    for d in range(n_sub):
        step(qi * n_sub + d, d * tk, True)
    g = g_ref[...].astype(F32)
    o_ref[...] = (acc_sc[...] / l_sc[...] * _silu(g)).astype(o_ref.dtype)


def _flash(q, kv, kr, proj, gate_off, batch, seq, n_heads, tq, tk):
    m = q.shape[0]
    nq = seq // tq
    dq = 2 * LANES
    assert gate_off % LANES == 0
    return pl.pallas_call(
        functools.partial(_flash_kernel, tq=tq, tk=tk),
        grid=(batch, n_heads, nq),
        in_specs=[
            pl.BlockSpec((tq, dq), lambda b, h, i: (b * nq + i, h)),
            pl.BlockSpec((seq, LANES), lambda b, h, i: (b, 2 * h)),
            pl.BlockSpec((seq, LANES), lambda b, h, i: (b, 0)),
            pl.BlockSpec((seq, LANES), lambda b, h, i: (b, 2 * h + 1)),
            pl.BlockSpec((tq, LANES), lambda b, h, i: (b * nq + i, gate_off // LANES + h)),
        ],
        out_specs=pl.BlockSpec((tq, LANES), lambda b, h, i: (b * nq + i, h)),
        out_shape=jax.ShapeDtypeStruct((m, n_heads * MLA_V_DIM), BF16),
        scratch_shapes=[pltpu.VMEM((tq, LANES), F32), pltpu.VMEM((tq, LANES), F32),
                        pltpu.VMEM((tq, LANES), F32)],
        compiler_params=_cparams(("parallel", "parallel", "arbitrary")),
        name="mla_flash",
    )(q, kv, kr, kv, proj)


def _merge_kernel(ys_ref, og_ref, ws_ref, wm_ref, g1_ref, g2_ref, o_ref, acc_sc, part_sc, *, ks, km):
    k = pl.program_id(2)

    @pl.when(k == 0)
    def _():
        acc_sc[...] = jnp.zeros_like(acc_sc)

    @pl.when(k < ks)
    def _():
        acc_sc[...] += _dot(ys_ref[...], ws_ref[...])

    @pl.when(k == ks - 1)
    def _():
        part_sc[...] = jax.nn.sigmoid(g1_ref[...].astype(F32)) * acc_sc[...]
        acc_sc[...] = jnp.zeros_like(acc_sc)

    @pl.when(k >= ks)
    def _():
        acc_sc[...] += _dot(og_ref[...], wm_ref[...])

    @pl.when(k == ks + km - 1)
    def _():
        o_ref[...] = (part_sc[...] + jax.nn.sigmoid(g2_ref[...].astype(F32)) * acc_sc[...]).astype(o_ref.dtype)


def _merge(ys, og, w_ssd, w_mla, proj, merge_off, tm, tn, tk):
    m, k1 = ys.shape
    k2 = og.shape[1]
    d = w_ssd.shape[1]
    ks, km = k1 // tk, k2 // tk
    assert merge_off % tn == 0 and d % tn == 0
    return pl.pallas_call(
        functools.partial(_merge_kernel, ks=ks, km=km),
        grid=(m // tm, d // tn, ks + km),
        in_specs=[
            pl.BlockSpec((tm, tk), lambda i, j, k: (i, jnp.minimum(k, ks - 1))),
            pl.BlockSpec((tm, tk), lambda i, j, k: (i, jnp.maximum(k - ks, 0))),
            pl.BlockSpec((tk, tn), lambda i, j, k: (jnp.minimum(k, ks - 1), j)),
            pl.BlockSpec((tk, tn), lambda i, j, k: (jnp.maximum(k - ks, 0), j)),
            pl.BlockSpec((tm, tn), lambda i, j, k: (i, merge_off // tn + j)),
            pl.BlockSpec((tm, tn), lambda i, j, k: (i, (merge_off + d) // tn + j)),
        ],
        out_specs=pl.BlockSpec((tm, tn), lambda i, j, k: (i, j)),
        out_shape=jax.ShapeDtypeStruct((m, d), BF16),
        scratch_shapes=[pltpu.VMEM((tm, tn), F32), pltpu.VMEM((tm, tn), F32)],
        compiler_params=_cparams(("parallel", "parallel", "arbitrary")),
        name="branch_merge",
    )(ys, og, w_ssd, w_mla, proj, proj)


def _out_kernel(a_ref, w_ref, x_ref, pw_ref, gate_ref, o_ref, *, nk):
    k = pl.program_id(1)

    @pl.when(k == 0)
    def _():
        o_ref[...] = jnp.zeros_like(o_ref)

    o_ref[...] += _dot(a_ref[...], w_ref[...])

    @pl.when(k == nk - 1)
    def _():
        y = o_ref[...]
        yn = y * lax.rsqrt(jnp.mean(y * y, axis=-1, keepdims=True) + NORM_EPS) * pw_ref[...]
        o_ref[...] = x_ref[...] + gate_ref[...] * yn


def _out_proj(merged, w_out, x2, post_w, mod4, seq, tm, tk):
    m, d = x2.shape
    nk = d // tk
    per_b = seq // tm
    return pl.pallas_call(
        functools.partial(_out_kernel, nk=nk),
        grid=(m // tm, nk),
        in_specs=[
            pl.BlockSpec((tm, tk), lambda i, k: (i, k)),
            pl.BlockSpec((tk, d), lambda i, k: (k, 0)),
            pl.BlockSpec((tm, d), lambda i, k: (i, 0)),
            pl.BlockSpec((1, d), lambda i, k: (0, 0)),
            pl.BlockSpec((None, None, 1, d), lambda i, k: (i // per_b, 2, 0, 0)),
        ],
        out_specs=pl.BlockSpec((tm, d), lambda i, k: (i, 0)),
        out_shape=jax.ShapeDtypeStruct((m, d), F32),
        compiler_params=_cparams(("parallel", "arbitrary")),
        name="out_proj",
    )(merged, w_out, x2, post_w.reshape(1, d), mod4)


def _proj_layout(d, d_inner, nh, n_mla_heads, q_lora, kv_lora):
    gn = SSD_N_GROUPS * SSD_D_STATE
    widths = [("z", d_inner), ("xs", d_inner), ("bm", gn), ("cm", gn),
              ("gate", n_mla_heads * MLA_V_DIM), ("merge", 2 * d), ("cq", q_lora), ("ckv", kv_lora),
              ("kr", 2 * MLA_QK_ROPE), ("dt", nh)]
    offs, o = {}, 0
    for name, w in widths:
        offs[name] = o
        o += w
    return offs, o


def _prep_w_in(w_in, d_inner, nh, n_mla_heads, q_lora, kv_lora, d, n_pad):
    gn = SSD_N_GROUPS * SSD_D_STATE
    conv_dim = d_inner + 2 * gn
    o = 0
    z = w_in[:, o:o + d_inner]; o += d_inner
    xbc = w_in[:, o:o + conv_dim]; o += conv_dim
    dt = w_in[:, o:o + nh]; o += nh
    cq = w_in[:, o:o + q_lora]; o += q_lora
    ckv = w_in[:, o:o + kv_lora]; o += kv_lora
    kr = w_in[:, o:o + MLA_QK_ROPE]; o += MLA_QK_ROPE
    gate = w_in[:, o:o + n_mla_heads * MLA_V_DIM]; o += n_mla_heads * MLA_V_DIM
    merge = w_in[:, o:o + 2 * d]; o += 2 * d
    half = MLA_QK_ROPE // 2
    krot = jnp.concatenate([-kr[:, half:], kr[:, :half]], axis=1)
    parts = [z, xbc, gate, merge, cq, ckv, kr, krot, dt]
    tot = sum(p.shape[1] for p in parts)
    if n_pad > tot:
        parts.append(jnp.zeros((w_in.shape[0], n_pad - tot), w_in.dtype))
    return jnp.concatenate(parts, axis=1).astype(BF16)


def _prep_w_q(w_q_up, n_mla_heads):
    k = w_q_up.shape[0]
    qk = MLA_QK_NOPE + MLA_QK_ROPE
    half = MLA_QK_ROPE // 2
    w = w_q_up.reshape(k, n_mla_heads, qk) * (qk ** -0.5 * math.log2(math.e))
    nope, rope = w[..., :MLA_QK_NOPE], w[..., MLA_QK_NOPE:]
    rot = jnp.concatenate([-rope[..., half:], rope[..., :half]], axis=-1)
    return jnp.concatenate([nope, rope, rot], axis=-1).reshape(k, n_mla_heads * 2 * LANES).astype(BF16)


def _round_up(v, mult):
    return (v + mult - 1) // mult * mult


def kernel(x, c, positions, ada_w, ada_b, pre_norm_w, post_norm_w, w_in, conv_w, conv_b, dt_bias,
           a_log, d_skip, ssd_norm_w, q_norm_w, w_q_up, kv_norm_w, w_kv_up, w_ssd_proj, w_mla_proj,
           w_out):
    batch, seq, d = x.shape
    depth = ada_w.shape[0]
    m = batch * seq
    d_inner = ssd_norm_w.shape[1]
    nh = dt_bias.shape[1]
    q_lora = q_norm_w.shape[1]
    kv_lora = kv_norm_w.shape[1]
    n_mla = w_mla_proj.shape[1] // MLA_V_DIM
    offs, n_tot = _proj_layout(d, d_inner, nh, n_mla, q_lora, kv_lora)

    tm_big = min(1024, m)
    tn_proj = 1024
    n_pad = _round_up(n_tot, tn_proj)

    c_pad = jnp.zeros((SUBLANES, d), F32).at[:batch].set(c)
    mod = _ada(c_pad, ada_w, ada_b)
    cos, sin = _rope_tables(positions)

    x2 = x.reshape(m, d)
    for l in range(depth):
        mod4 = mod[l, :batch].reshape(batch, 3, 1, d)
        h = _prenorm(x2, pre_norm_w[l], mod4, seq)
        w_in_p = _prep_w_in(w_in[l], d_inner, nh, n_mla, q_lora, kv_lora, d, n_pad)
        proj = _matmul(h, w_in_p, BF16, tm_big, tn_proj)

        pre = _ssd_pre(proj, offs["dt"], dt_bias[l], a_log[l], d_inner, tp=min(1024, seq))
        ys = _ssd(proj, offs, pre, conv_w[l], conv_b[l], d_skip[l], ssd_norm_w[l], batch, seq,
                  t_blk=min(512, seq))

        q = _norm_matmul(proj, offs["cq"], q_lora, q_norm_w[l], _prep_w_q(w_q_up[l], n_mla),
                         tm_big, min(1024, n_mla * 2 * LANES), cos, sin, name="q_up")
        kv = _norm_matmul(proj, offs["ckv"], kv_lora, kv_norm_w[l], w_kv_up[l].astype(BF16),
                          tm_big, min(1024, n_mla * 2 * LANES), name="kv_up")
        kr = _krope(proj, offs["kr"], cos, sin)
        tq = min(1024, seq)
        og = _flash(q, kv, kr, proj, offs["gate"], batch, seq, n_mla, tq, min(512, tq))

        merged = _merge(ys, og, w_ssd_proj[l].astype(BF16), w_mla_proj[l].astype(BF16), proj,
                        offs["merge"], tm_big, min(1024, d), min(2048, n_mla * MLA_V_DIM))
        x2 = _out_proj(merged, w_out[l].astype(BF16), x2, post_norm_w[l], mod4, seq,
                       min(512, seq), min(512, d))
    return x2.reshape(batch, seq, d)
```

```python
import functools
import math

import jax
import jax.numpy as jnp
from jax import lax
from jax.experimental import pallas as pl
from jax.experimental.pallas import tpu as pltpu

SSD_HEAD_DIM = 64
SSD_N_GROUPS = 8
SSD_D_STATE = 128
SSD_CHUNK = 128
SSD_CONV_WIDTH = 4
MLA_QK_NOPE = 128
MLA_QK_ROPE = 64
MLA_V_DIM = 128
ROPE_THETA = 10000.0
NORM_EPS = 1e-6

LANES = 128
SUBLANES = 8
VMEM_LIMIT = 56 * 1024 * 1024
CAST_BLOCK_BYTES = 8 * 1024 * 1024
W_IN_PREP_ROWS = 64

F32 = jnp.float32
BF16 = jnp.bfloat16


def _cparams(sem):
    return pltpu.CompilerParams(dimension_semantics=sem, vmem_limit_bytes=VMEM_LIMIT)


def _silu(v):
    return v * jax.nn.sigmoid(v)


def _split3(v):
    hi = v.astype(BF16)
    r1 = v - hi.astype(F32)
    mid = r1.astype(BF16)
    lo = (r1 - mid.astype(F32)).astype(BF16)
    return hi, mid, lo


def _dot(a, b):
    return jnp.dot(a, b, preferred_element_type=F32)


def _ada_kernel(c_ref, w_ref, b_ref, o_ref):
    @pl.when(pl.program_id(1) == 0)
    def _():
        o_ref[...] = jnp.broadcast_to(b_ref[...], o_ref.shape)

    a = _silu(c_ref[...]).astype(BF16)
    o_ref[...] += _dot(a, w_ref[...].astype(BF16))


def _ada(c_pad, ada_w, ada_b):
    depth, d, n = ada_w.shape
    tk = min(256, d)
    return pl.pallas_call(
        _ada_kernel,
        grid=(depth, d // tk),
        in_specs=[
            pl.BlockSpec((SUBLANES, tk), lambda l, k: (0, k)),
            pl.BlockSpec((None, tk, n), lambda l, k: (l, k, 0)),
            pl.BlockSpec((None, 1, n), lambda l, k: (l, 0, 0)),
        ],
        out_specs=pl.BlockSpec((None, SUBLANES, n), lambda l, k: (l, 0, 0)),
        out_shape=jax.ShapeDtypeStruct((depth, SUBLANES, n), F32),
        compiler_params=_cparams(("parallel", "arbitrary")),
        name="ada_mod",
    )(c_pad, ada_w, ada_b.reshape(depth, 1, n))


def _prenorm_kernel(x_ref, w_ref, shift_ref, scale_ref, o_ref):
    x = x_ref[...]
    y = x * lax.rsqrt(jnp.mean(x * x, axis=-1, keepdims=True) + NORM_EPS)
    o_ref[...] = (y * w_ref[...] * (1.0 + scale_ref[...]) + shift_ref[...]).astype(BF16)


def _prenorm(x2, w, mod4, seq):
    m, d = x2.shape
    tm = min(512, seq)
    per_b = seq // tm
    return pl.pallas_call(
        _prenorm_kernel,
        grid=(m // tm,),
        in_specs=[
            pl.BlockSpec((tm, d), lambda i: (i, 0)),
            pl.BlockSpec((1, d), lambda i: (0, 0)),
            pl.BlockSpec((None, None, 1, d), lambda i: (i // per_b, 0, 0, 0)),
            pl.BlockSpec((None, None, 1, d), lambda i: (i // per_b, 1, 0, 0)),
        ],
        out_specs=pl.BlockSpec((tm, d), lambda i: (i, 0)),
        out_shape=jax.ShapeDtypeStruct((m, d), BF16),
        compiler_params=_cparams(("parallel",)),
        name="prenorm",
    )(x2, w.reshape(1, d), mod4, mod4)


def _mm_kernel(a_ref, w_ref, o_ref):
    o_ref[...] = _dot(a_ref[...], w_ref[...]).astype(o_ref.dtype)


def _matmul(a, w3, l, out_dtype, tm, tn):
    m, k = a.shape
    n = w3.shape[2]
    return pl.pallas_call(
        _mm_kernel,
        grid=(m // tm, n // tn),
        in_specs=[
            pl.BlockSpec((tm, k), lambda i, j: (i, 0)),
            pl.BlockSpec((None, k, tn), lambda i, j: (l, 0, j)),
        ],
        out_specs=pl.BlockSpec((tm, tn), lambda i, j: (i, j)),
        out_shape=jax.ShapeDtypeStruct((m, n), out_dtype),
        compiler_params=_cparams(("parallel", "parallel")),
        name="in_proj",
    )(a, w3)


def _rope_block(a, cos, sin):
    return a * cos + pltpu.roll(a, LANES // 2, axis=1) * sin


def _norm_mm_kernel(a_ref, nw_ref, w_ref, *rest, rope, tn):
    if rope:
        cos_ref, sin_ref, o_ref, an_sc = rest
    else:
        o_ref, an_sc = rest

    @pl.when(pl.program_id(1) == 0)
    def _():
        a = a_ref[...].astype(F32)
        y = a * lax.rsqrt(jnp.mean(a * a, axis=-1, keepdims=True) + NORM_EPS)
        an_sc[...] = (y * nw_ref[...]).astype(BF16)

    acc = _dot(an_sc[...], w_ref[...])
    if rope:
        cos = cos_ref[...]
        sin = sin_ref[...]
        for hh in range(tn // (2 * LANES)):
            c0 = hh * 2 * LANES
            o_ref[:, c0:c0 + LANES] = acc[:, c0:c0 + LANES].astype(o_ref.dtype)
            o_ref[:, c0 + LANES:c0 + 2 * LANES] = _rope_block(
                acc[:, c0 + LANES:c0 + 2 * LANES], cos, sin).astype(o_ref.dtype)
    else:
        o_ref[...] = acc.astype(o_ref.dtype)


def _norm_matmul(proj, col_off, k, norm_w, w, l, tm, tn, cos=None, sin=None, name="norm_mm"):
    m = proj.shape[0]
    n = w.shape[2]
    rope = cos is not None
    assert col_off % k == 0
    in_specs = [
        pl.BlockSpec((tm, k), lambda i, j: (i, col_off // k)),
        pl.BlockSpec((1, k), lambda i, j: (0, 0)),
        pl.BlockSpec((None, k, tn), lambda i, j: (l, 0, j)),
    ]
    args = [proj, norm_w.reshape(1, k), w]
    if rope:
        in_specs += [pl.BlockSpec((tm, LANES), lambda i, j: (i, 0))] * 2
        args += [cos, sin]
    return pl.pallas_call(
        functools.partial(_norm_mm_kernel, rope=rope, tn=tn),
        grid=(m // tm, n // tn),
        in_specs=in_specs,
        out_specs=pl.BlockSpec((tm, tn), lambda i, j: (i, j)),
        out_shape=jax.ShapeDtypeStruct((m, n), BF16),
        scratch_shapes=[pltpu.VMEM((tm, k), BF16)],
        compiler_params=_cparams(("parallel", "arbitrary")),
        name=name,
    )(*args)


def _rope_tab_kernel(pos_ref, freq_ref, cos_ref, sin_ref):
    ang = pos_ref[...].astype(F32) * freq_ref[...]
    lane = lax.broadcasted_iota(jnp.int32, ang.shape, 1)
    keep = lane < MLA_QK_ROPE
    cos_ref[...] = jnp.where(keep, jnp.cos(ang), 0.0)
    sin_ref[...] = jnp.where(keep, jnp.sin(ang), 0.0)


def _rope_tables(positions):
    m = positions.size
    tm = min(1024, m)
    half = MLA_QK_ROPE // 2
    inv_freq = ROPE_THETA ** (-(jnp.arange(0, MLA_QK_ROPE, 2, dtype=F32) / MLA_QK_ROPE))
    freq = jnp.concatenate([inv_freq, inv_freq, jnp.zeros((LANES - 2 * half,), F32)]).reshape(1, LANES)
    return pl.pallas_call(
        _rope_tab_kernel,
        grid=(m // tm,),
        in_specs=[pl.BlockSpec((tm, 1), lambda i: (i, 0)),
                  pl.BlockSpec((1, LANES), lambda i: (0, 0))],
        out_specs=[pl.BlockSpec((tm, LANES), lambda i: (i, 0))] * 2,
        out_shape=[jax.ShapeDtypeStruct((m, LANES), F32)] * 2,
        compiler_params=_cparams(("parallel",)),
        name="rope_tables",
    )(positions.reshape(m, 1), freq)


def _krope_kernel(a_ref, cos_ref, sin_ref, o_ref):
    o_ref[...] = _rope_block(a_ref[...].astype(F32), cos_ref[...], sin_ref[...]).astype(BF16)


def _krope(proj, col_off, cos, sin):
    m = proj.shape[0]
    tm = min(1024, m)
    return pl.pallas_call(
        _krope_kernel,
        grid=(m // tm,),
        in_specs=[pl.BlockSpec((tm, LANES), lambda i: (i, col_off // LANES)),
                  pl.BlockSpec((tm, LANES), lambda i: (i, 0)),
                  pl.BlockSpec((tm, LANES), lambda i: (i, 0))],
        out_specs=pl.BlockSpec((tm, LANES), lambda i: (i, 0)),
        out_shape=jax.ShapeDtypeStruct((m, LANES), BF16),
        compiler_params=_cparams(("parallel",)),
        name="k_rope",
    )(proj, cos, sin)


def _ssd_pre_kernel(dt_ref, bias_ref, alog_ref, tri_ref, efull_ref,
                    dtb_ref, eacs_ref, toend_ref, acsrot_ref, acst_ref, decend_ref, *, tp, r):
    L = SSD_CHUNK
    dt = jax.nn.softplus(dt_ref[...].astype(F32) + bias_ref[...])
    dtb_ref[...] = dt.astype(BF16)
    a = dt * (-jnp.exp(alog_ref[...]))
    tri = tri_ref[...]
    lasts = []
    for c in range(tp // L):
        rows = slice(c * L, (c + 1) * L)
        hi, mid, lo = _split3(a[rows])
        acs = _dot(tri, hi) + _dot(tri, mid) + _dot(tri, lo)
        last = acs[L - 1:L, :]
        lasts.append(last)
        eacs_ref[rows, :] = jnp.exp(acs).astype(BF16)
        toend_ref[rows, :] = jnp.exp(last - acs).astype(BF16)
        acst_ref[c] = acs.T
        for g in range(SSD_N_GROUPS):
            shift = (LANES - g * r) % LANES
            acsrot_ref[g, rows, :] = acs if shift == 0 else pltpu.roll(acs, shift, axis=1)
    n_c = tp // L
    pad = [jnp.zeros((SUBLANES - n_c % SUBLANES, LANES), F32)] if n_c % SUBLANES else []
    el = jnp.exp(jnp.concatenate(lasts + pad, axis=0))
    hi, mid, lo = _split3(el)
    e = efull_ref[...]
    dec = _dot(hi, e) + _dot(mid, e) + _dot(lo, e)
    for c in range(n_c):
        decend_ref[c] = dec[c:c + 1, :]


def _ssd_pre(proj, col_off, dt_bias, a_log, d_inner, tp):
    m = proj.shape[0]
    nh = dt_bias.shape[0]
    assert nh == LANES and col_off % LANES == 0
    r = nh // SSD_N_GROUPS
    L = SSD_CHUNK
    tri = (jnp.arange(L)[:, None] >= jnp.arange(L)[None, :]).astype(BF16)
    efull = (jnp.arange(nh)[:, None] == (jnp.arange(d_inner)[None, :] // SSD_HEAD_DIM)).astype(BF16)
    nct = m // L
    return pl.pallas_call(
        functools.partial(_ssd_pre_kernel, tp=tp, r=r),
        grid=(m // tp,),
        in_specs=[
            pl.BlockSpec((tp, LANES), lambda i: (i, col_off // LANES)),
            pl.BlockSpec((1, LANES), lambda i: (0, 0)),
            pl.BlockSpec((1, LANES), lambda i: (0, 0)),
            pl.BlockSpec((L, L), lambda i: (0, 0)),
            pl.BlockSpec((nh, d_inner), lambda i: (0, 0)),
        ],
        out_specs=[
            pl.BlockSpec((tp, LANES), lambda i: (i, 0)),
            pl.BlockSpec((tp, LANES), lambda i: (i, 0)),
            pl.BlockSpec((tp, LANES), lambda i: (i, 0)),
            pl.BlockSpec((SSD_N_GROUPS, tp, LANES), lambda i: (0, i, 0)),
            pl.BlockSpec((tp // L, LANES, L), lambda i: (i, 0, 0)),
            pl.BlockSpec((tp // L, 1, d_inner), lambda i: (i, 0, 0)),
        ],
        out_shape=[
            jax.ShapeDtypeStruct((m, LANES), BF16),
            jax.ShapeDtypeStruct((m, LANES), BF16),
            jax.ShapeDtypeStruct((m, LANES), BF16),
            jax.ShapeDtypeStruct((SSD_N_GROUPS, m, LANES), F32),
            jax.ShapeDtypeStruct((nct, LANES, L), F32),
            jax.ShapeDtypeStruct((nct, 1, d_inner), F32),
        ],
        compiler_params=_cparams(("parallel",)),
        name="ssd_pre",
    )(proj, dt_bias.reshape(1, nh), a_log.reshape(1, nh), tri, efull)


def _ssd_kernel(xs_ref, bm_ref, cm_ref, z_ref, dtb_ref, eacs_ref, toend_ref, acs_ref, acst_ref,
                decend_ref, cwx_ref, cbx_ref, cwb_ref, cbb_ref, cwc_ref, cbc_ref, dskip_ref,
                nw_ref, e_ref, o_ref,
                xpad_sc, bpad_sc, cpad_sc, xc_sc, bc_sc, cc_sc, state_sc, *, t_blk, r):
    L = SSD_CHUNK
    K = SSD_CONV_WIDTH
    P = SSD_HEAD_DIM
    gw = r * P
    halo = SUBLANES

    @pl.when(pl.program_id(2) == 0)
    def _():
        state_sc[...] = jnp.zeros_like(state_sc)
        xpad_sc[0:halo, :] = jnp.zeros((halo, gw), F32)
        bpad_sc[0:halo, :] = jnp.zeros((halo, SSD_D_STATE), F32)
        cpad_sc[0:halo, :] = jnp.zeros((halo, SSD_D_STATE), F32)

    def conv(pad_sc, in_ref, w_ref, b_ref, out_sc):
        pad_sc[halo:halo + t_blk, :] = in_ref[...].astype(F32)
        acc = b_ref[...]
        for k in range(K):
            s0 = halo - (K - 1) + k
            acc = acc + w_ref[k:k + 1, :] * pad_sc[s0:s0 + t_blk, :]
        tail = pad_sc[t_blk:t_blk + halo, :]
        pad_sc[0:halo, :] = tail
        out_sc[...] = _silu(acc).astype(out_sc.dtype)

    conv(xpad_sc, xs_ref, cwx_ref, cbx_ref, xc_sc)
    conv(bpad_sc, bm_ref, cwb_ref, cbb_ref, bc_sc)
    conv(cpad_sc, cm_ref, cwc_ref, cbc_ref, cc_sc)

    li = lax.broadcasted_iota(jnp.int32, (L, L), 0)
    si = lax.broadcasted_iota(jnp.int32, (L, L), 1)
    causal = li >= si
    lane = lax.broadcasted_iota(jnp.int32, (L, 2 * P), 1)
    lo_half = lane < P

    def chunk(c, carry):
        rows = pl.ds(pl.multiple_of(c * L, L), L)
        xc = xc_sc[rows, :]
        b_c = bc_sc[rows, :]
        c_c = cc_sc[rows, :]
        e = e_ref[...]
        dt_e = _dot(dtb_ref[rows, :], e)
        eacs_e = _dot(eacs_ref[rows, :], e)
        toend_e = _dot(toend_ref[rows, :], e)
        xdt = xc * dt_e
        xdt_b = xdt.astype(BF16)
        xw_b = (xdt * toend_e).astype(BF16)
        cb = lax.dot_general(c_c, b_c, (((1,), (1,)), ((), ())), preferred_element_type=F32)
        acs = acs_ref[rows, :]
        acst = acst_ref[c]
        st = state_sc[...]
        y = _dot(c_c, st.astype(BF16)) * eacs_e
        pieces = []
        for jp in range(r // 2):
            xp = xdt_b[:, jp * 2 * P:(jp + 1) * 2 * P]
            acc = None
            for u in range(2):
                j = 2 * jp + u
                colb = jnp.broadcast_to(acs[:, j:j + 1], (L, L))
                rowb = jnp.broadcast_to(acst[j:j + 1, :], (L, L))
                dec = jnp.exp(jnp.where(causal, colb - rowb, -jnp.inf))
                m_h = (cb * dec).astype(BF16)
                xm = jnp.where(lo_half if u == 0 else jnp.logical_not(lo_half), xp, jnp.zeros_like(xp))
                d = _dot(m_h, xm)
                acc = d if acc is None else acc + d
            pieces.append(acc)
        y = y + jnp.concatenate(pieces, axis=1)
        upd = lax.dot_general(b_c, xw_b, (((0,), (0,)), ((), ())), preferred_element_type=F32)
        state_sc[...] = st * decend_ref[c] + upd
        y = y + dskip_ref[...] * xc
        yg = y * _silu(z_ref[rows, :].astype(F32))
        ms = jnp.mean(yg * yg, axis=-1, keepdims=True)
        o_ref[rows, :] = (yg * lax.rsqrt(ms + NORM_EPS) * nw_ref[...]).astype(o_ref.dtype)
        return carry

    lax.fori_loop(0, t_blk // L, chunk, 0)


def _ssd(proj, offs, pre, conv_w, conv_b, d_skip, ssd_norm_w, batch, seq, t_blk):
    m = proj.shape[0]
    d_inner = ssd_norm_w.shape[0]
    nh = d_skip.shape[0]
    G = SSD_N_GROUPS
    N = SSD_D_STATE
    L = SSD_CHUNK
    r = nh // G
    gw = r * SSD_HEAD_DIM
    assert gw % LANES == 0 and r % 2 == 0 and r % SUBLANES == 0
    dtb, eacs, toend, acsrot, acst, decend = pre
    nt = seq // t_blk
    ncb = t_blk // L
    e_all = (jnp.arange(LANES)[None, :, None]
             == (jnp.arange(G)[:, None, None] * r + jnp.arange(gw)[None, None, :] // SSD_HEAD_DIM)).astype(BF16)
    dskip_e = jnp.repeat(d_skip.astype(F32), SSD_HEAD_DIM).reshape(1, d_inner)
    cwx, cwb, cwc = conv_w[:, :d_inner], conv_w[:, d_inner:d_inner + G * N], conv_w[:, d_inner + G * N:]
    cb2 = conv_b.reshape(1, -1)
    cbx, cbb, cbc = cb2[:, :d_inner], cb2[:, d_inner:d_inner + G * N], cb2[:, d_inner + G * N:]
    for o in (offs["xs"], offs["z"]):
        assert o % gw == 0
    row = lambda b, g, t: b * nt + t
    in_specs = [
        pl.BlockSpec((t_blk, gw), lambda b, g, t: (row(b, g, t), offs["xs"] // gw + g)),
        pl.BlockSpec((t_blk, N), lambda b, g, t: (row(b, g, t), offs["bm"] // N + g)),
        pl.BlockSpec((t_blk, N), lambda b, g, t: (row(b, g, t), offs["cm"] // N + g)),
        pl.BlockSpec((t_blk, gw), lambda b, g, t: (row(b, g, t), offs["z"] // gw + g)),
        pl.BlockSpec((t_blk, LANES), lambda b, g, t: (row(b, g, t), 0)),
        pl.BlockSpec((t_blk, LANES), lambda b, g, t: (row(b, g, t), 0)),
        pl.BlockSpec((t_blk, LANES), lambda b, g, t: (row(b, g, t), 0)),
        pl.BlockSpec((None, t_blk, LANES), lambda b, g, t: (g, row(b, g, t), 0)),
        pl.BlockSpec((ncb, r, L), lambda b, g, t: (row(b, g, t), g, 0)),
        pl.BlockSpec((ncb, 1, gw), lambda b, g, t: (row(b, g, t), 0, g)),
        pl.BlockSpec((SSD_CONV_WIDTH, gw), lambda b, g, t: (0, g)),
        pl.BlockSpec((1, gw), lambda b, g, t: (0, g)),
        pl.BlockSpec((SSD_CONV_WIDTH, N), lambda b, g, t: (0, g)),
        pl.BlockSpec((1, N), lambda b, g, t: (0, g)),
        pl.BlockSpec((SSD_CONV_WIDTH, N), lambda b, g, t: (0, g)),
        pl.BlockSpec((1, N), lambda b, g, t: (0, g)),
        pl.BlockSpec((1, gw), lambda b, g, t: (0, g)),
        pl.BlockSpec((1, gw), lambda b, g, t: (0, g)),
        pl.BlockSpec((None, LANES, gw), lambda b, g, t: (g, 0, 0)),
    ]
    return pl.pallas_call(
        functools.partial(_ssd_kernel, t_blk=t_blk, r=r),
        grid=(batch, G, nt),
        in_specs=in_specs,
        out_specs=pl.BlockSpec((t_blk, gw), lambda b, g, t: (row(b, g, t), g)),
        out_shape=jax.ShapeDtypeStruct((m, d_inner), BF16),
        scratch_shapes=[
            pltpu.VMEM((t_blk + SUBLANES, gw), F32),
            pltpu.VMEM((t_blk + SUBLANES, N), F32),
            pltpu.VMEM((t_blk + SUBLANES, N), F32),
            pltpu.VMEM((t_blk, gw), F32),
            pltpu.VMEM((t_blk, N), BF16),
            pltpu.VMEM((t_blk, N), BF16),
            pltpu.VMEM((N, gw), F32),
        ],
        compiler_params=_cparams(("parallel", "parallel", "arbitrary")),
        name="ssd_scan",
    )(proj, proj, proj, proj, dtb, eacs, toend, acsrot, acst, decend,
      cwx, cbx, cwb, cbb, cwc, cbc, dskip_e, ssd_norm_w.reshape(1, d_inner), e_all)


def _flash_kernel(q_ref, kn_ref, kr_ref, v_ref, g_ref, o_ref, vt_sc, m_sc, l_sc, acc_sc, *, tq, tk, seq):
    qi = pl.program_id(2)
    n_sub = tq // tk
    tqs = min(2 * LANES, tq)
    ahead = 2 * (tq // tqs)

    @pl.when(qi == 0)
    def _():
        for c in range(seq // tk):
            vt_sc[c] = v_ref[c * tk:(c + 1) * tk, :].astype(F32).T.astype(BF16)

    m_sc[...] = jnp.full(m_sc.shape, -jnp.inf, F32)
    l_sc[...] = jnp.zeros(l_sc.shape, F32)
    acc_sc[...] = jnp.zeros(acc_sc.shape, F32)

    def process(blocks):
        chains = []
        for bi, (kb, diag) in enumerate(blocks):
            k0 = 0 if diag is None else diag * tk
            chains += [(bi, diag, k0, c0) for c0 in range(k0, tq, tqs)]
        loaded = {}

        def scores(bi, diag, k0, c0):
            if bi not in loaded:
                rows = pl.ds(pl.multiple_of(blocks[bi][0] * tk, tk), tk)
                loaded[bi] = jnp.concatenate([kn_ref[rows, :], kr_ref[rows, :]], axis=1)
            st = lax.dot_general(loaded[bi], q_ref[c0:c0 + tqs, :], (((1,), (1,)), ((), ())),
                                 preferred_element_type=F32)
            if diag is not None and c0 < k0 + tk:
                ki = lax.broadcasted_iota(jnp.int32, (tk, tqs), 0) + k0
                qj = lax.broadcasted_iota(jnp.int32, (tk, tqs), 1) + c0
                st = jnp.where(ki <= qj, st, -jnp.inf)
            return st

        sts = {i: scores(*chains[i]) for i in range(min(ahead, len(chains)))}
        for i, (bi, _, _, c0) in enumerate(chains):
            if i + ahead < len(chains):
                sts[i + ahead] = scores(*chains[i + ahead])
            st = sts.pop(i)
            vt = vt_sc[blocks[bi][0]]
            cs = slice(c0, c0 + tqs)
            m_prev = m_sc[:, cs]
            m_new = jnp.maximum(m_prev, jnp.max(st, axis=0, keepdims=True))
            alpha = jnp.exp2(m_prev - m_new)
            p = jnp.exp2(st - m_new)
            l_sc[:, cs] = alpha * l_sc[:, cs] + jnp.sum(p, axis=0, keepdims=True)
            acc_sc[:, cs] = acc_sc[:, cs] * alpha + _dot(vt, p.astype(BF16))
            m_sc[:, cs] = m_new

    def full_body(j, carry):
        process([(j * n_sub + d, None) for d in range(n_sub)])
        return carry

    lax.fori_loop(0, qi, full_body, 0)
    process([(qi * n_sub + d, d) for d in range(n_sub)])
    o = (acc_sc[...] / l_sc[...]).T
    o_ref[...] = (o * _silu(g_ref[...].astype(F32))).astype(o_ref.dtype)


def _flash(q, kv, kr, proj, gate_off, batch, seq, n_heads, tq, tk):
    m = q.shape[0]
    nq = seq // tq
    dq = 2 * LANES
    assert gate_off % LANES == 0
    return pl.pallas_call(
        functools.partial(_flash_kernel, tq=tq, tk=tk, seq=seq),
        grid=(batch, n_heads, nq),
        in_specs=[
            pl.BlockSpec((tq, dq), lambda b, h, i: (b * nq + i, h)),
            pl.BlockSpec((seq, LANES), lambda b, h, i: (b, 2 * h)),
            pl.BlockSpec((seq, LANES), lambda b, h, i: (b, 0)),
            pl.BlockSpec((seq, LANES), lambda b, h, i: (b, 2 * h + 1)),
            pl.BlockSpec((tq, LANES), lambda b, h, i: (b * nq + i, gate_off // LANES + h)),
        ],
        out_specs=pl.BlockSpec((tq, LANES), lambda b, h, i: (b * nq + i, h)),
        out_shape=jax.ShapeDtypeStruct((m, n_heads * MLA_V_DIM), BF16),
        scratch_shapes=[pltpu.VMEM((seq // tk, MLA_V_DIM, tk), BF16), pltpu.VMEM((1, tq), F32),
                        pltpu.VMEM((1, tq), F32), pltpu.VMEM((MLA_V_DIM, tq), F32)],
        compiler_params=_cparams(("parallel", "parallel", "arbitrary")),
        name="mla_flash",
    )(q, kv, kr, kv, proj)


def _merge_kernel(ys_ref, og_ref, ws_ref, wm_ref, g1_ref, g2_ref, o_ref, acc_sc, part_sc, *, ks, km):
    k = pl.program_id(2)

    @pl.when(k == 0)
    def _():
        acc_sc[...] = jnp.zeros_like(acc_sc)

    @pl.when(k < ks)
    def _():
        acc_sc[...] += _dot(ys_ref[...], ws_ref[...])

    @pl.when(k == ks - 1)
    def _():
        part_sc[...] = jax.nn.sigmoid(g1_ref[...].astype(F32)) * acc_sc[...]
        acc_sc[...] = jnp.zeros_like(acc_sc)

    @pl.when(k >= ks)
    def _():
        acc_sc[...] += _dot(og_ref[...], wm_ref[...])

    @pl.when(k == ks + km - 1)
    def _():
        o_ref[...] = (part_sc[...] + jax.nn.sigmoid(g2_ref[...].astype(F32)) * acc_sc[...]).astype(o_ref.dtype)


def _merge(ys, og, w_ssd, w_mla, l, proj, merge_off, tm, tn, tk):
    m, k1 = ys.shape
    k2 = og.shape[1]
    d = w_ssd.shape[2]
    ks, km = k1 // tk, k2 // tk
    assert merge_off % tn == 0 and d % tn == 0
    return pl.pallas_call(
        functools.partial(_merge_kernel, ks=ks, km=km),
        grid=(m // tm, d // tn, ks + km),
        in_specs=[
            pl.BlockSpec((tm, tk), lambda i, j, k: (i, jnp.minimum(k, ks - 1))),
            pl.BlockSpec((tm, tk), lambda i, j, k: (i, jnp.maximum(k - ks, 0))),
            pl.BlockSpec((None, tk, tn), lambda i, j, k: (l, jnp.minimum(k, ks - 1), j)),
            pl.BlockSpec((None, tk, tn), lambda i, j, k: (l, jnp.maximum(k - ks, 0), j)),
            pl.BlockSpec((tm, tn), lambda i, j, k: (i, merge_off // tn + j)),
            pl.BlockSpec((tm, tn), lambda i, j, k: (i, (merge_off + d) // tn + j)),
        ],
        out_specs=pl.BlockSpec((tm, tn), lambda i, j, k: (i, j)),
        out_shape=jax.ShapeDtypeStruct((m, d), BF16),
        scratch_shapes=[pltpu.VMEM((tm, tn), F32), pltpu.VMEM((tm, tn), F32)],
        compiler_params=_cparams(("parallel", "parallel", "arbitrary")),
        name="branch_merge",
    )(ys, og, w_ssd, w_mla, proj, proj)


def _out_kernel(a_ref, w_ref, x_ref, pw_ref, gate_ref, o_ref, *, nk):
    k = pl.program_id(1)

    @pl.when(k == 0)
    def _():
        o_ref[...] = jnp.zeros_like(o_ref)

    o_ref[...] += _dot(a_ref[...], w_ref[...])

    @pl.when(k == nk - 1)
    def _():
        y = o_ref[...]
        yn = y * lax.rsqrt(jnp.mean(y * y, axis=-1, keepdims=True) + NORM_EPS) * pw_ref[...]
        o_ref[...] = x_ref[...] + gate_ref[...] * yn


def _out_proj(merged, w_out, l, x2, post_w, mod4, seq, tm, tk):
    m, d = x2.shape
    nk = d // tk
    per_b = seq // tm
    return pl.pallas_call(
        functools.partial(_out_kernel, nk=nk),
        grid=(m // tm, nk),
        in_specs=[
            pl.BlockSpec((tm, tk), lambda i, k: (i, k)),
            pl.BlockSpec((None, tk, d), lambda i, k: (l, k, 0)),
            pl.BlockSpec((tm, d), lambda i, k: (i, 0)),
            pl.BlockSpec((1, d), lambda i, k: (0, 0)),
            pl.BlockSpec((None, None, 1, d), lambda i, k: (i // per_b, 2, 0, 0)),
        ],
        out_specs=pl.BlockSpec((tm, d), lambda i, k: (i, 0)),
        out_shape=jax.ShapeDtypeStruct((m, d), F32),
        compiler_params=_cparams(("parallel", "arbitrary")),
        name="out_proj",
    )(merged, w_out, x2, post_w.reshape(1, d), mod4)


def _proj_layout(d, d_inner, nh, n_mla_heads, q_lora, kv_lora):
    gn = SSD_N_GROUPS * SSD_D_STATE
    widths = [("z", d_inner), ("xs", d_inner), ("bm", gn), ("cm", gn),
              ("gate", n_mla_heads * MLA_V_DIM), ("merge", 2 * d), ("cq", q_lora), ("ckv", kv_lora),
              ("kr", 2 * MLA_QK_ROPE), ("dt", nh)]
    offs, o = {}, 0
    for name, w in widths:
        offs[name] = o
        o += w
    return offs, o


def _cast_kernel(w_ref, o_ref):
    o_ref[...] = w_ref[...].astype(o_ref.dtype)


def _cast_bf16(w3):
    depth, k, n = w3.shape
    tr = max(2 * SUBLANES, min(k, CAST_BLOCK_BYTES // (n * 4)))
    assert k % tr == 0
    return pl.pallas_call(
        _cast_kernel,
        grid=(depth, k // tr),
        in_specs=[pl.BlockSpec((None, tr, n), lambda l, r: (l, r, 0))],
        out_specs=pl.BlockSpec((None, tr, n), lambda l, r: (l, r, 0)),
        out_shape=jax.ShapeDtypeStruct((depth, k, n), BF16),
        compiler_params=_cparams(("parallel", "parallel")),
        name="cast_bf16",
    )(w3)


def _w_in_kernel(w_ref, o_ref, *, moves, kr_src, kr_dst, n_used):
    def load(src, width):
        lead = src % LANES
        v = w_ref[:, src - lead:src + width]
        return v[:, lead:] if lead else v

    for dst, src, width in moves:
        o_ref[:, dst:dst + width] = load(src, width).astype(BF16)
    half = MLA_QK_ROPE // 2
    kr = load(kr_src, MLA_QK_ROPE)
    o_ref[:, kr_dst:kr_dst + 2 * MLA_QK_ROPE] = jnp.concatenate(
        [kr, -kr[:, half:], kr[:, :half]], axis=1).astype(BF16)
    n_out = o_ref.shape[1]
    if n_out > n_used:
        o_ref[:, n_used:] = jnp.zeros((o_ref.shape[0], n_out - n_used), BF16)


def _prep_w_in(w_in3, offs, n_tot, n_pad, d, d_inner, nh, n_mla_heads, q_lora, kv_lora):
    depth, k, n_src = w_in3.shape
    gn = SSD_N_GROUPS * SSD_D_STATE
    conv_dim = d_inner + 2 * gn
    s_xbc = d_inner
    s_dt = s_xbc + conv_dim
    s_cq = s_dt + nh
    s_ckv = s_cq + q_lora
    s_kr = s_ckv + kv_lora
    s_gate = s_kr + MLA_QK_ROPE
    s_merge = s_gate + n_mla_heads * MLA_V_DIM
    assert s_merge + 2 * d == n_src
    moves = (
        (offs["z"], 0, d_inner + conv_dim),
        (offs["gate"], s_gate, n_mla_heads * MLA_V_DIM + 2 * d),
        (offs["cq"], s_cq, q_lora + kv_lora),
        (offs["dt"], s_dt, nh),
    )
    tr = W_IN_PREP_ROWS
    assert k % tr == 0
    return pl.pallas_call(
        functools.partial(_w_in_kernel, moves=moves, kr_src=s_kr, kr_dst=offs["kr"], n_used=n_tot),
        grid=(depth, k // tr),
        in_specs=[pl.BlockSpec((None, tr, n_src), lambda l, r: (l, r, 0))],
        out_specs=pl.BlockSpec((None, tr, n_pad), lambda l, r: (l, r, 0)),
        out_shape=jax.ShapeDtypeStruct((depth, k, n_pad), BF16),
        compiler_params=_cparams(("parallel", "parallel")),
        name="w_in_prep",
    )(w_in3)


def _prep_w_q(w_q_up3, n_mla_heads):
    depth, k, _ = w_q_up3.shape
    qk = MLA_QK_NOPE + MLA_QK_ROPE
    half = MLA_QK_ROPE // 2
    w = w_q_up3.reshape(depth, k, n_mla_heads, qk) * (qk ** -0.5 * math.log2(math.e))
    nope, rope = w[..., :MLA_QK_NOPE], w[..., MLA_QK_NOPE:]
    rot = jnp.concatenate([-rope[..., half:], rope[..., :half]], axis=-1)
    return jnp.concatenate([nope, rope, rot], axis=-1).reshape(
        depth, k, n_mla_heads * 2 * LANES).astype(BF16)


def _round_up(v, mult):
    return (v + mult - 1) // mult * mult


def kernel(x, c, positions, ada_w, ada_b, pre_norm_w, post_norm_w, w_in, conv_w, conv_b, dt_bias,
           a_log, d_skip, ssd_norm_w, q_norm_w, w_q_up, kv_norm_w, w_kv_up, w_ssd_proj, w_mla_proj,
           w_out):
    batch, seq, d = x.shape
    depth = ada_w.shape[0]
    m = batch * seq
    d_inner = ssd_norm_w.shape[1]
    nh = dt_bias.shape[1]
    q_lora = q_norm_w.shape[1]
    kv_lora = kv_norm_w.shape[1]
    n_mla = w_mla_proj.shape[1] // MLA_V_DIM
    offs, n_tot = _proj_layout(d, d_inner, nh, n_mla, q_lora, kv_lora)

    tm_big = min(1024, m)
    tn_proj = 1024
    n_pad = _round_up(n_tot, tn_proj)

    c_pad = jnp.zeros((SUBLANES, d), F32).at[:batch].set(c)
    mod = _ada(c_pad, ada_w, ada_b)
    cos, sin = _rope_tables(positions)

    w_in_b = _prep_w_in(w_in, offs, n_tot, n_pad, d, d_inner, nh, n_mla, q_lora, kv_lora)
    w_q_b = _prep_w_q(w_q_up, n_mla)
    w_kv_b = _cast_bf16(w_kv_up)
    w_ssd_b = _cast_bf16(w_ssd_proj)
    w_mla_b = _cast_bf16(w_mla_proj)
    w_out_b = _cast_bf16(w_out)

    x2 = x.reshape(m, d)
    for l in range(depth):
        mod4 = mod[l, :batch].reshape(batch, 3, 1, d)
        h = _prenorm(x2, pre_norm_w[l], mod4, seq)
        proj = _matmul(h, w_in_b, l, BF16, tm_big, tn_proj)

        pre = _ssd_pre(proj, offs["dt"], dt_bias[l], a_log[l], d_inner, tp=min(1024, seq))
        ys = _ssd(proj, offs, pre, conv_w[l], conv_b[l], d_skip[l], ssd_norm_w[l], batch, seq,
                  t_blk=min(512, seq))

        q = _norm_matmul(proj, offs["cq"], q_lora, q_norm_w[l], w_q_b, l,
                         tm_big, min(1024, n_mla * 2 * LANES), cos, sin, name="q_up")
        kv = _norm_matmul(proj, offs["ckv"], kv_lora, kv_norm_w[l], w_kv_b, l,
                          tm_big, min(1024, n_mla * 2 * LANES), name="kv_up")
        kr = _krope(proj, offs["kr"], cos, sin)
        tq = min(1024, seq)
        og = _flash(q, kv, kr, proj, offs["gate"], batch, seq, n_mla, tq, min(512, tq))

        merged = _merge(ys, og, w_ssd_b, w_mla_b, l, proj,
                        offs["merge"], tm_big, min(1024, d), min(2048, n_mla * MLA_V_DIM))
        x2 = _out_proj(merged, w_out_b, l, x2, post_norm_w[l], mod4, seq,
                       min(512, seq), min(512, d))
    return x2.reshape(batch, seq, d)
```

```python
import functools
import math

import jax
import jax.numpy as jnp
from jax import lax
from jax.experimental import pallas as pl
from jax.experimental.pallas import tpu as pltpu

SSD_HEAD_DIM = 64
SSD_N_GROUPS = 8
SSD_D_STATE = 128
SSD_CHUNK = 128
SSD_CONV_WIDTH = 4
MLA_QK_NOPE = 128
MLA_QK_ROPE = 64
MLA_V_DIM = 128
ROPE_THETA = 10000.0
NORM_EPS = 1e-6
LOG2_E = math.log2(math.e)

LANES = 128
SUBLANES = 8
CONV_HALO = 16
VMEM_LIMIT = 56 * 1024 * 1024
CAST_BLOCK_BYTES = 8 * 1024 * 1024
W_IN_CHUNK = 64
W_IN_COLS = 512
W_IN_COPY, W_IN_ROTATE, W_IN_ZERO = 0, 1, 2

F32 = jnp.float32
BF16 = jnp.bfloat16


def _cparams(sem):
    return pltpu.CompilerParams(dimension_semantics=sem, vmem_limit_bytes=VMEM_LIMIT)


def _silu(v):
    h = 0.5 * v
    return h + h * jnp.tanh(h)


def _split3(v):
    hi = v.astype(BF16)
    r1 = v - hi.astype(F32)
    mid = r1.astype(BF16)
    lo = (r1 - mid.astype(F32)).astype(BF16)
    return hi, mid, lo


def _dot(a, b):
    return jnp.dot(a, b, preferred_element_type=F32)


def _ada_kernel(c_ref, w_ref, b_ref, o_ref):
    @pl.when(pl.program_id(1) == 0)
    def _():
        o_ref[...] = jnp.broadcast_to(b_ref[...], o_ref.shape)

    a = _silu(c_ref[...]).astype(BF16)
    o_ref[...] += _dot(a, w_ref[...].astype(BF16))


def _ada(c_pad, ada_w, ada_b):
    depth, d, n = ada_w.shape
    tk = min(256, d)
    return pl.pallas_call(
        _ada_kernel,
        grid=(depth, d // tk),
        in_specs=[
            pl.BlockSpec((SUBLANES, tk), lambda l, k: (0, k)),
            pl.BlockSpec((None, tk, n), lambda l, k: (l, k, 0)),
            pl.BlockSpec((None, 1, n), lambda l, k: (l, 0, 0)),
        ],
        out_specs=pl.BlockSpec((None, SUBLANES, n), lambda l, k: (l, 0, 0)),
        out_shape=jax.ShapeDtypeStruct((depth, SUBLANES, n), F32),
        compiler_params=_cparams(("parallel", "arbitrary")),
        name="ada_mod",
    )(c_pad, ada_w, ada_b.reshape(depth, 1, n))


def _prenorm_kernel(x_ref, w_ref, shift_ref, scale_ref, o_ref):
    x = x_ref[...]
    y = x * lax.rsqrt(jnp.mean(x * x, axis=-1, keepdims=True) + NORM_EPS)
    o_ref[...] = (y * w_ref[...] * (1.0 + scale_ref[...]) + shift_ref[...]).astype(BF16)


def _prenorm(x2, w, mod4, seq):
    m, d = x2.shape
    tm = min(512, seq)
    per_b = seq // tm
    return pl.pallas_call(
        _prenorm_kernel,
        grid=(m // tm,),
        in_specs=[
            pl.BlockSpec((tm, d), lambda i: (i, 0)),
            pl.BlockSpec((1, d), lambda i: (0, 0)),
            pl.BlockSpec((None, None, 1, d), lambda i: (i // per_b, 0, 0, 0)),
            pl.BlockSpec((None, None, 1, d), lambda i: (i // per_b, 1, 0, 0)),
        ],
        out_specs=pl.BlockSpec((tm, d), lambda i: (i, 0)),
        out_shape=jax.ShapeDtypeStruct((m, d), BF16),
        compiler_params=_cparams(("parallel",)),
        name="prenorm",
    )(x2, w.reshape(1, d), mod4, mod4)


def _mm_kernel(a_ref, w_ref, o_ref):
    o_ref[...] = _dot(a_ref[...], w_ref[...]).astype(o_ref.dtype)


def _matmul(a, w3, l, out_dtype, tm, tn):
    m, k = a.shape
    n = w3.shape[2]
    return pl.pallas_call(
        _mm_kernel,
        grid=(m // tm, n // tn),
        in_specs=[
            pl.BlockSpec((tm, k), lambda i, j: (i, 0)),
            pl.BlockSpec((None, k, tn), lambda i, j: (l, 0, j)),
        ],
        out_specs=pl.BlockSpec((tm, tn), lambda i, j: (i, j)),
        out_shape=jax.ShapeDtypeStruct((m, n), out_dtype),
        compiler_params=_cparams(("parallel", "parallel")),
        name="in_proj",
    )(a, w3)


def _rope_block(a, cos, sin):
    return a * cos + pltpu.roll(a, LANES // 2, axis=1) * sin


def _norm_mm_kernel(a_ref, nw_ref, w_ref, *rest, rope, tn):
    if rope:
        cos_ref, sin_ref, o_ref, an_sc = rest
    else:
        o_ref, an_sc = rest

    @pl.when(pl.program_id(1) == 0)
    def _():
        a = a_ref[...].astype(F32)
        y = a * lax.rsqrt(jnp.mean(a * a, axis=-1, keepdims=True) + NORM_EPS)
        an_sc[...] = (y * nw_ref[...]).astype(BF16)

    acc = _dot(an_sc[...], w_ref[...])
    if rope:
        cos = cos_ref[...]
        sin = sin_ref[...]
        for hh in range(tn // (2 * LANES)):
            c0 = hh * 2 * LANES
            o_ref[:, c0:c0 + LANES] = acc[:, c0:c0 + LANES].astype(o_ref.dtype)
            o_ref[:, c0 + LANES:c0 + 2 * LANES] = _rope_block(
                acc[:, c0 + LANES:c0 + 2 * LANES], cos, sin).astype(o_ref.dtype)
    else:
        o_ref[...] = acc.astype(o_ref.dtype)


def _norm_matmul(proj, col_off, k, norm_w, w, l, tm, tn, cos=None, sin=None, name="norm_mm"):
    m = proj.shape[0]
    n = w.shape[2]
    rope = cos is not None
    assert col_off % k == 0
    in_specs = [
        pl.BlockSpec((tm, k), lambda i, j: (i, col_off // k)),
        pl.BlockSpec((1, k), lambda i, j: (0, 0)),
        pl.BlockSpec((None, k, tn), lambda i, j: (l, 0, j)),
    ]
    args = [proj, norm_w.reshape(1, k), w]
    if rope:
        in_specs += [pl.BlockSpec((tm, LANES), lambda i, j: (i, 0))] * 2
        args += [cos, sin]
    return pl.pallas_call(
        functools.partial(_norm_mm_kernel, rope=rope, tn=tn),
        grid=(m // tm, n // tn),
        in_specs=in_specs,
        out_specs=pl.BlockSpec((tm, tn), lambda i, j: (i, j)),
        out_shape=jax.ShapeDtypeStruct((m, n), BF16),
        scratch_shapes=[pltpu.VMEM((tm, k), BF16)],
        compiler_params=_cparams(("parallel", "arbitrary")),
        name=name,
    )(*args)


def _rope_tab_kernel(pos_ref, freq_ref, cos_ref, sin_ref):
    ang = pos_ref[...].astype(F32) * freq_ref[...]
    lane = lax.broadcasted_iota(jnp.int32, ang.shape, 1)
    keep = lane < MLA_QK_ROPE
    cos_ref[...] = jnp.where(keep, jnp.cos(ang), 0.0)
    sin_ref[...] = jnp.where(keep, jnp.sin(ang), 0.0)


def _rope_tables(positions):
    m = positions.size
    tm = min(1024, m)
    half = MLA_QK_ROPE // 2
    inv_freq = ROPE_THETA ** (-(jnp.arange(0, MLA_QK_ROPE, 2, dtype=F32) / MLA_QK_ROPE))
    freq = jnp.concatenate([inv_freq, inv_freq, jnp.zeros((LANES - 2 * half,), F32)]).reshape(1, LANES)
    return pl.pallas_call(
        _rope_tab_kernel,
        grid=(m // tm,),
        in_specs=[pl.BlockSpec((tm, 1), lambda i: (i, 0)),
                  pl.BlockSpec((1, LANES), lambda i: (0, 0))],
        out_specs=[pl.BlockSpec((tm, LANES), lambda i: (i, 0))] * 2,
        out_shape=[jax.ShapeDtypeStruct((m, LANES), F32)] * 2,
        compiler_params=_cparams(("parallel",)),
        name="rope_tables",
    )(positions.reshape(m, 1), freq)


def _krope_kernel(a_ref, cos_ref, sin_ref, o_ref):
    o_ref[...] = _rope_block(a_ref[...].astype(F32), cos_ref[...], sin_ref[...]).astype(BF16)


def _krope(proj, col_off, cos, sin):
    m = proj.shape[0]
    tm = min(1024, m)
    return pl.pallas_call(
        _krope_kernel,
        grid=(m // tm,),
        in_specs=[pl.BlockSpec((tm, LANES), lambda i: (i, col_off // LANES)),
                  pl.BlockSpec((tm, LANES), lambda i: (i, 0)),
                  pl.BlockSpec((tm, LANES), lambda i: (i, 0))],
        out_specs=pl.BlockSpec((tm, LANES), lambda i: (i, 0)),
        out_shape=jax.ShapeDtypeStruct((m, LANES), BF16),
        compiler_params=_cparams(("parallel",)),
        name="k_rope",
    )(proj, cos, sin)


def _ssd_pre_kernel(dt_ref, bias_ref, alog_ref, tri_ref, efull_ref,
                    dtb_ref, eacs_ref, toend_ref, acsrot_ref, acst_ref, decend_ref, *, tp, r):
    L = SSD_CHUNK
    dt = jax.nn.softplus(dt_ref[...].astype(F32) + bias_ref[...])
    dtb_ref[...] = dt.astype(BF16)
    a = dt * (-jnp.exp(alog_ref[...])) * LOG2_E
    tri = tri_ref[...]
    lasts = []
    for c in range(tp // L):
        rows = slice(c * L, (c + 1) * L)
        hi, mid, lo = _split3(a[rows])
        acs = _dot(tri, hi) + _dot(tri, mid) + _dot(tri, lo)
        last = acs[L - 1:L, :]
        lasts.append(last)
        eacs_ref[rows, :] = jnp.exp2(acs).astype(BF16)
        toend_ref[rows, :] = jnp.exp2(last - acs).astype(BF16)
        acst_ref[c] = acs.T
        for g in range(SSD_N_GROUPS):
            shift = (LANES - g * r) % LANES
            acsrot_ref[g, rows, :] = acs if shift == 0 else pltpu.roll(acs, shift, axis=1)
    n_c = tp // L
    pad = [jnp.zeros((SUBLANES - n_c % SUBLANES, LANES), F32)] if n_c % SUBLANES else []
    el = jnp.exp2(jnp.concatenate(lasts + pad, axis=0))
    hi, mid, lo = _split3(el)
    e = efull_ref[...]
    dec = _dot(hi, e) + _dot(mid, e) + _dot(lo, e)
    for c in range(n_c):
        decend_ref[c] = dec[c:c + 1, :]


def _ssd_pre(proj, col_off, dt_bias, a_log, d_inner, tp):
    m = proj.shape[0]
    nh = dt_bias.shape[0]
    assert nh == LANES and col_off % LANES == 0
    r = nh // SSD_N_GROUPS
    L = SSD_CHUNK
    tri = (jnp.arange(L)[:, None] >= jnp.arange(L)[None, :]).astype(BF16)
    efull = (jnp.arange(nh)[:, None] == (jnp.arange(d_inner)[None, :] // SSD_HEAD_DIM)).astype(BF16)
    nct = m // L
    return pl.pallas_call(
        functools.partial(_ssd_pre_kernel, tp=tp, r=r),
        grid=(m // tp,),
        in_specs=[
            pl.BlockSpec((tp, LANES), lambda i: (i, col_off // LANES)),
            pl.BlockSpec((1, LANES), lambda i: (0, 0)),
            pl.BlockSpec((1, LANES), lambda i: (0, 0)),
            pl.BlockSpec((L, L), lambda i: (0, 0)),
            pl.BlockSpec((nh, d_inner), lambda i: (0, 0)),
        ],
        out_specs=[
            pl.BlockSpec((tp, LANES), lambda i: (i, 0)),
            pl.BlockSpec((tp, LANES), lambda i: (i, 0)),
            pl.BlockSpec((tp, LANES), lambda i: (i, 0)),
            pl.BlockSpec((SSD_N_GROUPS, tp, LANES), lambda i: (0, i, 0)),
            pl.BlockSpec((tp // L, LANES, L), lambda i: (i, 0, 0)),
            pl.BlockSpec((tp // L, 1, d_inner), lambda i: (i, 0, 0)),
        ],
        out_shape=[
            jax.ShapeDtypeStruct((m, LANES), BF16),
            jax.ShapeDtypeStruct((m, LANES), BF16),
            jax.ShapeDtypeStruct((m, LANES), BF16),
            jax.ShapeDtypeStruct((SSD_N_GROUPS, m, LANES), F32),
            jax.ShapeDtypeStruct((nct, LANES, L), F32),
            jax.ShapeDtypeStruct((nct, 1, d_inner), F32),
        ],
        compiler_params=_cparams(("parallel",)),
        name="ssd_pre",
    )(proj, dt_bias.reshape(1, nh), a_log.reshape(1, nh), tri, efull)


def _ssd_kernel(xs_ref, bm_ref, cm_ref, z_ref, dtb_ref, eacs_ref, toend_ref, acs_ref, acst_ref,
                decend_ref, cwx_ref, cbx_ref, cwbc_ref, cbbc_ref, dskip_ref,
                nw_ref, e_ref, shift_ref, o_ref,
                xin_sc, bcin_sc, state_sc, *, t_blk, r):
    L = SSD_CHUNK
    K = SSD_CONV_WIDTH
    P = SSD_HEAD_DIM
    N = SSD_D_STATE
    gw = r * P
    halo = CONV_HALO

    @pl.when(pl.program_id(2) == 0)
    def _():
        state_sc[...] = jnp.zeros_like(state_sc)
        xin_sc[0:halo, :] = jnp.zeros((halo, gw), BF16)
        bcin_sc[0:halo, :] = jnp.zeros((halo, 2 * N), BF16)

    xin_sc[halo:halo + t_blk, :] = xs_ref[...]
    bcin_sc[halo:halo + t_blk, 0:N] = bm_ref[...]
    bcin_sc[halo:halo + t_blk, N:2 * N] = cm_ref[...]

    def conv(in_sc, rows_ext, w_ref, b_ref):
        xe = in_sc[rows_ext, :]
        sh = _dot(shift_ref[...], xe)
        acc = b_ref[...] + w_ref[K - 1:K, :] * xe[halo:halo + L, :].astype(F32)
        for k in range(K - 1):
            acc = acc + w_ref[k:k + 1, :] * sh[k * L:(k + 1) * L, :]
        return _silu(acc)

    li = lax.broadcasted_iota(jnp.int32, (L, L), 0)
    si = lax.broadcasted_iota(jnp.int32, (L, L), 1)
    causal = li >= si
    lane = lax.broadcasted_iota(jnp.int32, (L, 2 * P), 1)
    lo_half = lane < P

    def chunk(c, carry):
        rows = pl.ds(pl.multiple_of(c * L, L), L)
        rows_ext = pl.ds(pl.multiple_of(c * L, L), halo + L)
        xc = conv(xin_sc, rows_ext, cwx_ref, cbx_ref)
        bc_c = conv(bcin_sc, rows_ext, cwbc_ref, cbbc_ref).astype(BF16)
        b_c = bc_c[:, 0:N]
        c_c = bc_c[:, N:2 * N]
        e = e_ref[...]
        dt_e = _dot(dtb_ref[rows, :], e)
        eacs_e = _dot(eacs_ref[rows, :], e)
        toend_e = _dot(toend_ref[rows, :], e)
        xdt = xc * dt_e
        xdt_b = xdt.astype(BF16)
        xw_b = (xdt * toend_e).astype(BF16)
        cb = lax.dot_general(c_c, b_c, (((1,), (1,)), ((), ())), preferred_element_type=F32)
        acs = acs_ref[rows, :]
        acst = acst_ref[c]
        st = state_sc[...]
        y = _dot(c_c, st.astype(BF16)) * eacs_e
        pieces = []
        for jp in range(r // 2):
            xp = xdt_b[:, jp * 2 * P:(jp + 1) * 2 * P]
            acc = None
            for u in range(2):
                j = 2 * jp + u
                colb = jnp.broadcast_to(acs[:, j:j + 1], (L, L))
                rowb = jnp.broadcast_to(acst[j:j + 1, :], (L, L))
                dec = jnp.exp2(jnp.where(causal, colb - rowb, -jnp.inf))
                m_h = (cb * dec).astype(BF16)
                xm = jnp.where(lo_half if u == 0 else jnp.logical_not(lo_half), xp, jnp.zeros_like(xp))
                d = _dot(m_h, xm)
                acc = d if acc is None else acc + d
            pieces.append(acc)
        y = y + jnp.concatenate(pieces, axis=1)
        upd = lax.dot_general(b_c, xw_b, (((0,), (0,)), ((), ())), preferred_element_type=F32)
        state_sc[...] = st * decend_ref[c] + upd
        y = y + dskip_ref[...] * xc
        yg = y * _silu(z_ref[rows, :].astype(F32))
        ms = jnp.mean(yg * yg, axis=-1, keepdims=True)
        o_ref[rows, :] = (yg * lax.rsqrt(ms + NORM_EPS) * nw_ref[...]).astype(o_ref.dtype)
        return carry

    lax.fori_loop(0, t_blk // L, chunk, 0)
    xin_sc[0:halo, :] = xin_sc[t_blk:t_blk + halo, :]
    bcin_sc[0:halo, :] = bcin_sc[t_blk:t_blk + halo, :]


def _ssd(proj, offs, pre, conv_w, conv_b, d_skip, ssd_norm_w, batch, seq, t_blk):
    m = proj.shape[0]
    d_inner = ssd_norm_w.shape[0]
    nh = d_skip.shape[0]
    G = SSD_N_GROUPS
    N = SSD_D_STATE
    L = SSD_CHUNK
    r = nh // G
    gw = r * SSD_HEAD_DIM
    assert gw % LANES == 0 and r % 2 == 0 and r % SUBLANES == 0
    dtb, eacs, toend, acsrot, acst, decend = pre
    nt = seq // t_blk
    ncb = t_blk // L
    e_all = (jnp.arange(LANES)[None, :, None]
             == (jnp.arange(G)[:, None, None] * r + jnp.arange(gw)[None, None, :] // SSD_HEAD_DIM)).astype(BF16)
    dskip_e = jnp.repeat(d_skip.astype(F32), SSD_HEAD_DIM).reshape(1, d_inner)

    def bc_pairs(a):
        b_part = a[:, d_inner:d_inner + G * N].reshape(-1, G, 1, N)
        c_part = a[:, d_inner + G * N:].reshape(-1, G, 1, N)
        return jnp.concatenate([b_part, c_part], axis=2).reshape(-1, G * 2 * N)

    cb2 = conv_b.reshape(1, -1)
    cwx, cbx = conv_w[:, :d_inner], cb2[:, :d_inner]
    cwbc, cbbc = bc_pairs(conv_w), bc_pairs(cb2)
    kw = SSD_CONV_WIDTH
    tt = jnp.arange((kw - 1) * L)
    shift = (jnp.arange(CONV_HALO + L)[None, :]
             == (CONV_HALO + tt % L - (kw - 1) + tt // L)[:, None]).astype(BF16)
    for o in (offs["xs"], offs["z"]):
        assert o % gw == 0
    row = lambda b, g, t: b * nt + t
    in_specs = [
        pl.BlockSpec((t_blk, gw), lambda b, g, t: (row(b, g, t), offs["xs"] // gw + g)),
        pl.BlockSpec((t_blk, N), lambda b, g, t: (row(b, g, t), offs["bm"] // N + g)),
        pl.BlockSpec((t_blk, N), lambda b, g, t: (row(b, g, t), offs["cm"] // N + g)),
        pl.BlockSpec((t_blk, gw), lambda b, g, t: (row(b, g, t), offs["z"] // gw + g)),
        pl.BlockSpec((t_blk, LANES), lambda b, g, t: (row(b, g, t), 0)),
        pl.BlockSpec((t_blk, LANES), lambda b, g, t: (row(b, g, t), 0)),
        pl.BlockSpec((t_blk, LANES), lambda b, g, t: (row(b, g, t), 0)),
        pl.BlockSpec((None, t_blk, LANES), lambda b, g, t: (g, row(b, g, t), 0)),
        pl.BlockSpec((ncb, r, L), lambda b, g, t: (row(b, g, t), g, 0)),
        pl.BlockSpec((ncb, 1, gw), lambda b, g, t: (row(b, g, t), 0, g)),
        pl.BlockSpec((SSD_CONV_WIDTH, gw), lambda b, g, t: (0, g)),
        pl.BlockSpec((1, gw), lambda b, g, t: (0, g)),
        pl.BlockSpec((SSD_CONV_WIDTH, 2 * N), lambda b, g, t: (0, g)),
        pl.BlockSpec((1, 2 * N), lambda b, g, t: (0, g)),
        pl.BlockSpec((1, gw), lambda b, g, t: (0, g)),
        pl.BlockSpec((1, gw), lambda b, g, t: (0, g)),
        pl.BlockSpec((None, LANES, gw), lambda b, g, t: (g, 0, 0)),
        pl.BlockSpec(((kw - 1) * L, CONV_HALO + L), lambda b, g, t: (0, 0)),
    ]
    return pl.pallas_call(
        functools.partial(_ssd_kernel, t_blk=t_blk, r=r),
        grid=(batch, G, nt),
        in_specs=in_specs,
        out_specs=pl.BlockSpec((t_blk, gw), lambda b, g, t: (row(b, g, t), g)),
        out_shape=jax.ShapeDtypeStruct((m, d_inner), BF16),
        scratch_shapes=[
            pltpu.VMEM((t_blk + CONV_HALO, gw), BF16),
            pltpu.VMEM((t_blk + CONV_HALO, 2 * N), BF16),
            pltpu.VMEM((N, gw), F32),
        ],
        compiler_params=_cparams(("parallel", "parallel", "arbitrary")),
        name="ssd_scan",
    )(proj, proj, proj, proj, dtb, eacs, toend, acsrot, acst, decend,
      cwx, cbx, cwbc, cbbc, dskip_e, ssd_norm_w.reshape(1, d_inner), e_all, shift)


def _flash_kernel(q_ref, kn_ref, kr_ref, v_ref, g_ref, o_ref, vt_sc, qt_sc, m_sc, l_sc, acc_sc, *, tq, tk, seq):
    qi = pl.program_id(2)
    n_sub = tq // tk
    tqs = min(2 * LANES, tq)
    ahead = 2 * (tq // tqs)

    @pl.when(qi == 0)
    def _():
        for c in range(seq // tk):
            vt_sc[c] = v_ref[c * tk:(c + 1) * tk, :].astype(F32).T.astype(BF16)

    qt_sc[...] = q_ref[...].astype(F32).T.astype(BF16)
    m_sc[...] = jnp.full(m_sc.shape, -jnp.inf, F32)
    l_sc[...] = jnp.zeros(l_sc.shape, F32)
    acc_sc[...] = jnp.zeros(acc_sc.shape, F32)

    def process(blocks):
        chains = []
        for bi, (kb, diag) in enumerate(blocks):
            k0 = 0 if diag is None else diag * tk
            chains += [(bi, diag, k0, c0) for c0 in range(k0, tq, tqs)]
        loaded = {}

        def scores(bi, diag, k0, c0):
            if bi not in loaded:
                rows = pl.ds(pl.multiple_of(blocks[bi][0] * tk, tk), tk)
                loaded[bi] = jnp.concatenate([kn_ref[rows, :], kr_ref[rows, :]], axis=1)
            st = _dot(loaded[bi], qt_sc[:, c0:c0 + tqs])
            if diag is not None and c0 < k0 + tk:
                ki = lax.broadcasted_iota(jnp.int32, (tk, tqs), 0) + k0
                qj = lax.broadcasted_iota(jnp.int32, (tk, tqs), 1) + c0
                st = jnp.where(ki <= qj, st, -jnp.inf)
            return st

        sts = {i: scores(*chains[i]) for i in range(min(ahead, len(chains)))}
        for i, (bi, _, _, c0) in enumerate(chains):
            if i + ahead < len(chains):
                sts[i + ahead] = scores(*chains[i + ahead])
            st = sts.pop(i)
            vt = vt_sc[blocks[bi][0]]
            cs = slice(c0, c0 + tqs)
            m_prev = m_sc[:, cs]
            m_new = jnp.maximum(m_prev, jnp.max(st, axis=0, keepdims=True))
            alpha = jnp.exp2(m_prev - m_new)
            p = jnp.exp2(st - m_new)
            l_sc[:, cs] = alpha * l_sc[:, cs] + jnp.sum(p, axis=0, keepdims=True)
            acc_sc[:, cs] = acc_sc[:, cs] * alpha + _dot(vt, p.astype(BF16))
            m_sc[:, cs] = m_new

    def full_body(j, carry):
        process([(j * n_sub + d, None) for d in range(n_sub)])
        return carry

    lax.fori_loop(0, qi, full_body, 0)
    process([(qi * n_sub + d, d) for d in range(n_sub)])
    o = (acc_sc[...] / l_sc[...]).T
    o_ref[...] = (o * _silu(g_ref[...].astype(F32))).astype(o_ref.dtype)


def _flash(q, kv, kr, proj, gate_off, batch, seq, n_heads, tq, tk):
    m = q.shape[0]
    nq = seq // tq
    dq = 2 * LANES
    assert gate_off % LANES == 0
    return pl.pallas_call(
        functools.partial(_flash_kernel, tq=tq, tk=tk, seq=seq),
        grid=(batch, n_heads, nq),
        in_specs=[
            pl.BlockSpec((tq, dq), lambda b, h, i: (b * nq + i, h)),
            pl.BlockSpec((seq, LANES), lambda b, h, i: (b, 2 * h)),
            pl.BlockSpec((seq, LANES), lambda b, h, i: (b, 0)),
            pl.BlockSpec((seq, LANES), lambda b, h, i: (b, 2 * h + 1)),
            pl.BlockSpec((tq, LANES), lambda b, h, i: (b * nq + i, gate_off // LANES + h)),
        ],
        out_specs=pl.BlockSpec((tq, LANES), lambda b, h, i: (b * nq + i, h)),
        out_shape=jax.ShapeDtypeStruct((m, n_heads * MLA_V_DIM), BF16),
        scratch_shapes=[pltpu.VMEM((seq // tk, MLA_V_DIM, tk), BF16), pltpu.VMEM((dq, tq), BF16),
                        pltpu.VMEM((1, tq), F32), pltpu.VMEM((1, tq), F32),
                        pltpu.VMEM((MLA_V_DIM, tq), F32)],
        compiler_params=_cparams(("parallel", "parallel", "arbitrary")),
        name="mla_flash",
    )(q, kv, kr, kv, proj)


def _ssd_out_kernel(a_ref, w_ref, g_ref, o_ref):
    y = _dot(a_ref[...], w_ref[...])
    o_ref[...] = (jax.nn.sigmoid(g_ref[...].astype(F32)) * y).astype(o_ref.dtype)


def _ssd_out(ys, w_ssd, l, proj, gate_off, tm, tn):
    m, k1 = ys.shape
    d = w_ssd.shape[2]
    assert gate_off % tn == 0
    return pl.pallas_call(
        _ssd_out_kernel,
        grid=(m // tm, d // tn),
        in_specs=[
            pl.BlockSpec((tm, k1), lambda i, j: (i, 0)),
            pl.BlockSpec((None, k1, tn), lambda i, j: (l, 0, j)),
            pl.BlockSpec((tm, tn), lambda i, j: (i, gate_off // tn + j)),
        ],
        out_specs=pl.BlockSpec((tm, tn), lambda i, j: (i, j)),
        out_shape=jax.ShapeDtypeStruct((m, d), BF16),
        compiler_params=_cparams(("parallel", "parallel")),
        name="ssd_out",
    )(ys, w_ssd, proj)


def _mla_out_kernel(a_ref, w_ref, g_ref, p_ref, o_ref):
    y = _dot(a_ref[...], w_ref[...])
    o_ref[...] = (p_ref[...].astype(F32) + jax.nn.sigmoid(g_ref[...].astype(F32)) * y).astype(o_ref.dtype)


def _mla_out(og, w_mla, l, proj, gate_off, part, tm, tn):
    m, k2 = og.shape
    d = w_mla.shape[2]
    assert gate_off % tn == 0
    return pl.pallas_call(
        _mla_out_kernel,
        grid=(m // tm, d // tn),
        in_specs=[
            pl.BlockSpec((tm, k2), lambda i, j: (i, 0)),
            pl.BlockSpec((None, k2, tn), lambda i, j: (l, 0, j)),
            pl.BlockSpec((tm, tn), lambda i, j: (i, gate_off // tn + j)),
            pl.BlockSpec((tm, tn), lambda i, j: (i, j)),
        ],
        out_specs=pl.BlockSpec((tm, tn), lambda i, j: (i, j)),
        out_shape=jax.ShapeDtypeStruct((m, d), BF16),
        compiler_params=_cparams(("parallel", "parallel")),
        name="mla_out",
    )(og, w_mla, proj, part)


def _out_kernel(a_ref, w_ref, x_ref, pw_ref, gate_ref, o_ref, *, nk):
    k = pl.program_id(1)

    @pl.when(k == 0)
    def _():
        o_ref[...] = jnp.zeros_like(o_ref)

    o_ref[...] += _dot(a_ref[...], w_ref[...])

    @pl.when(k == nk - 1)
    def _():
        y = o_ref[...]
        yn = y * lax.rsqrt(jnp.mean(y * y, axis=-1, keepdims=True) + NORM_EPS) * pw_ref[...]
        o_ref[...] = x_ref[...] + gate_ref[...] * yn


def _out_proj(merged, w_out, l, x2, post_w, mod4, seq, tm, tk):
    m, d = x2.shape
    nk = d // tk
    per_b = seq // tm
    return pl.pallas_call(
        functools.partial(_out_kernel, nk=nk),
        grid=(m // tm, nk),
        in_specs=[
            pl.BlockSpec((tm, tk), lambda i, k: (i, k)),
            pl.BlockSpec((None, tk, d), lambda i, k: (l, k, 0)),
            pl.BlockSpec((tm, d), lambda i, k: (i, 0)),
            pl.BlockSpec((1, d), lambda i, k: (0, 0)),
            pl.BlockSpec((None, None, 1, d), lambda i, k: (i // per_b, 2, 0, 0)),
        ],
        out_specs=pl.BlockSpec((tm, d), lambda i, k: (i, 0)),
        out_shape=jax.ShapeDtypeStruct((m, d), F32),
        compiler_params=_cparams(("parallel", "arbitrary")),
        name="out_proj",
    )(merged, w_out, x2, post_w.reshape(1, d), mod4)


def _proj_layout(d, d_inner, nh, n_mla_heads, q_lora, kv_lora):
    gn = SSD_N_GROUPS * SSD_D_STATE
    widths = [("z", d_inner), ("xs", d_inner), ("bm", gn), ("cm", gn),
              ("gate", n_mla_heads * MLA_V_DIM), ("merge", 2 * d), ("cq", q_lora), ("ckv", kv_lora),
              ("kr", 2 * MLA_QK_ROPE), ("dt", nh)]
    offs, o = {}, 0
    for name, w in widths:
        offs[name] = o
        o += w
    return offs, o


def _cast_kernel(w_ref, o_ref):
    o_ref[...] = w_ref[...].astype(o_ref.dtype)


def _cast_bf16(w3):
    depth, k, n = w3.shape
    tr = max(2 * SUBLANES, min(k, CAST_BLOCK_BYTES // (n * 4)))
    assert k % tr == 0
    return pl.pallas_call(
        _cast_kernel,
        grid=(depth, k // tr),
        in_specs=[pl.BlockSpec((None, tr, n), lambda l, r: (l, r, 0))],
        out_specs=pl.BlockSpec((None, tr, n), lambda l, r: (l, r, 0)),
        out_shape=jax.ShapeDtypeStruct((depth, k, n), BF16),
        compiler_params=_cparams(("parallel", "parallel")),
        name="cast_bf16",
    )(w3)


def _w_in_kernel(src_ref, mode_ref, *refs, n_chunks):
    o_ref = refs[n_chunks]
    j = pl.program_id(1)
    half = MLA_QK_ROPE // 2
    parts = []
    for c in range(n_chunks):
        blk = refs[c][...]
        mode = mode_ref[j * n_chunks + c]
        rot = jnp.concatenate([-blk[half:], blk[:half]], axis=0)
        v = jnp.where(mode == W_IN_ROTATE, rot, blk)
        parts.append(jnp.where(mode == W_IN_ZERO, 0.0, v))
    o_ref[...] = jnp.concatenate(parts, axis=0).T.astype(BF16)


def _prep_w_in(w_in3, offs, n_tot, n_pad, d, d_inner, nh, n_mla_heads, q_lora, kv_lora):
    depth, k, n_src = w_in3.shape
    gn = SSD_N_GROUPS * SSD_D_STATE
    conv_dim = d_inner + 2 * gn
    s_xbc = d_inner
    s_dt = s_xbc + conv_dim
    s_cq = s_dt + nh
    s_ckv = s_cq + q_lora
    s_kr = s_ckv + kv_lora
    s_gate = s_kr + MLA_QK_ROPE
    s_merge = s_gate + n_mla_heads * MLA_V_DIM
    assert s_merge + 2 * d == n_src and MLA_QK_ROPE == W_IN_CHUNK
    segments = (
        (offs["z"], 0, d_inner + conv_dim, W_IN_COPY),
        (offs["gate"], s_gate, n_mla_heads * MLA_V_DIM + 2 * d, W_IN_COPY),
        (offs["cq"], s_cq, q_lora + kv_lora, W_IN_COPY),
        (offs["kr"], s_kr, MLA_QK_ROPE, W_IN_COPY),
        (offs["kr"] + MLA_QK_ROPE, s_kr, MLA_QK_ROPE, W_IN_ROTATE),
        (offs["dt"], s_dt, nh, W_IN_COPY),
    )
    n_dst = n_pad // W_IN_CHUNK
    src_tab = [0] * n_dst
    mode_tab = [W_IN_ZERO] * n_dst
    for dst, src, width, mode in segments:
        assert dst % W_IN_CHUNK == 0 and src % W_IN_CHUNK == 0 and width % W_IN_CHUNK == 0
        for t in range(width // W_IN_CHUNK):
            src_tab[dst // W_IN_CHUNK + t] = src // W_IN_CHUNK + t
            mode_tab[dst // W_IN_CHUNK + t] = mode
    n_chunks = W_IN_COLS // W_IN_CHUNK
    w_t = jnp.swapaxes(w_in3, 1, 2)

    def chunk_spec(c):
        return pl.BlockSpec((None, W_IN_CHUNK, k), lambda l, j, src, mode: (l, src[j * n_chunks + c], 0))

    return pl.pallas_call(
        functools.partial(_w_in_kernel, n_chunks=n_chunks),
        grid_spec=pltpu.PrefetchScalarGridSpec(
            num_scalar_prefetch=2,
            grid=(depth, n_pad // W_IN_COLS),
            in_specs=[chunk_spec(c) for c in range(n_chunks)],
            out_specs=pl.BlockSpec((None, k, W_IN_COLS), lambda l, j, src, mode: (l, 0, j)),
        ),
        out_shape=jax.ShapeDtypeStruct((depth, k, n_pad), BF16),
        compiler_params=_cparams(("parallel", "parallel")),
        name="w_in_prep",
    )(jnp.asarray(src_tab, jnp.int32), jnp.asarray(mode_tab, jnp.int32), *([w_t] * n_chunks))


def _prep_w_q(w_q_up3, n_mla_heads):
    depth, k, _ = w_q_up3.shape
    qk = MLA_QK_NOPE + MLA_QK_ROPE
    half = MLA_QK_ROPE // 2
    w = w_q_up3.reshape(depth, k, n_mla_heads, qk) * (qk ** -0.5 * LOG2_E)
    nope, rope = w[..., :MLA_QK_NOPE], w[..., MLA_QK_NOPE:]
    rot = jnp.concatenate([-rope[..., half:], rope[..., :half]], axis=-1)
    return jnp.concatenate([nope, rope, rot], axis=-1).reshape(
        depth, k, n_mla_heads * 2 * LANES).astype(BF16)


def _round_up(v, mult):
    return (v + mult - 1) // mult * mult


def kernel(x, c, positions, ada_w, ada_b, pre_norm_w, post_norm_w, w_in, conv_w, conv_b, dt_bias,
           a_log, d_skip, ssd_norm_w, q_norm_w, w_q_up, kv_norm_w, w_kv_up, w_ssd_proj, w_mla_proj,
           w_out):
    batch, seq, d = x.shape
    depth = ada_w.shape[0]
    m = batch * seq
    d_inner = ssd_norm_w.shape[1]
    nh = dt_bias.shape[1]
    q_lora = q_norm_w.shape[1]
    kv_lora = kv_norm_w.shape[1]
    n_mla = w_mla_proj.shape[1] // MLA_V_DIM
    offs, n_tot = _proj_layout(d, d_inner, nh, n_mla, q_lora, kv_lora)

    tm_big = min(1024, m)
    tn_proj = 1024
    n_pad = _round_up(n_tot, tn_proj)

    c_pad = jnp.zeros((SUBLANES, d), F32).at[:batch].set(c)
    mod = _ada(c_pad, ada_w, ada_b)
    cos, sin = _rope_tables(positions)

    w_in_b = _prep_w_in(w_in, offs, n_tot, n_pad, d, d_inner, nh, n_mla, q_lora, kv_lora)
    w_q_b = _prep_w_q(w_q_up, n_mla)
    w_kv_b = _cast_bf16(w_kv_up)
    w_ssd_b = _cast_bf16(w_ssd_proj)
    w_mla_b = _cast_bf16(w_mla_proj)
    w_out_b = _cast_bf16(w_out)

    x2 = x.reshape(m, d)
    for l in range(depth):
        mod4 = mod[l, :batch].reshape(batch, 3, 1, d)
        h = _prenorm(x2, pre_norm_w[l], mod4, seq)
        proj = _matmul(h, w_in_b, l, BF16, tm_big, tn_proj)

        pre = _ssd_pre(proj, offs["dt"], dt_bias[l], a_log[l], d_inner, tp=min(1024, seq))
        ys = _ssd(proj, offs, pre, conv_w[l], conv_b[l], d_skip[l], ssd_norm_w[l], batch, seq,
                  t_blk=min(512, seq))

        q = _norm_matmul(proj, offs["cq"], q_lora, q_norm_w[l], w_q_b, l,
                         tm_big, min(1024, n_mla * 2 * LANES), cos, sin, name="q_up")
        kv = _norm_matmul(proj, offs["ckv"], kv_lora, kv_norm_w[l], w_kv_b, l,
                          tm_big, min(1024, n_mla * 2 * LANES), name="kv_up")
        kr = _krope(proj, offs["kr"], cos, sin)
        tq = min(1024, seq)
        og = _flash(q, kv, kr, proj, offs["gate"], batch, seq, n_mla, tq, min(512, tq))

        part = _ssd_out(ys, w_ssd_b, l, proj, offs["merge"], tm_big, min(512, d))
        merged = _mla_out(og, w_mla_b, l, proj, offs["merge"] + d, part, tm_big, min(1024, d))
        x2 = _out_proj(merged, w_out_b, l, x2, post_norm_w[l], mod4, seq,
                       min(512, seq), min(512, d))
    return x2.reshape(batch, seq, d)
```

```python
import functools
import math

import jax
import jax.numpy as jnp
from jax import lax
from jax.experimental import pallas as pl
from jax.experimental.pallas import tpu as pltpu

SSD_HEAD_DIM = 64
SSD_N_GROUPS = 8
SSD_D_STATE = 128
SSD_CHUNK = 128
SSD_CONV_WIDTH = 4
MLA_QK_NOPE = 128
MLA_QK_ROPE = 64
MLA_V_DIM = 128
ROPE_THETA = 10000.0
NORM_EPS = 1e-6
LOG2_E = math.log2(math.e)

LANES = 128
SUBLANES = 8
CONV_HALO = 16
VMEM_LIMIT = 56 * 1024 * 1024
CAST_BLOCK_BYTES = 8 * 1024 * 1024
W_IN_CHUNK = 64
W_IN_COLS = 512
W_IN_COPY, W_IN_ROTATE, W_IN_ZERO = 0, 1, 2

F32 = jnp.float32
BF16 = jnp.bfloat16


def _cparams(sem):
    return pltpu.CompilerParams(dimension_semantics=sem, vmem_limit_bytes=VMEM_LIMIT)


def _silu(v):
    h = 0.5 * v
    return h + h * jnp.tanh(h)


def _split3(v):
    hi = v.astype(BF16)
    r1 = v - hi.astype(F32)
    mid = r1.astype(BF16)
    lo = (r1 - mid.astype(F32)).astype(BF16)
    return hi, mid, lo


def _dot(a, b):
    return jnp.dot(a, b, preferred_element_type=F32)


def _ada_kernel(c_ref, w_ref, b_ref, o_ref):
    @pl.when(pl.program_id(1) == 0)
    def _():
        o_ref[...] = jnp.broadcast_to(b_ref[...], o_ref.shape)

    a = _silu(c_ref[...]).astype(BF16)
    o_ref[...] += _dot(a, w_ref[...].astype(BF16))


def _ada(c_pad, ada_w, ada_b):
    depth, d, n = ada_w.shape
    tk = min(256, d)
    return pl.pallas_call(
        _ada_kernel,
        grid=(depth, d // tk),
        in_specs=[
            pl.BlockSpec((SUBLANES, tk), lambda l, k: (0, k)),
            pl.BlockSpec((None, tk, n), lambda l, k: (l, k, 0)),
            pl.BlockSpec((None, 1, n), lambda l, k: (l, 0, 0)),
        ],
        out_specs=pl.BlockSpec((None, SUBLANES, n), lambda l, k: (l, 0, 0)),
        out_shape=jax.ShapeDtypeStruct((depth, SUBLANES, n), F32),
        compiler_params=_cparams(("parallel", "arbitrary")),
        name="ada_mod",
    )(c_pad, ada_w, ada_b.reshape(depth, 1, n))


def _prenorm_kernel(x_ref, w_ref, shift_ref, scale_ref, o_ref):
    x = x_ref[...]
    y = x * lax.rsqrt(jnp.mean(x * x, axis=-1, keepdims=True) + NORM_EPS)
    o_ref[...] = (y * w_ref[...] * (1.0 + scale_ref[...]) + shift_ref[...]).astype(BF16)


def _prenorm(x2, w, mod4, seq):
    m, d = x2.shape
    tm = min(512, seq)
    per_b = seq // tm
    return pl.pallas_call(
        _prenorm_kernel,
        grid=(m // tm,),
        in_specs=[
            pl.BlockSpec((tm, d), lambda i: (i, 0)),
            pl.BlockSpec((1, d), lambda i: (0, 0)),
            pl.BlockSpec((None, None, 1, d), lambda i: (i // per_b, 0, 0, 0)),
            pl.BlockSpec((None, None, 1, d), lambda i: (i // per_b, 1, 0, 0)),
        ],
        out_specs=pl.BlockSpec((tm, d), lambda i: (i, 0)),
        out_shape=jax.ShapeDtypeStruct((m, d), BF16),
        compiler_params=_cparams(("parallel",)),
        name="prenorm",
    )(x2, w.reshape(1, d), mod4, mod4)


def _mm_kernel(a_ref, w_ref, o_ref):
    o_ref[...] = _dot(a_ref[...], w_ref[...]).astype(o_ref.dtype)


def _matmul(a, w3, l, out_dtype, tm, tn):
    m, k = a.shape
    n = w3.shape[2]
    return pl.pallas_call(
        _mm_kernel,
        grid=(m // tm, n // tn),
        in_specs=[
            pl.BlockSpec((tm, k), lambda i, j: (i, 0)),
            pl.BlockSpec((None, k, tn), lambda i, j: (l, 0, j)),
        ],
        out_specs=pl.BlockSpec((tm, tn), lambda i, j: (i, j)),
        out_shape=jax.ShapeDtypeStruct((m, n), out_dtype),
        compiler_params=_cparams(("parallel", "parallel")),
        name="in_proj",
    )(a, w3)


def _rope_block(a, cos, sin):
    return a * cos + pltpu.roll(a, LANES // 2, axis=1) * sin


def _norm_mm_kernel(a_ref, nw_ref, w_ref, *rest, rope, tn):
    if rope:
        cos_ref, sin_ref, o_ref, an_sc = rest
    else:
        o_ref, an_sc = rest

    @pl.when(pl.program_id(1) == 0)
    def _():
        a = a_ref[...].astype(F32)
        y = a * lax.rsqrt(jnp.mean(a * a, axis=-1, keepdims=True) + NORM_EPS)
        an_sc[...] = (y * nw_ref[...]).astype(BF16)

    acc = _dot(an_sc[...], w_ref[...])
    if rope:
        cos = cos_ref[...]
        sin = sin_ref[...]
        for hh in range(tn // (2 * LANES)):
            c0 = hh * 2 * LANES
            o_ref[:, c0:c0 + LANES] = acc[:, c0:c0 + LANES].astype(o_ref.dtype)
            o_ref[:, c0 + LANES:c0 + 2 * LANES] = _rope_block(
                acc[:, c0 + LANES:c0 + 2 * LANES], cos, sin).astype(o_ref.dtype)
    else:
        o_ref[...] = acc.astype(o_ref.dtype)


def _norm_matmul(proj, col_off, k, norm_w, w, l, tm, tn, cos=None, sin=None, name="norm_mm"):
    m = proj.shape[0]
    n = w.shape[2]
    rope = cos is not None
    assert col_off % k == 0
    in_specs = [
        pl.BlockSpec((tm, k), lambda i, j: (i, col_off // k)),
        pl.BlockSpec((1, k), lambda i, j: (0, 0)),
        pl.BlockSpec((None, k, tn), lambda i, j: (l, 0, j)),
    ]
    args = [proj, norm_w.reshape(1, k), w]
    if rope:
        in_specs += [pl.BlockSpec((tm, LANES), lambda i, j: (i, 0))] * 2
        args += [cos, sin]
    return pl.pallas_call(
        functools.partial(_norm_mm_kernel, rope=rope, tn=tn),
        grid=(m // tm, n // tn),
        in_specs=in_specs,
        out_specs=pl.BlockSpec((tm, tn), lambda i, j: (i, j)),
        out_shape=jax.ShapeDtypeStruct((m, n), BF16),
        scratch_shapes=[pltpu.VMEM((tm, k), BF16)],
        compiler_params=_cparams(("parallel", "arbitrary")),
        name=name,
    )(*args)


def _rope_tab_kernel(pos_ref, freq_ref, cos_ref, sin_ref):
    ang = pos_ref[...].astype(F32) * freq_ref[...]
    lane = lax.broadcasted_iota(jnp.int32, ang.shape, 1)
    keep = lane < MLA_QK_ROPE
    cos_ref[...] = jnp.where(keep, jnp.cos(ang), 0.0)
    sin_ref[...] = jnp.where(keep, jnp.sin(ang), 0.0)


def _rope_tables(positions):
    m = positions.size
    tm = min(1024, m)
    half = MLA_QK_ROPE // 2
    inv_freq = ROPE_THETA ** (-(jnp.arange(0, MLA_QK_ROPE, 2, dtype=F32) / MLA_QK_ROPE))
    freq = jnp.concatenate([inv_freq, inv_freq, jnp.zeros((LANES - 2 * half,), F32)]).reshape(1, LANES)
    return pl.pallas_call(
        _rope_tab_kernel,
        grid=(m // tm,),
        in_specs=[pl.BlockSpec((tm, 1), lambda i: (i, 0)),
                  pl.BlockSpec((1, LANES), lambda i: (0, 0))],
        out_specs=[pl.BlockSpec((tm, LANES), lambda i: (i, 0))] * 2,
        out_shape=[jax.ShapeDtypeStruct((m, LANES), F32)] * 2,
        compiler_params=_cparams(("parallel",)),
        name="rope_tables",
    )(positions.reshape(m, 1), freq)


def _krope_kernel(a_ref, cos_ref, sin_ref, o_ref):
    o_ref[...] = _rope_block(a_ref[...].astype(F32), cos_ref[...], sin_ref[...]).astype(BF16)


def _krope(proj, col_off, cos, sin):
    m = proj.shape[0]
    tm = min(1024, m)
    return pl.pallas_call(
        _krope_kernel,
        grid=(m // tm,),
        in_specs=[pl.BlockSpec((tm, LANES), lambda i: (i, col_off // LANES)),
                  pl.BlockSpec((tm, LANES), lambda i: (i, 0)),
                  pl.BlockSpec((tm, LANES), lambda i: (i, 0))],
        out_specs=pl.BlockSpec((tm, LANES), lambda i: (i, 0)),
        out_shape=jax.ShapeDtypeStruct((m, LANES), BF16),
        compiler_params=_cparams(("parallel",)),
        name="k_rope",
    )(proj, cos, sin)


def _ssd_pre_kernel(dt_ref, bias_ref, alog_ref, tri_ref, efull_ref,
                    dtb_ref, eacs_ref, toend_ref, acsrot_ref, acst_ref, decend_ref, *, tp, r):
    L = SSD_CHUNK
    dt = jax.nn.softplus(dt_ref[...].astype(F32) + bias_ref[...])
    dtb_ref[...] = dt.astype(BF16)
    a = dt * (-jnp.exp(alog_ref[...])) * LOG2_E
    tri = tri_ref[...]
    lasts = []
    for c in range(tp // L):
        rows = slice(c * L, (c + 1) * L)
        hi, mid, lo = _split3(a[rows])
        acs = _dot(tri, hi) + _dot(tri, mid) + _dot(tri, lo)
        last = acs[L - 1:L, :]
        lasts.append(last)
        eacs_ref[rows, :] = jnp.exp2(acs).astype(BF16)
        toend_ref[rows, :] = jnp.exp2(last - acs).astype(BF16)
        acst_ref[c] = acs.T
        for g in range(SSD_N_GROUPS):
            shift = (LANES - g * r) % LANES
            acsrot_ref[g, rows, :] = acs if shift == 0 else pltpu.roll(acs, shift, axis=1)
    n_c = tp // L
    pad = [jnp.zeros((SUBLANES - n_c % SUBLANES, LANES), F32)] if n_c % SUBLANES else []
    el = jnp.exp2(jnp.concatenate(lasts + pad, axis=0))
    hi, mid, lo = _split3(el)
    e = efull_ref[...]
    dec = _dot(hi, e) + _dot(mid, e) + _dot(lo, e)
    for c in range(n_c):
        decend_ref[c] = dec[c:c + 1, :]


def _ssd_pre(proj, col_off, dt_bias, a_log, d_inner, tp):
    m = proj.shape[0]
    nh = dt_bias.shape[0]
    assert nh == LANES and col_off % LANES == 0
    r = nh // SSD_N_GROUPS
    L = SSD_CHUNK
    tri = (jnp.arange(L)[:, None] >= jnp.arange(L)[None, :]).astype(BF16)
    efull = (jnp.arange(nh)[:, None] == (jnp.arange(d_inner)[None, :] // SSD_HEAD_DIM)).astype(BF16)
    nct = m // L
    return pl.pallas_call(
        functools.partial(_ssd_pre_kernel, tp=tp, r=r),
        grid=(m // tp,),
        in_specs=[
            pl.BlockSpec((tp, LANES), lambda i: (i, col_off // LANES)),
            pl.BlockSpec((1, LANES), lambda i: (0, 0)),
            pl.BlockSpec((1, LANES), lambda i: (0, 0)),
            pl.BlockSpec((L, L), lambda i: (0, 0)),
            pl.BlockSpec((nh, d_inner), lambda i: (0, 0)),
        ],
        out_specs=[
            pl.BlockSpec((tp, LANES), lambda i: (i, 0)),
            pl.BlockSpec((tp, LANES), lambda i: (i, 0)),
            pl.BlockSpec((tp, LANES), lambda i: (i, 0)),
            pl.BlockSpec((SSD_N_GROUPS, tp, LANES), lambda i: (0, i, 0)),
            pl.BlockSpec((tp // L, LANES, L), lambda i: (i, 0, 0)),
            pl.BlockSpec((tp // L, 1, d_inner), lambda i: (i, 0, 0)),
        ],
        out_shape=[
            jax.ShapeDtypeStruct((m, LANES), BF16),
            jax.ShapeDtypeStruct((m, LANES), BF16),
            jax.ShapeDtypeStruct((m, LANES), BF16),
            jax.ShapeDtypeStruct((SSD_N_GROUPS, m, LANES), F32),
            jax.ShapeDtypeStruct((nct, LANES, L), F32),
            jax.ShapeDtypeStruct((nct, 1, d_inner), F32),
        ],
        compiler_params=_cparams(("parallel",)),
        name="ssd_pre",
    )(proj, dt_bias.reshape(1, nh), a_log.reshape(1, nh), tri, efull)


def _ssd_kernel(xs_ref, bm_ref, cm_ref, z_ref, dtb_ref, eacs_ref, toend_ref, acs_ref, acst_ref,
                decend_ref, cwx_ref, cbx_ref, cwbc_ref, cbbc_ref, dskip_ref,
                nw_ref, e_ref, shift_ref, o_ref,
                xin_sc, bcin_sc, state_sc, *, t_blk, r):
    L = SSD_CHUNK
    K = SSD_CONV_WIDTH
    P = SSD_HEAD_DIM
    N = SSD_D_STATE
    gw = r * P
    halo = CONV_HALO

    @pl.when(pl.program_id(2) == 0)
    def _():
        state_sc[...] = jnp.zeros_like(state_sc)
        xin_sc[0:halo, :] = jnp.zeros((halo, gw), BF16)
        bcin_sc[0:halo, :] = jnp.zeros((halo, 2 * N), BF16)

    xin_sc[halo:halo + t_blk, :] = xs_ref[...]
    bcin_sc[halo:halo + t_blk, 0:N] = bm_ref[...]
    bcin_sc[halo:halo + t_blk, N:2 * N] = cm_ref[...]

    def conv(in_sc, rows_ext, w_ref, b_ref):
        xe = in_sc[rows_ext, :]
        sh = _dot(shift_ref[...], xe)
        acc = b_ref[...] + w_ref[K - 1:K, :] * xe[halo:halo + L, :].astype(F32)
        for k in range(K - 1):
            acc = acc + w_ref[k:k + 1, :] * sh[k * L:(k + 1) * L, :]
        return _silu(acc)

    li = lax.broadcasted_iota(jnp.int32, (L, L), 0)
    si = lax.broadcasted_iota(jnp.int32, (L, L), 1)
    causal = li >= si
    lane = lax.broadcasted_iota(jnp.int32, (L, 2 * P), 1)
    lo_half = lane < P

    def chunk(c, carry):
        rows = pl.ds(pl.multiple_of(c * L, L), L)
        rows_ext = pl.ds(pl.multiple_of(c * L, L), halo + L)
        xc = conv(xin_sc, rows_ext, cwx_ref, cbx_ref)
        bc_c = conv(bcin_sc, rows_ext, cwbc_ref, cbbc_ref).astype(BF16)
        b_c = bc_c[:, 0:N]
        c_c = bc_c[:, N:2 * N]
        e = e_ref[...]
        dt_e = _dot(dtb_ref[rows, :], e)
        eacs_e = _dot(eacs_ref[rows, :], e)
        toend_e = _dot(toend_ref[rows, :], e)
        xdt = xc * dt_e
        xdt_b = xdt.astype(BF16)
        xw_b = (xdt * toend_e).astype(BF16)
        cb = lax.dot_general(c_c, b_c, (((1,), (1,)), ((), ())), preferred_element_type=F32)
        acs = acs_ref[rows, :]
        acst = acst_ref[c]
        st = state_sc[...]
        y = _dot(c_c, st.astype(BF16)) * eacs_e
        pieces = []
        for jp in range(r // 2):
            xp = xdt_b[:, jp * 2 * P:(jp + 1) * 2 * P]
            acc = None
            for u in range(2):
                j = 2 * jp + u
                colb = jnp.broadcast_to(acs[:, j:j + 1], (L, L))
                rowb = jnp.broadcast_to(acst[j:j + 1, :], (L, L))
                dec = jnp.exp2(jnp.where(causal, colb - rowb, -jnp.inf))
                m_h = (cb * dec).astype(BF16)
                xm = jnp.where(lo_half if u == 0 else jnp.logical_not(lo_half), xp, jnp.zeros_like(xp))
                d = _dot(m_h, xm)
                acc = d if acc is None else acc + d
            pieces.append(acc)
        y = y + jnp.concatenate(pieces, axis=1)
        upd = lax.dot_general(b_c, xw_b, (((0,), (0,)), ((), ())), preferred_element_type=F32)
        state_sc[...] = st * decend_ref[c] + upd
        y = y + dskip_ref[...] * xc
        yg = y * _silu(z_ref[rows, :].astype(F32))
        ms = jnp.mean(yg * yg, axis=-1, keepdims=True)
        o_ref[rows, :] = (yg * lax.rsqrt(ms + NORM_EPS) * nw_ref[...]).astype(o_ref.dtype)
        return carry

    lax.fori_loop(0, t_blk // L, chunk, 0)
    xin_sc[0:halo, :] = xin_sc[t_blk:t_blk + halo, :]
    bcin_sc[0:halo, :] = bcin_sc[t_blk:t_blk + halo, :]


def _ssd(proj, offs, pre, conv_w, conv_b, d_skip, ssd_norm_w, batch, seq, t_blk):
    m = proj.shape[0]
    d_inner = ssd_norm_w.shape[0]
    nh = d_skip.shape[0]
    G = SSD_N_GROUPS
    N = SSD_D_STATE
    L = SSD_CHUNK
    r = nh // G
    gw = r * SSD_HEAD_DIM
    assert gw % LANES == 0 and r % 2 == 0 and r % SUBLANES == 0
    dtb, eacs, toend, acsrot, acst, decend = pre
    nt = seq // t_blk
    ncb = t_blk // L
    e_all = (jnp.arange(LANES)[None, :, None]
             == (jnp.arange(G)[:, None, None] * r + jnp.arange(gw)[None, None, :] // SSD_HEAD_DIM)).astype(BF16)
    dskip_e = jnp.repeat(d_skip.astype(F32), SSD_HEAD_DIM).reshape(1, d_inner)

    def bc_pairs(a):
        b_part = a[:, d_inner:d_inner + G * N].reshape(-1, G, 1, N)
        c_part = a[:, d_inner + G * N:].reshape(-1, G, 1, N)
        return jnp.concatenate([b_part, c_part], axis=2).reshape(-1, G * 2 * N)

    cb2 = conv_b.reshape(1, -1)
    cwx, cbx = conv_w[:, :d_inner], cb2[:, :d_inner]
    cwbc, cbbc = bc_pairs(conv_w), bc_pairs(cb2)
    kw = SSD_CONV_WIDTH
    tt = jnp.arange((kw - 1) * L)
    shift = (jnp.arange(CONV_HALO + L)[None, :]
             == (CONV_HALO + tt % L - (kw - 1) + tt // L)[:, None]).astype(BF16)
    for o in (offs["xs"], offs["z"]):
        assert o % gw == 0
    row = lambda b, g, t: b * nt + t
    in_specs = [
        pl.BlockSpec((t_blk, gw), lambda b, g, t: (row(b, g, t), offs["xs"] // gw + g)),
        pl.BlockSpec((t_blk, N), lambda b, g, t: (row(b, g, t), offs["bm"] // N + g)),
        pl.BlockSpec((t_blk, N), lambda b, g, t: (row(b, g, t), offs["cm"] // N + g)),
        pl.BlockSpec((t_blk, gw), lambda b, g, t: (row(b, g, t), offs["z"] // gw + g)),
        pl.BlockSpec((t_blk, LANES), lambda b, g, t: (row(b, g, t), 0)),
        pl.BlockSpec((t_blk, LANES), lambda b, g, t: (row(b, g, t), 0)),
        pl.BlockSpec((t_blk, LANES), lambda b, g, t: (row(b, g, t), 0)),
        pl.BlockSpec((None, t_blk, LANES), lambda b, g, t: (g, row(b, g, t), 0)),
        pl.BlockSpec((ncb, r, L), lambda b, g, t: (row(b, g, t), g, 0)),
        pl.BlockSpec((ncb, 1, gw), lambda b, g, t: (row(b, g, t), 0, g)),
        pl.BlockSpec((SSD_CONV_WIDTH, gw), lambda b, g, t: (0, g)),
        pl.BlockSpec((1, gw), lambda b, g, t: (0, g)),
        pl.BlockSpec((SSD_CONV_WIDTH, 2 * N), lambda b, g, t: (0, g)),
        pl.BlockSpec((1, 2 * N), lambda b, g, t: (0, g)),
        pl.BlockSpec((1, gw), lambda b, g, t: (0, g)),
        pl.BlockSpec((1, gw), lambda b, g, t: (0, g)),
        pl.BlockSpec((None, LANES, gw), lambda b, g, t: (g, 0, 0)),
        pl.BlockSpec(((kw - 1) * L, CONV_HALO + L), lambda b, g, t: (0, 0)),
    ]
    return pl.pallas_call(
        functools.partial(_ssd_kernel, t_blk=t_blk, r=r),
        grid=(batch, G, nt),
        in_specs=in_specs,
        out_specs=pl.BlockSpec((t_blk, gw), lambda b, g, t: (row(b, g, t), g)),
        out_shape=jax.ShapeDtypeStruct((m, d_inner), BF16),
        scratch_shapes=[
            pltpu.VMEM((t_blk + CONV_HALO, gw), BF16),
            pltpu.VMEM((t_blk + CONV_HALO, 2 * N), BF16),
            pltpu.VMEM((N, gw), F32),
        ],
        compiler_params=_cparams(("parallel", "parallel", "arbitrary")),
        name="ssd_scan",
    )(proj, proj, proj, proj, dtb, eacs, toend, acsrot, acst, decend,
      cwx, cbx, cwbc, cbbc, dskip_e, ssd_norm_w.reshape(1, d_inner), e_all, shift)


def _flash_kernel(q_ref, kn_ref, kr_ref, v_ref, g_ref, o_ref, vt_sc, qt_sc, sa_sc, sb_sc, m_sc, l_sc, acc_sc,
                  *, tq, tk, seq):
    qi = pl.program_id(2)
    n_sub = tq // tk
    tqs = min(2 * LANES, tq)
    assert n_sub == 2

    @pl.when(qi == 0)
    def _():
        for c in range(seq // tk):
            vt_sc[c] = v_ref[c * tk:(c + 1) * tk, :].astype(F32).T.astype(BF16)

    qt_sc[...] = q_ref[...].astype(F32).T.astype(BF16)
    m_sc[...] = jnp.full(m_sc.shape, -jnp.inf, F32)
    l_sc[...] = jnp.zeros(l_sc.shape, F32)
    acc_sc[...] = jnp.zeros(acc_sc.shape, F32)

    def step(kb_next, buf_next, diag_next, kb, buf, diag):
        nxt, cur = [], []
        if kb_next is not None:
            rows = pl.ds(pl.multiple_of(kb_next * tk, tk), tk)
            k = jnp.concatenate([kn_ref[rows, :], kr_ref[rows, :]], axis=1)
            k0 = 0 if diag_next is None else diag_next * tk
            nxt = [(k, k0, c0) for c0 in range(k0, tq, tqs)]
        if kb is not None:
            vt = vt_sc[kb]
            cur = list(range(0 if diag is None else diag * tk, tq, tqs))
        for i in range(max(len(nxt), len(cur))):
            if i < len(nxt):
                k, k0, c0 = nxt[i]
                st = _dot(k, qt_sc[:, c0:c0 + tqs])
                if diag_next is not None and c0 < k0 + tk:
                    ki = lax.broadcasted_iota(jnp.int32, (tk, tqs), 0) + k0
                    qj = lax.broadcasted_iota(jnp.int32, (tk, tqs), 1) + c0
                    st = jnp.where(ki <= qj, st, -jnp.inf)
                buf_next[:, c0:c0 + tqs] = st
            if i < len(cur):
                cs = slice(cur[i], cur[i] + tqs)
                st = buf[:, cs]
                m_prev = m_sc[:, cs]
                m_new = jnp.maximum(m_prev, jnp.max(st, axis=0, keepdims=True))
                alpha = jnp.exp2(m_prev - m_new)
                p = jnp.exp2(st - m_new)
                l_sc[:, cs] = alpha * l_sc[:, cs] + jnp.sum(p, axis=0, keepdims=True)
                acc_sc[:, cs] = acc_sc[:, cs] * alpha + _dot(vt, p.astype(BF16))
                m_sc[:, cs] = m_new

    d0 = qi * n_sub

    @pl.when(qi == 0)
    def _():
        step(d0, sa_sc, 0, None, None, None)

    @pl.when(qi > 0)
    def _():
        step(0, sa_sc, None, None, None, None)

        def pair(j):
            step(2 * j + 1, sb_sc, None, 2 * j, sa_sc, None)
            step(2 * j + 2, sa_sc, None, 2 * j + 1, sb_sc, None)

        def quad(i, carry):
            pair(2 * i)
            pair(2 * i + 1)
            return carry

        n_pairs = qi - 1
        lax.fori_loop(0, n_pairs // 2, quad, 0)

        @pl.when(n_pairs % 2 == 1)
        def _():
            pair(n_pairs - 1)

        step(d0 - 1, sb_sc, None, d0 - 2, sa_sc, None)
        step(d0, sa_sc, 0, d0 - 1, sb_sc, None)

    step(d0 + 1, sb_sc, 1, d0, sa_sc, 0)
    step(None, None, None, d0 + 1, sb_sc, 1)
    o = (acc_sc[...] / l_sc[...]).T
    o_ref[...] = (o * _silu(g_ref[...].astype(F32))).astype(o_ref.dtype)


def _flash(q, kv, kr, proj, gate_off, batch, seq, n_heads, tq, tk):
    m = q.shape[0]
    nq = seq // tq
    dq = 2 * LANES
    assert gate_off % LANES == 0
    return pl.pallas_call(
        functools.partial(_flash_kernel, tq=tq, tk=tk, seq=seq),
        grid=(batch, n_heads, nq),
        in_specs=[
            pl.BlockSpec((tq, dq), lambda b, h, i: (b * nq + i, h)),
            pl.BlockSpec((seq, LANES), lambda b, h, i: (b, 2 * h)),
            pl.BlockSpec((seq, LANES), lambda b, h, i: (b, 0)),
            pl.BlockSpec((seq, LANES), lambda b, h, i: (b, 2 * h + 1)),
            pl.BlockSpec((tq, LANES), lambda b, h, i: (b * nq + i, gate_off // LANES + h)),
        ],
        out_specs=pl.BlockSpec((tq, LANES), lambda b, h, i: (b * nq + i, h)),
        out_shape=jax.ShapeDtypeStruct((m, n_heads * MLA_V_DIM), BF16),
        scratch_shapes=[pltpu.VMEM((seq // tk, MLA_V_DIM, tk), BF16), pltpu.VMEM((dq, tq), BF16),
                        pltpu.VMEM((tk, tq), F32), pltpu.VMEM((tk, tq), F32),
                        pltpu.VMEM((1, tq), F32), pltpu.VMEM((1, tq), F32),
                        pltpu.VMEM((MLA_V_DIM, tq), F32)],
        compiler_params=_cparams(("parallel", "parallel", "arbitrary")),
        name="mla_flash",
    )(q, kv, kr, kv, proj)


def _ssd_out_kernel(a_ref, w_ref, g_ref, o_ref):
    y = _dot(a_ref[...], w_ref[...])
    o_ref[...] = (jax.nn.sigmoid(g_ref[...].astype(F32)) * y).astype(o_ref.dtype)


def _ssd_out(ys, w_ssd, l, proj, gate_off, tm, tn):
    m, k1 = ys.shape
    d = w_ssd.shape[2]
    assert gate_off % tn == 0
    return pl.pallas_call(
        _ssd_out_kernel,
        grid=(m // tm, d // tn),
        in_specs=[
            pl.BlockSpec((tm, k1), lambda i, j: (i, 0)),
            pl.BlockSpec((None, k1, tn), lambda i, j: (l, 0, j)),
            pl.BlockSpec((tm, tn), lambda i, j: (i, gate_off // tn + j)),
        ],
        out_specs=pl.BlockSpec((tm, tn), lambda i, j: (i, j)),
        out_shape=jax.ShapeDtypeStruct((m, d), BF16),
        compiler_params=_cparams(("parallel", "parallel")),
        name="ssd_out",
    )(ys, w_ssd, proj)


def _mla_out_kernel(a_ref, w_ref, g_ref, p_ref, o_ref):
    y = _dot(a_ref[...], w_ref[...])
    o_ref[...] = (p_ref[...].astype(F32) + jax.nn.sigmoid(g_ref[...].astype(F32)) * y).astype(o_ref.dtype)


def _mla_out(og, w_mla, l, proj, gate_off, part, tm, tn):
    m, k2 = og.shape
    d = w_mla.shape[2]
    assert gate_off % tn == 0
    return pl.pallas_call(
        _mla_out_kernel,
        grid=(m // tm, d // tn),
        in_specs=[
            pl.BlockSpec((tm, k2), lambda i, j: (i, 0)),
            pl.BlockSpec((None, k2, tn), lambda i, j: (l, 0, j)),
            pl.BlockSpec((tm, tn), lambda i, j: (i, gate_off // tn + j)),
            pl.BlockSpec((tm, tn), lambda i, j: (i, j)),
        ],
        out_specs=pl.BlockSpec((tm, tn), lambda i, j: (i, j)),
        out_shape=jax.ShapeDtypeStruct((m, d), BF16),
        compiler_params=_cparams(("parallel", "parallel")),
        name="mla_out",
    )(og, w_mla, proj, part)


def _out_kernel(a_ref, w_ref, x_ref, pw_ref, gate_ref, o_ref, *, nk):
    k = pl.program_id(1)

    @pl.when(k == 0)
    def _():
        o_ref[...] = jnp.zeros_like(o_ref)

    o_ref[...] += _dot(a_ref[...], w_ref[...])

    @pl.when(k == nk - 1)
    def _():
        y = o_ref[...]
        yn = y * lax.rsqrt(jnp.mean(y * y, axis=-1, keepdims=True) + NORM_EPS) * pw_ref[...]
        o_ref[...] = x_ref[...] + gate_ref[...] * yn


def _out_proj(merged, w_out, l, x2, post_w, mod4, seq, tm, tk):
    m, d = x2.shape
    nk = d // tk
    per_b = seq // tm
    return pl.pallas_call(
        functools.partial(_out_kernel, nk=nk),
        grid=(m // tm, nk),
        in_specs=[
            pl.BlockSpec((tm, tk), lambda i, k: (i, k)),
            pl.BlockSpec((None, tk, d), lambda i, k: (l, k, 0)),
            pl.BlockSpec((tm, d), lambda i, k: (i, 0)),
            pl.BlockSpec((1, d), lambda i, k: (0, 0)),
            pl.BlockSpec((None, None, 1, d), lambda i, k: (i // per_b, 2, 0, 0)),
        ],
        out_specs=pl.BlockSpec((tm, d), lambda i, k: (i, 0)),
        out_shape=jax.ShapeDtypeStruct((m, d), F32),
        compiler_params=_cparams(("parallel", "arbitrary")),
        name="out_proj",
    )(merged, w_out, x2, post_w.reshape(1, d), mod4)


def _proj_layout(d, d_inner, nh, n_mla_heads, q_lora, kv_lora):
    gn = SSD_N_GROUPS * SSD_D_STATE
    widths = [("z", d_inner), ("xs", d_inner), ("bm", gn), ("cm", gn),
              ("gate", n_mla_heads * MLA_V_DIM), ("merge", 2 * d), ("cq", q_lora), ("ckv", kv_lora),
              ("kr", 2 * MLA_QK_ROPE), ("dt", nh)]
    offs, o = {}, 0
    for name, w in widths:
        offs[name] = o
        o += w
    return offs, o


def _cast_kernel(w_ref, o_ref):
    o_ref[...] = w_ref[...].astype(o_ref.dtype)


def _cast_bf16(w3):
    depth, k, n = w3.shape
    tr = max(2 * SUBLANES, min(k, CAST_BLOCK_BYTES // (n * 4)))
    assert k % tr == 0
    return pl.pallas_call(
        _cast_kernel,
        grid=(depth, k // tr),
        in_specs=[pl.BlockSpec((None, tr, n), lambda l, r: (l, r, 0))],
        out_specs=pl.BlockSpec((None, tr, n), lambda l, r: (l, r, 0)),
        out_shape=jax.ShapeDtypeStruct((depth, k, n), BF16),
        compiler_params=_cparams(("parallel", "parallel")),
        name="cast_bf16",
    )(w3)


def _w_in_kernel(src_ref, mode_ref, *refs, n_chunks):
    o_ref = refs[n_chunks]
    j = pl.program_id(1)
    half = MLA_QK_ROPE // 2
    parts = []
    for c in range(n_chunks):
        blk = refs[c][...]
        mode = mode_ref[j * n_chunks + c]
        rot = jnp.concatenate([-blk[half:], blk[:half]], axis=0)
        v = jnp.where(mode == W_IN_ROTATE, rot, blk)
        parts.append(jnp.where(mode == W_IN_ZERO, 0.0, v))
    o_ref[...] = jnp.concatenate(parts, axis=0).T.astype(BF16)


def _prep_w_in(w_in3, offs, n_tot, n_pad, d, d_inner, nh, n_mla_heads, q_lora, kv_lora):
    depth, k, n_src = w_in3.shape
    gn = SSD_N_GROUPS * SSD_D_STATE
    conv_dim = d_inner + 2 * gn
    s_xbc = d_inner
    s_dt = s_xbc + conv_dim
    s_cq = s_dt + nh
    s_ckv = s_cq + q_lora
    s_kr = s_ckv + kv_lora
    s_gate = s_kr + MLA_QK_ROPE
    s_merge = s_gate + n_mla_heads * MLA_V_DIM
    assert s_merge + 2 * d == n_src and MLA_QK_ROPE == W_IN_CHUNK
    segments = (
        (offs["z"], 0, d_inner + conv_dim, W_IN_COPY),
        (offs["gate"], s_gate, n_mla_heads * MLA_V_DIM + 2 * d, W_IN_COPY),
        (offs["cq"], s_cq, q_lora + kv_lora, W_IN_COPY),
        (offs["kr"], s_kr, MLA_QK_ROPE, W_IN_COPY),
        (offs["kr"] + MLA_QK_ROPE, s_kr, MLA_QK_ROPE, W_IN_ROTATE),
        (offs["dt"], s_dt, nh, W_IN_COPY),
    )
    n_dst = n_pad // W_IN_CHUNK
    src_tab = [0] * n_dst
    mode_tab = [W_IN_ZERO] * n_dst
    for dst, src, width, mode in segments:
        assert dst % W_IN_CHUNK == 0 and src % W_IN_CHUNK == 0 and width % W_IN_CHUNK == 0
        for t in range(width // W_IN_CHUNK):
            src_tab[dst // W_IN_CHUNK + t] = src // W_IN_CHUNK + t
            mode_tab[dst // W_IN_CHUNK + t] = mode
    n_chunks = W_IN_COLS // W_IN_CHUNK
    w_t = jnp.swapaxes(w_in3, 1, 2)

    def chunk_spec(c):
        return pl.BlockSpec((None, W_IN_CHUNK, k), lambda l, j, src, mode: (l, src[j * n_chunks + c], 0))

    return pl.pallas_call(
        functools.partial(_w_in_kernel, n_chunks=n_chunks),
        grid_spec=pltpu.PrefetchScalarGridSpec(
            num_scalar_prefetch=2,
            grid=(depth, n_pad // W_IN_COLS),
            in_specs=[chunk_spec(c) for c in range(n_chunks)],
            out_specs=pl.BlockSpec((None, k, W_IN_COLS), lambda l, j, src, mode: (l, 0, j)),
        ),
        out_shape=jax.ShapeDtypeStruct((depth, k, n_pad), BF16),
        compiler_params=_cparams(("parallel", "parallel")),
        name="w_in_prep",
    )(jnp.asarray(src_tab, jnp.int32), jnp.asarray(mode_tab, jnp.int32), *([w_t] * n_chunks))


def _prep_w_q(w_q_up3, n_mla_heads):
    depth, k, _ = w_q_up3.shape
    qk = MLA_QK_NOPE + MLA_QK_ROPE
    half = MLA_QK_ROPE // 2
    w = w_q_up3.reshape(depth, k, n_mla_heads, qk) * (qk ** -0.5 * LOG2_E)
    nope, rope = w[..., :MLA_QK_NOPE], w[..., MLA_QK_NOPE:]
    rot = jnp.concatenate([-rope[..., half:], rope[..., :half]], axis=-1)
    return jnp.concatenate([nope, rope, rot], axis=-1).reshape(
        depth, k, n_mla_heads * 2 * LANES).astype(BF16)


def _round_up(v, mult):
    return (v + mult - 1) // mult * mult


def kernel(x, c, positions, ada_w, ada_b, pre_norm_w, post_norm_w, w_in, conv_w, conv_b, dt_bias,
           a_log, d_skip, ssd_norm_w, q_norm_w, w_q_up, kv_norm_w, w_kv_up, w_ssd_proj, w_mla_proj,
           w_out):
    batch, seq, d = x.shape
    depth = ada_w.shape[0]
    m = batch * seq
    d_inner = ssd_norm_w.shape[1]
    nh = dt_bias.shape[1]
    q_lora = q_norm_w.shape[1]
    kv_lora = kv_norm_w.shape[1]
    n_mla = w_mla_proj.shape[1] // MLA_V_DIM
    offs, n_tot = _proj_layout(d, d_inner, nh, n_mla, q_lora, kv_lora)

    tm_big = min(1024, m)
    tn_proj = 1024
    n_pad = _round_up(n_tot, tn_proj)

    c_pad = jnp.zeros((SUBLANES, d), F32).at[:batch].set(c)
    mod = _ada(c_pad, ada_w, ada_b)
    cos, sin = _rope_tables(positions)

    w_in_b = _prep_w_in(w_in, offs, n_tot, n_pad, d, d_inner, nh, n_mla, q_lora, kv_lora)
    w_q_b = _prep_w_q(w_q_up, n_mla)
    w_kv_b = _cast_bf16(w_kv_up)
    w_ssd_b = _cast_bf16(w_ssd_proj)
    w_mla_b = _cast_bf16(w_mla_proj)
    w_out_b = _cast_bf16(w_out)

    x2 = x.reshape(m, d)
    for l in range(depth):
        mod4 = mod[l, :batch].reshape(batch, 3, 1, d)
        h = _prenorm(x2, pre_norm_w[l], mod4, seq)
        proj = _matmul(h, w_in_b, l, BF16, tm_big, tn_proj)

        pre = _ssd_pre(proj, offs["dt"], dt_bias[l], a_log[l], d_inner, tp=min(1024, seq))
        ys = _ssd(proj, offs, pre, conv_w[l], conv_b[l], d_skip[l], ssd_norm_w[l], batch, seq,
                  t_blk=min(512, seq))

        q = _norm_matmul(proj, offs["cq"], q_lora, q_norm_w[l], w_q_b, l,
                         tm_big, min(2048, n_mla * 2 * LANES), cos, sin, name="q_up")
        kv = _norm_matmul(proj, offs["ckv"], kv_lora, kv_norm_w[l], w_kv_b, l,
                          tm_big, min(4096, n_mla * 2 * LANES), name="kv_up")
        kr = _krope(proj, offs["kr"], cos, sin)
        tq = min(1024, seq)
        og = _flash(q, kv, kr, proj, offs["gate"], batch, seq, n_mla, tq, min(512, tq))

        part = _ssd_out(ys, w_ssd_b, l, proj, offs["merge"], tm_big, min(512, d))
        merged = _mla_out(og, w_mla_b, l, proj, offs["merge"] + d, part, tm_big, min(1024, d))
        x2 = _out_proj(merged, w_out_b, l, x2, post_norm_w[l], mod4, seq,
                       min(512, seq), min(512, d))
    return x2.reshape(batch, seq, d)
```

```python
import functools
import math

import jax
import jax.numpy as jnp
from jax import lax
from jax.experimental import pallas as pl
from jax.experimental.pallas import tpu as pltpu

SSD_HEAD_DIM = 64
SSD_N_GROUPS = 8
SSD_D_STATE = 128
SSD_CHUNK = 128
SSD_CONV_WIDTH = 4
MLA_QK_NOPE = 128
MLA_QK_ROPE = 64
MLA_V_DIM = 128
ROPE_THETA = 10000.0
NORM_EPS = 1e-6
LOG2_E = math.log2(math.e)

LANES = 128
SUBLANES = 8
CONV_HALO = 16
VMEM_LIMIT = 56 * 1024 * 1024
CAST_BLOCK_BYTES = 8 * 1024 * 1024
W_IN_CHUNK = 64
W_IN_COLS = 512
W_IN_COPY, W_IN_ROTATE, W_IN_ZERO = 0, 1, 2

F32 = jnp.float32
BF16 = jnp.bfloat16


def _cparams(sem):
    return pltpu.CompilerParams(dimension_semantics=sem, vmem_limit_bytes=VMEM_LIMIT)


def _silu(v):
    h = 0.5 * v
    return h + h * jnp.tanh(h)


def _split3(v):
    hi = v.astype(BF16)
    r1 = v - hi.astype(F32)
    mid = r1.astype(BF16)
    lo = (r1 - mid.astype(F32)).astype(BF16)
    return hi, mid, lo


def _dot(a, b):
    return jnp.dot(a, b, preferred_element_type=F32)


def _ada_kernel(c_ref, w_ref, b_ref, o_ref):
    @pl.when(pl.program_id(1) == 0)
    def _():
        o_ref[...] = jnp.broadcast_to(b_ref[...], o_ref.shape)

    a = _silu(c_ref[...]).astype(BF16)
    o_ref[...] += _dot(a, w_ref[...].astype(BF16))


def _ada(c_pad, ada_w, ada_b):
    depth, d, n = ada_w.shape
    tk = min(256, d)
    return pl.pallas_call(
        _ada_kernel,
        grid=(depth, d // tk),
        in_specs=[
            pl.BlockSpec((SUBLANES, tk), lambda l, k: (0, k)),
            pl.BlockSpec((None, tk, n), lambda l, k: (l, k, 0)),
            pl.BlockSpec((None, 1, n), lambda l, k: (l, 0, 0)),
        ],
        out_specs=pl.BlockSpec((None, SUBLANES, n), lambda l, k: (l, 0, 0)),
        out_shape=jax.ShapeDtypeStruct((depth, SUBLANES, n), F32),
        compiler_params=_cparams(("parallel", "arbitrary")),
        name="ada_mod",
    )(c_pad, ada_w, ada_b.reshape(depth, 1, n))


def _prenorm_kernel(x_ref, w_ref, shift_ref, scale_ref, o_ref):
    x = x_ref[...]
    y = x * lax.rsqrt(jnp.mean(x * x, axis=-1, keepdims=True) + NORM_EPS)
    o_ref[...] = (y * w_ref[...] * (1.0 + scale_ref[...]) + shift_ref[...]).astype(BF16)


def _prenorm(x2, w, mod4, seq):
    m, d = x2.shape
    tm = min(512, seq)
    per_b = seq // tm
    return pl.pallas_call(
        _prenorm_kernel,
        grid=(m // tm,),
        in_specs=[
            pl.BlockSpec((tm, d), lambda i: (i, 0)),
            pl.BlockSpec((1, d), lambda i: (0, 0)),
            pl.BlockSpec((None, None, 1, d), lambda i: (i // per_b, 0, 0, 0)),
            pl.BlockSpec((None, None, 1, d), lambda i: (i // per_b, 1, 0, 0)),
        ],
        out_specs=pl.BlockSpec((tm, d), lambda i: (i, 0)),
        out_shape=jax.ShapeDtypeStruct((m, d), BF16),
        compiler_params=_cparams(("parallel",)),
        name="prenorm",
    )(x2, w.reshape(1, d), mod4, mod4)


def _mm_kernel(a_ref, w_ref, o_ref):
    o_ref[...] = _dot(a_ref[...], w_ref[...]).astype(o_ref.dtype)


def _matmul(a, w3, l, out_dtype, tm, tn):
    m, k = a.shape
    n = w3.shape[2]
    return pl.pallas_call(
        _mm_kernel,
        grid=(m // tm, n // tn),
        in_specs=[
            pl.BlockSpec((tm, k), lambda i, j: (i, 0)),
            pl.BlockSpec((None, k, tn), lambda i, j: (l, 0, j)),
        ],
        out_specs=pl.BlockSpec((tm, tn), lambda i, j: (i, j)),
        out_shape=jax.ShapeDtypeStruct((m, n), out_dtype),
        compiler_params=_cparams(("parallel", "parallel")),
        name="in_proj",
    )(a, w3)


def _rope_block(a, cos, sin):
    return a * cos + pltpu.roll(a, LANES // 2, axis=1) * sin


def _norm_mm_kernel(a_ref, nw_ref, w_ref, *rest, rope, tn):
    if rope:
        cos_ref, sin_ref, o_ref, an_sc = rest
    else:
        o_ref, an_sc = rest

    @pl.when(pl.program_id(1) == 0)
    def _():
        a = a_ref[...].astype(F32)
        y = a * lax.rsqrt(jnp.mean(a * a, axis=-1, keepdims=True) + NORM_EPS)
        an_sc[...] = (y * nw_ref[...]).astype(BF16)

    acc = _dot(an_sc[...], w_ref[...])
    if rope:
        cos = cos_ref[...]
        sin = sin_ref[...]
        for hh in range(tn // (2 * LANES)):
            c0 = hh * 2 * LANES
            o_ref[:, c0:c0 + LANES] = acc[:, c0:c0 + LANES].astype(o_ref.dtype)
            o_ref[:, c0 + LANES:c0 + 2 * LANES] = _rope_block(
                acc[:, c0 + LANES:c0 + 2 * LANES], cos, sin).astype(o_ref.dtype)
    else:
        o_ref[...] = acc.astype(o_ref.dtype)


def _norm_matmul(proj, col_off, k, norm_w, w, l, tm, tn, cos=None, sin=None, name="norm_mm"):
    m = proj.shape[0]
    n = w.shape[2]
    rope = cos is not None
    assert col_off % k == 0
    in_specs = [
        pl.BlockSpec((tm, k), lambda i, j: (i, col_off // k)),
        pl.BlockSpec((1, k), lambda i, j: (0, 0)),
        pl.BlockSpec((None, k, tn), lambda i, j: (l, 0, j)),
    ]
    args = [proj, norm_w.reshape(1, k), w]
    if rope:
        in_specs += [pl.BlockSpec((tm, LANES), lambda i, j: (i, 0))] * 2
        args += [cos, sin]
    return pl.pallas_call(
        functools.partial(_norm_mm_kernel, rope=rope, tn=tn),
        grid=(m // tm, n // tn),
        in_specs=in_specs,
        out_specs=pl.BlockSpec((tm, tn), lambda i, j: (i, j)),
        out_shape=jax.ShapeDtypeStruct((m, n), BF16),
        scratch_shapes=[pltpu.VMEM((tm, k), BF16)],
        compiler_params=_cparams(("parallel", "arbitrary")),
        name=name,
    )(*args)


def _rope_tab_kernel(pos_ref, freq_ref, cos_ref, sin_ref):
    ang = pos_ref[...].astype(F32) * freq_ref[...]
    lane = lax.broadcasted_iota(jnp.int32, ang.shape, 1)
    keep = lane < MLA_QK_ROPE
    cos_ref[...] = jnp.where(keep, jnp.cos(ang), 0.0)
    sin_ref[...] = jnp.where(keep, jnp.sin(ang), 0.0)


def _rope_tables(positions):
    m = positions.size
    tm = min(1024, m)
    half = MLA_QK_ROPE // 2
    inv_freq = ROPE_THETA ** (-(jnp.arange(0, MLA_QK_ROPE, 2, dtype=F32) / MLA_QK_ROPE))
    freq = jnp.concatenate([inv_freq, inv_freq, jnp.zeros((LANES - 2 * half,), F32)]).reshape(1, LANES)
    return pl.pallas_call(
        _rope_tab_kernel,
        grid=(m // tm,),
        in_specs=[pl.BlockSpec((tm, 1), lambda i: (i, 0)),
                  pl.BlockSpec((1, LANES), lambda i: (0, 0))],
        out_specs=[pl.BlockSpec((tm, LANES), lambda i: (i, 0))] * 2,
        out_shape=[jax.ShapeDtypeStruct((m, LANES), F32)] * 2,
        compiler_params=_cparams(("parallel",)),
        name="rope_tables",
    )(positions.reshape(m, 1), freq)


def _krope_kernel(a_ref, cos_ref, sin_ref, o_ref):
    o_ref[...] = _rope_block(a_ref[...].astype(F32), cos_ref[...], sin_ref[...]).astype(BF16)


def _krope(proj, col_off, cos, sin):
    m = proj.shape[0]
    tm = min(1024, m)
    return pl.pallas_call(
        _krope_kernel,
        grid=(m // tm,),
        in_specs=[pl.BlockSpec((tm, LANES), lambda i: (i, col_off // LANES)),
                  pl.BlockSpec((tm, LANES), lambda i: (i, 0)),
                  pl.BlockSpec((tm, LANES), lambda i: (i, 0))],
        out_specs=pl.BlockSpec((tm, LANES), lambda i: (i, 0)),
        out_shape=jax.ShapeDtypeStruct((m, LANES), BF16),
        compiler_params=_cparams(("parallel",)),
        name="k_rope",
    )(proj, cos, sin)


def _ssd_pre_kernel(dt_ref, bias_ref, alog_ref, tri_ref, efull_ref,
                    dtb_ref, eacs_ref, toend_ref, acsrot_ref, acst_ref, decend_ref, *, tp, r):
    L = SSD_CHUNK
    dt = jax.nn.softplus(dt_ref[...].astype(F32) + bias_ref[...])
    dtb_ref[...] = dt.astype(BF16)
    a = dt * (-jnp.exp(alog_ref[...])) * LOG2_E
    tri = tri_ref[...]
    lasts = []
    for c in range(tp // L):
        rows = slice(c * L, (c + 1) * L)
        hi, mid, lo = _split3(a[rows])
        acs = _dot(tri, hi) + _dot(tri, mid) + _dot(tri, lo)
        last = acs[L - 1:L, :]
        lasts.append(last)
        eacs_ref[rows, :] = jnp.exp2(acs).astype(BF16)
        toend_ref[rows, :] = jnp.exp2(last - acs).astype(BF16)
        acst_ref[c] = acs.T
        for g in range(SSD_N_GROUPS):
            shift = (LANES - g * r) % LANES
            acsrot_ref[g, rows, :] = acs if shift == 0 else pltpu.roll(acs, shift, axis=1)
    n_c = tp // L
    pad = [jnp.zeros((SUBLANES - n_c % SUBLANES, LANES), F32)] if n_c % SUBLANES else []
    el = jnp.exp2(jnp.concatenate(lasts + pad, axis=0))
    hi, mid, lo = _split3(el)
    e = efull_ref[...]
    dec = _dot(hi, e) + _dot(mid, e) + _dot(lo, e)
    for c in range(n_c):
        decend_ref[c] = dec[c:c + 1, :]


def _ssd_pre(proj, col_off, dt_bias, a_log, d_inner, tp):
    m = proj.shape[0]
    nh = dt_bias.shape[0]
    assert nh == LANES and col_off % LANES == 0
    r = nh // SSD_N_GROUPS
    L = SSD_CHUNK
    tri = (jnp.arange(L)[:, None] >= jnp.arange(L)[None, :]).astype(BF16)
    efull = (jnp.arange(nh)[:, None] == (jnp.arange(d_inner)[None, :] // SSD_HEAD_DIM)).astype(BF16)
    nct = m // L
    return pl.pallas_call(
        functools.partial(_ssd_pre_kernel, tp=tp, r=r),
        grid=(m // tp,),
        in_specs=[
            pl.BlockSpec((tp, LANES), lambda i: (i, col_off // LANES)),
            pl.BlockSpec((1, LANES), lambda i: (0, 0)),
            pl.BlockSpec((1, LANES), lambda i: (0, 0)),
            pl.BlockSpec((L, L), lambda i: (0, 0)),
            pl.BlockSpec((nh, d_inner), lambda i: (0, 0)),
        ],
        out_specs=[
            pl.BlockSpec((tp, LANES), lambda i: (i, 0)),
            pl.BlockSpec((tp, LANES), lambda i: (i, 0)),
            pl.BlockSpec((tp, LANES), lambda i: (i, 0)),
            pl.BlockSpec((SSD_N_GROUPS, tp, LANES), lambda i: (0, i, 0)),
            pl.BlockSpec((tp // L, LANES, L), lambda i: (i, 0, 0)),
            pl.BlockSpec((tp // L, 1, d_inner), lambda i: (i, 0, 0)),
        ],
        out_shape=[
            jax.ShapeDtypeStruct((m, LANES), BF16),
            jax.ShapeDtypeStruct((m, LANES), BF16),
            jax.ShapeDtypeStruct((m, LANES), BF16),
            jax.ShapeDtypeStruct((SSD_N_GROUPS, m, LANES), F32),
            jax.ShapeDtypeStruct((nct, LANES, L), F32),
            jax.ShapeDtypeStruct((nct, 1, d_inner), F32),
        ],
        compiler_params=_cparams(("parallel",)),
        name="ssd_pre",
    )(proj, dt_bias.reshape(1, nh), a_log.reshape(1, nh), tri, efull)


def _ssd_kernel(xs_ref, bm_ref, cm_ref, z_ref, dtb_ref, eacs_ref, toend_ref, acs_ref, acst_ref,
                decend_ref, cwx_ref, cbx_ref, cwbc_ref, cbbc_ref, dskip_ref,
                nw_ref, e_ref, shift_ref, o_ref,
                xin_sc, bcin_sc, state_sc, *, t_blk, r):
    L = SSD_CHUNK
    K = SSD_CONV_WIDTH
    P = SSD_HEAD_DIM
    N = SSD_D_STATE
    gw = r * P
    halo = CONV_HALO

    @pl.when(pl.program_id(2) == 0)
    def _():
        state_sc[...] = jnp.zeros_like(state_sc)
        xin_sc[0:halo, :] = jnp.zeros((halo, gw), BF16)
        bcin_sc[0:halo, :] = jnp.zeros((halo, 2 * N), BF16)

    xin_sc[halo:halo + t_blk, :] = xs_ref[...]
    bcin_sc[halo:halo + t_blk, 0:N] = bm_ref[...]
    bcin_sc[halo:halo + t_blk, N:2 * N] = cm_ref[...]

    def conv(in_sc, rows_ext, w_ref, b_ref):
        xe = in_sc[rows_ext, :]
        sh = _dot(shift_ref[...], xe)
        acc = b_ref[...] + w_ref[K - 1:K, :] * xe[halo:halo + L, :].astype(F32)
        for k in range(K - 1):
            acc = acc + w_ref[k:k + 1, :] * sh[k * L:(k + 1) * L, :]
        return _silu(acc)

    li = lax.broadcasted_iota(jnp.int32, (L, L), 0)
    si = lax.broadcasted_iota(jnp.int32, (L, L), 1)
    causal = li >= si
    lane = lax.broadcasted_iota(jnp.int32, (L, 2 * P), 1)
    lo_half = lane < P

    def chunk(c, carry):
        rows = pl.ds(pl.multiple_of(c * L, L), L)
        rows_ext = pl.ds(pl.multiple_of(c * L, L), halo + L)
        xc = conv(xin_sc, rows_ext, cwx_ref, cbx_ref)
        bc_c = conv(bcin_sc, rows_ext, cwbc_ref, cbbc_ref).astype(BF16)
        b_c = bc_c[:, 0:N]
        c_c = bc_c[:, N:2 * N]
        e = e_ref[...]
        dt_e = _dot(dtb_ref[rows, :], e)
        eacs_e = _dot(eacs_ref[rows, :], e)
        toend_e = _dot(toend_ref[rows, :], e)
        xdt = xc * dt_e
        xdt_b = xdt.astype(BF16)
        xw_b = (xdt * toend_e).astype(BF16)
        cb = lax.dot_general(c_c, b_c, (((1,), (1,)), ((), ())), preferred_element_type=F32)
        acs = acs_ref[rows, :]
        acst = acst_ref[c]
        st = state_sc[...]
        y = _dot(c_c, st.astype(BF16)) * eacs_e
        pieces = []
        for jp in range(r // 2):
            xp = xdt_b[:, jp * 2 * P:(jp + 1) * 2 * P]
            acc = None
            for u in range(2):
                j = 2 * jp + u
                colb = jnp.broadcast_to(acs[:, j:j + 1], (L, L))
                rowb = jnp.broadcast_to(acst[j:j + 1, :], (L, L))
                dec = jnp.exp2(jnp.where(causal, colb - rowb, -jnp.inf))
                m_h = (cb * dec).astype(BF16)
                xm = jnp.where(lo_half if u == 0 else jnp.logical_not(lo_half), xp, jnp.zeros_like(xp))
                d = _dot(m_h, xm)
                acc = d if acc is None else acc + d
            pieces.append(acc)
        y = y + jnp.concatenate(pieces, axis=1)
        upd = lax.dot_general(b_c, xw_b, (((0,), (0,)), ((), ())), preferred_element_type=F32)
        state_sc[...] = st * decend_ref[c] + upd
        y = y + dskip_ref[...] * xc
        yg = y * _silu(z_ref[rows, :].astype(F32))
        ms = jnp.mean(yg * yg, axis=-1, keepdims=True)
        o_ref[rows, :] = (yg * lax.rsqrt(ms + NORM_EPS) * nw_ref[...]).astype(o_ref.dtype)
        return carry

    lax.fori_loop(0, t_blk // L, chunk, 0)
    xin_sc[0:halo, :] = xin_sc[t_blk:t_blk + halo, :]
    bcin_sc[0:halo, :] = bcin_sc[t_blk:t_blk + halo, :]


def _ssd(proj, offs, pre, conv_w, conv_b, d_skip, ssd_norm_w, batch, seq, t_blk):
    m = proj.shape[0]
    d_inner = ssd_norm_w.shape[0]
    nh = d_skip.shape[0]
    G = SSD_N_GROUPS
    N = SSD_D_STATE
    L = SSD_CHUNK
    r = nh // G
    gw = r * SSD_HEAD_DIM
    assert gw % LANES == 0 and r % 2 == 0 and r % SUBLANES == 0
    dtb, eacs, toend, acsrot, acst, decend = pre
    nt = seq // t_blk
    ncb = t_blk // L
    e_all = (jnp.arange(LANES)[None, :, None]
             == (jnp.arange(G)[:, None, None] * r + jnp.arange(gw)[None, None, :] // SSD_HEAD_DIM)).astype(BF16)
    dskip_e = jnp.repeat(d_skip.astype(F32), SSD_HEAD_DIM).reshape(1, d_inner)

    def bc_pairs(a):
        b_part = a[:, d_inner:d_inner + G * N].reshape(-1, G, 1, N)
        c_part = a[:, d_inner + G * N:].reshape(-1, G, 1, N)
        return jnp.concatenate([b_part, c_part], axis=2).reshape(-1, G * 2 * N)

    cb2 = conv_b.reshape(1, -1)
    cwx, cbx = conv_w[:, :d_inner], cb2[:, :d_inner]
    cwbc, cbbc = bc_pairs(conv_w), bc_pairs(cb2)
    kw = SSD_CONV_WIDTH
    tt = jnp.arange((kw - 1) * L)
    shift = (jnp.arange(CONV_HALO + L)[None, :]
             == (CONV_HALO + tt % L - (kw - 1) + tt // L)[:, None]).astype(BF16)
    for o in (offs["xs"], offs["z"]):
        assert o % gw == 0
    row = lambda b, g, t: b * nt + t
    in_specs = [
        pl.BlockSpec((t_blk, gw), lambda b, g, t: (row(b, g, t), offs["xs"] // gw + g)),
        pl.BlockSpec((t_blk, N), lambda b, g, t: (row(b, g, t), offs["bm"] // N + g)),
        pl.BlockSpec((t_blk, N), lambda b, g, t: (row(b, g, t), offs["cm"] // N + g)),
        pl.BlockSpec((t_blk, gw), lambda b, g, t: (row(b, g, t), offs["z"] // gw + g)),
        pl.BlockSpec((t_blk, LANES), lambda b, g, t: (row(b, g, t), 0)),
        pl.BlockSpec((t_blk, LANES), lambda b, g, t: (row(b, g, t), 0)),
        pl.BlockSpec((t_blk, LANES), lambda b, g, t: (row(b, g, t), 0)),
        pl.BlockSpec((None, t_blk, LANES), lambda b, g, t: (g, row(b, g, t), 0)),
        pl.BlockSpec((ncb, r, L), lambda b, g, t: (row(b, g, t), g, 0)),
        pl.BlockSpec((ncb, 1, gw), lambda b, g, t: (row(b, g, t), 0, g)),
        pl.BlockSpec((SSD_CONV_WIDTH, gw), lambda b, g, t: (0, g)),
        pl.BlockSpec((1, gw), lambda b, g, t: (0, g)),
        pl.BlockSpec((SSD_CONV_WIDTH, 2 * N), lambda b, g, t: (0, g)),
        pl.BlockSpec((1, 2 * N), lambda b, g, t: (0, g)),
        pl.BlockSpec((1, gw), lambda b, g, t: (0, g)),
        pl.BlockSpec((1, gw), lambda b, g, t: (0, g)),
        pl.BlockSpec((None, LANES, gw), lambda b, g, t: (g, 0, 0)),
        pl.BlockSpec(((kw - 1) * L, CONV_HALO + L), lambda b, g, t: (0, 0)),
    ]
    return pl.pallas_call(
        functools.partial(_ssd_kernel, t_blk=t_blk, r=r),
        grid=(batch, G, nt),
        in_specs=in_specs,
        out_specs=pl.BlockSpec((t_blk, gw), lambda b, g, t: (row(b, g, t), g)),
        out_shape=jax.ShapeDtypeStruct((m, d_inner), BF16),
        scratch_shapes=[
            pltpu.VMEM((t_blk + CONV_HALO, gw), BF16),
            pltpu.VMEM((t_blk + CONV_HALO, 2 * N), BF16),
            pltpu.VMEM((N, gw), F32),
        ],
        compiler_params=_cparams(("parallel", "parallel", "arbitrary")),
        name="ssd_scan",
    )(proj, proj, proj, proj, dtb, eacs, toend, acsrot, acst, decend,
      cwx, cbx, cwbc, cbbc, dskip_e, ssd_norm_w.reshape(1, d_inner), e_all, shift)


def _flash_kernel(q_ref, kn_ref, kr_ref, v_ref, g_ref, o_ref, vt_sc, qt_sc, sa_sc, sb_sc, m_sc, l_sc, acc_sc,
                  *, tq, tk, seq):
    n_sub = tq // tk
    tqs = min(2 * LANES, tq)
    assert n_sub == 2

    for c in range(seq // tk):
        vt_sc[c] = v_ref[c * tk:(c + 1) * tk, :].astype(F32).T.astype(BF16)

    def step(kb_next, buf_next, diag_next, kb, buf, diag):
        nxt, cur = [], []
        if kb_next is not None:
            rows = pl.ds(pl.multiple_of(kb_next * tk, tk), tk)
            k = jnp.concatenate([kn_ref[rows, :], kr_ref[rows, :]], axis=1)
            k0 = 0 if diag_next is None else diag_next * tk
            nxt = [(k, k0, c0) for c0 in range(k0, tq, tqs)]
        if kb is not None:
            vt = vt_sc[kb]
            cur = list(range(0 if diag is None else diag * tk, tq, tqs))
        for i in range(max(len(nxt), len(cur))):
            if i < len(nxt):
                k, k0, c0 = nxt[i]
                st = _dot(k, qt_sc[:, c0:c0 + tqs])
                if diag_next is not None and c0 < k0 + tk:
                    ki = lax.broadcasted_iota(jnp.int32, (tk, tqs), 0) + k0
                    qj = lax.broadcasted_iota(jnp.int32, (tk, tqs), 1) + c0
                    st = jnp.where(ki <= qj, st, -jnp.inf)
                buf_next[:, c0:c0 + tqs] = st
            if i < len(cur):
                cs = slice(cur[i], cur[i] + tqs)
                st = buf[:, cs]
                m_prev = m_sc[:, cs]
                m_new = jnp.maximum(m_prev, jnp.max(st, axis=0, keepdims=True))
                alpha = jnp.exp2(m_prev - m_new)
                p = jnp.exp2(st - m_new)
                l_sc[:, cs] = alpha * l_sc[:, cs] + jnp.sum(p, axis=0, keepdims=True)
                acc_sc[:, cs] = acc_sc[:, cs] * alpha + _dot(vt, p.astype(BF16))
                m_sc[:, cs] = m_new

    def q_block(qi, carry):
        qrows = pl.ds(pl.multiple_of(qi * tq, tq), tq)
        qt_sc[...] = q_ref[qrows, :].astype(F32).T.astype(BF16)
        m_sc[...] = jnp.full(m_sc.shape, -jnp.inf, F32)
        l_sc[...] = jnp.zeros(l_sc.shape, F32)
        acc_sc[...] = jnp.zeros(acc_sc.shape, F32)
        d0 = qi * n_sub

        @pl.when(qi == 0)
        def _():
            step(d0, sa_sc, 0, None, None, None)

        @pl.when(qi > 0)
        def _():
            step(0, sa_sc, None, None, None, None)

            def pair(j):
                step(2 * j + 1, sb_sc, None, 2 * j, sa_sc, None)
                step(2 * j + 2, sa_sc, None, 2 * j + 1, sb_sc, None)

            def quad(i, c):
                pair(2 * i)
                pair(2 * i + 1)
                return c

            n_pairs = qi - 1
            lax.fori_loop(0, n_pairs // 2, quad, 0)

            @pl.when(n_pairs % 2 == 1)
            def _():
                pair(n_pairs - 1)

            step(d0 - 1, sb_sc, None, d0 - 2, sa_sc, None)
            step(d0, sa_sc, 0, d0 - 1, sb_sc, None)

        step(d0 + 1, sb_sc, 1, d0, sa_sc, 0)
        step(None, None, None, d0 + 1, sb_sc, 1)
        o = (acc_sc[...] / l_sc[...]).T
        o_ref[qrows, :] = (o * _silu(g_ref[qrows, :].astype(F32))).astype(o_ref.dtype)
        return carry

    lax.fori_loop(0, seq // tq, q_block, 0)


def _flash(q, kv, kr, proj, gate_off, batch, seq, n_heads, tq, tk):
    m = q.shape[0]
    dq = 2 * LANES
    assert gate_off % LANES == 0
    return pl.pallas_call(
        functools.partial(_flash_kernel, tq=tq, tk=tk, seq=seq),
        grid=(batch, n_heads),
        in_specs=[
            pl.BlockSpec((seq, dq), lambda b, h: (b, h)),
            pl.BlockSpec((seq, LANES), lambda b, h: (b, 2 * h)),
            pl.BlockSpec((seq, LANES), lambda b, h: (b, 0)),
            pl.BlockSpec((seq, LANES), lambda b, h: (b, 2 * h + 1)),
            pl.BlockSpec((seq, LANES), lambda b, h: (b, gate_off // LANES + h)),
        ],
        out_specs=pl.BlockSpec((seq, LANES), lambda b, h: (b, h)),
        out_shape=jax.ShapeDtypeStruct((m, n_heads * MLA_V_DIM), BF16),
        scratch_shapes=[pltpu.VMEM((seq // tk, MLA_V_DIM, tk), BF16), pltpu.VMEM((dq, tq), BF16),
                        pltpu.VMEM((tk, tq), F32), pltpu.VMEM((tk, tq), F32),
                        pltpu.VMEM((1, tq), F32), pltpu.VMEM((1, tq), F32),
                        pltpu.VMEM((MLA_V_DIM, tq), F32)],
        compiler_params=_cparams(("parallel", "parallel")),
        name="mla_flash",
    )(q, kv, kr, kv, proj)


def _ssd_out_kernel(a_ref, w_ref, g_ref, o_ref):
    y = _dot(a_ref[...], w_ref[...])
    o_ref[...] = (jax.nn.sigmoid(g_ref[...].astype(F32)) * y).astype(o_ref.dtype)


def _ssd_out(ys, w_ssd, l, proj, gate_off, tm, tn):
    m, k1 = ys.shape
    d = w_ssd.shape[2]
    assert gate_off % tn == 0
    return pl.pallas_call(
        _ssd_out_kernel,
        grid=(m // tm, d // tn),
        in_specs=[
            pl.BlockSpec((tm, k1), lambda i, j: (i, 0)),
            pl.BlockSpec((None, k1, tn), lambda i, j: (l, 0, j)),
            pl.BlockSpec((tm, tn), lambda i, j: (i, gate_off // tn + j)),
        ],
        out_specs=pl.BlockSpec((tm, tn), lambda i, j: (i, j)),
        out_shape=jax.ShapeDtypeStruct((m, d), BF16),
        compiler_params=_cparams(("parallel", "parallel")),
        name="ssd_out",
    )(ys, w_ssd, proj)


def _mla_out_kernel(a_ref, w_ref, g_ref, p_ref, o_ref):
    y = _dot(a_ref[...], w_ref[...])
    o_ref[...] = (p_ref[...].astype(F32) + jax.nn.sigmoid(g_ref[...].astype(F32)) * y).astype(o_ref.dtype)


def _mla_out(og, w_mla, l, proj, gate_off, part, tm, tn):
    m, k2 = og.shape
    d = w_mla.shape[2]
    assert gate_off % tn == 0
    return pl.pallas_call(
        _mla_out_kernel,
        grid=(m // tm, d // tn),
        in_specs=[
            pl.BlockSpec((tm, k2), lambda i, j: (i, 0)),
            pl.BlockSpec((None, k2, tn), lambda i, j: (l, 0, j)),
            pl.BlockSpec((tm, tn), lambda i, j: (i, gate_off // tn + j)),
            pl.BlockSpec((tm, tn), lambda i, j: (i, j)),
        ],
        out_specs=pl.BlockSpec((tm, tn), lambda i, j: (i, j)),
        out_shape=jax.ShapeDtypeStruct((m, d), BF16),
        compiler_params=_cparams(("parallel", "parallel")),
        name="mla_out",
    )(og, w_mla, proj, part)


def _out_kernel(a_ref, w_ref, x_ref, pw_ref, gate_ref, o_ref, *, nk):
    k = pl.program_id(1)

    @pl.when(k == 0)
    def _():
        o_ref[...] = jnp.zeros_like(o_ref)

    o_ref[...] += _dot(a_ref[...], w_ref[...])

    @pl.when(k == nk - 1)
    def _():
        y = o_ref[...]
        yn = y * lax.rsqrt(jnp.mean(y * y, axis=-1, keepdims=True) + NORM_EPS) * pw_ref[...]
        o_ref[...] = x_ref[...] + gate_ref[...] * yn


def _out_proj(merged, w_out, l, x2, post_w, mod4, seq, tm, tk):
    m, d = x2.shape
    nk = d // tk
    per_b = seq // tm
    return pl.pallas_call(
        functools.partial(_out_kernel, nk=nk),
        grid=(m // tm, nk),
        in_specs=[
            pl.BlockSpec((tm, tk), lambda i, k: (i, k)),
            pl.BlockSpec((None, tk, d), lambda i, k: (l, k, 0)),
            pl.BlockSpec((tm, d), lambda i, k: (i, 0)),
            pl.BlockSpec((1, d), lambda i, k: (0, 0)),
            pl.BlockSpec((None, None, 1, d), lambda i, k: (i // per_b, 2, 0, 0)),
        ],
        out_specs=pl.BlockSpec((tm, d), lambda i, k: (i, 0)),
        out_shape=jax.ShapeDtypeStruct((m, d), F32),
        compiler_params=_cparams(("parallel", "arbitrary")),
        name="out_proj",
    )(merged, w_out, x2, post_w.reshape(1, d), mod4)


def _proj_layout(d, d_inner, nh, n_mla_heads, q_lora, kv_lora):
    gn = SSD_N_GROUPS * SSD_D_STATE
    widths = [("z", d_inner), ("xs", d_inner), ("bm", gn), ("cm", gn),
              ("gate", n_mla_heads * MLA_V_DIM), ("merge", 2 * d), ("cq", q_lora), ("ckv", kv_lora),
              ("kr", 2 * MLA_QK_ROPE), ("dt", nh)]
    offs, o = {}, 0
    for name, w in widths:
        offs[name] = o
        o += w
    return offs, o


def _cast_kernel(w_ref, o_ref):
    o_ref[...] = w_ref[...].astype(o_ref.dtype)


def _cast_bf16(w3):
    depth, k, n = w3.shape
    tr = max(2 * SUBLANES, min(k, CAST_BLOCK_BYTES // (n * 4)))
    assert k % tr == 0
    return pl.pallas_call(
        _cast_kernel,
        grid=(depth, k // tr),
        in_specs=[pl.BlockSpec((None, tr, n), lambda l, r: (l, r, 0))],
        out_specs=pl.BlockSpec((None, tr, n), lambda l, r: (l, r, 0)),
        out_shape=jax.ShapeDtypeStruct((depth, k, n), BF16),
        compiler_params=_cparams(("parallel", "parallel")),
        name="cast_bf16",
    )(w3)


def _w_in_kernel(src_ref, mode_ref, *refs, n_chunks):
    o_ref = refs[n_chunks]
    j = pl.program_id(1)
    half = MLA_QK_ROPE // 2
    parts = []
    for c in range(n_chunks):
        blk = refs[c][...]
        mode = mode_ref[j * n_chunks + c]
        rot = jnp.concatenate([-blk[half:], blk[:half]], axis=0)
        v = jnp.where(mode == W_IN_ROTATE, rot, blk)
        parts.append(jnp.where(mode == W_IN_ZERO, 0.0, v))
    o_ref[...] = jnp.concatenate(parts, axis=0).T.astype(BF16)


def _prep_w_in(w_in3, offs, n_tot, n_pad, d, d_inner, nh, n_mla_heads, q_lora, kv_lora):
    depth, k, n_src = w_in3.shape
    gn = SSD_N_GROUPS * SSD_D_STATE
    conv_dim = d_inner + 2 * gn
    s_xbc = d_inner
    s_dt = s_xbc + conv_dim
    s_cq = s_dt + nh
    s_ckv = s_cq + q_lora
    s_kr = s_ckv + kv_lora
    s_gate = s_kr + MLA_QK_ROPE
    s_merge = s_gate + n_mla_heads * MLA_V_DIM
    assert s_merge + 2 * d == n_src and MLA_QK_ROPE == W_IN_CHUNK
    segments = (
        (offs["z"], 0, d_inner + conv_dim, W_IN_COPY),
        (offs["gate"], s_gate, n_mla_heads * MLA_V_DIM + 2 * d, W_IN_COPY),
        (offs["cq"], s_cq, q_lora + kv_lora, W_IN_COPY),
        (offs["kr"], s_kr, MLA_QK_ROPE, W_IN_COPY),
        (offs["kr"] + MLA_QK_ROPE, s_kr, MLA_QK_ROPE, W_IN_ROTATE),
        (offs["dt"], s_dt, nh, W_IN_COPY),
    )
    n_dst = n_pad // W_IN_CHUNK
    src_tab = [0] * n_dst
    mode_tab = [W_IN_ZERO] * n_dst
    for dst, src, width, mode in segments:
        assert dst % W_IN_CHUNK == 0 and src % W_IN_CHUNK == 0 and width % W_IN_CHUNK == 0
        for t in range(width // W_IN_CHUNK):
            src_tab[dst // W_IN_CHUNK + t] = src // W_IN_CHUNK + t
            mode_tab[dst // W_IN_CHUNK + t] = mode
    n_chunks = W_IN_COLS // W_IN_CHUNK
    w_t = jnp.swapaxes(w_in3, 1, 2)

    def chunk_spec(c):
        return pl.BlockSpec((None, W_IN_CHUNK, k), lambda l, j, src, mode: (l, src[j * n_chunks + c], 0))

    return pl.pallas_call(
        functools.partial(_w_in_kernel, n_chunks=n_chunks),
        grid_spec=pltpu.PrefetchScalarGridSpec(
            num_scalar_prefetch=2,
            grid=(depth, n_pad // W_IN_COLS),
            in_specs=[chunk_spec(c) for c in range(n_chunks)],
            out_specs=pl.BlockSpec((None, k, W_IN_COLS), lambda l, j, src, mode: (l, 0, j)),
        ),
        out_shape=jax.ShapeDtypeStruct((depth, k, n_pad), BF16),
        compiler_params=_cparams(("parallel", "parallel")),
        name="w_in_prep",
    )(jnp.asarray(src_tab, jnp.int32), jnp.asarray(mode_tab, jnp.int32), *([w_t] * n_chunks))


def _prep_w_q(w_q_up3, n_mla_heads):
    depth, k, _ = w_q_up3.shape
    qk = MLA_QK_NOPE + MLA_QK_ROPE
    half = MLA_QK_ROPE // 2
    w = w_q_up3.reshape(depth, k, n_mla_heads, qk) * (qk ** -0.5 * LOG2_E)
    nope, rope = w[..., :MLA_QK_NOPE], w[..., MLA_QK_NOPE:]
    rot = jnp.concatenate([-rope[..., half:], rope[..., :half]], axis=-1)
    return jnp.concatenate([nope, rope, rot], axis=-1).reshape(
        depth, k, n_mla_heads * 2 * LANES).astype(BF16)


def _round_up(v, mult):
    return (v + mult - 1) // mult * mult


def kernel(x, c, positions, ada_w, ada_b, pre_norm_w, post_norm_w, w_in, conv_w, conv_b, dt_bias,
           a_log, d_skip, ssd_norm_w, q_norm_w, w_q_up, kv_norm_w, w_kv_up, w_ssd_proj, w_mla_proj,
           w_out):
    batch, seq, d = x.shape
    depth = ada_w.shape[0]
    m = batch * seq
    d_inner = ssd_norm_w.shape[1]
    nh = dt_bias.shape[1]
    q_lora = q_norm_w.shape[1]
    kv_lora = kv_norm_w.shape[1]
    n_mla = w_mla_proj.shape[1] // MLA_V_DIM
    offs, n_tot = _proj_layout(d, d_inner, nh, n_mla, q_lora, kv_lora)

    tm_big = min(1024, m)
    tn_proj = 1024
    n_pad = _round_up(n_tot, tn_proj)

    c_pad = jnp.zeros((SUBLANES, d), F32).at[:batch].set(c)
    mod = _ada(c_pad, ada_w, ada_b)
    cos, sin = _rope_tables(positions)

    w_in_b = _prep_w_in(w_in, offs, n_tot, n_pad, d, d_inner, nh, n_mla, q_lora, kv_lora)
    w_q_b = _prep_w_q(w_q_up, n_mla)
    w_kv_b = _cast_bf16(w_kv_up)
    w_ssd_b = _cast_bf16(w_ssd_proj)
    w_mla_b = _cast_bf16(w_mla_proj)
    w_out_b = _cast_bf16(w_out)

    x2 = x.reshape(m, d)
    for l in range(depth):
        mod4 = mod[l, :batch].reshape(batch, 3, 1, d)
        h = _prenorm(x2, pre_norm_w[l], mod4, seq)
        proj = _matmul(h, w_in_b, l, BF16, tm_big, tn_proj)

        pre = _ssd_pre(proj, offs["dt"], dt_bias[l], a_log[l], d_inner, tp=min(1024, seq))
        ys = _ssd(proj, offs, pre, conv_w[l], conv_b[l], d_skip[l], ssd_norm_w[l], batch, seq,
                  t_blk=min(512, seq))

        q = _norm_matmul(proj, offs["cq"], q_lora, q_norm_w[l], w_q_b, l,
                         tm_big, min(2048, n_mla * 2 * LANES), cos, sin, name="q_up")
        kv = _norm_matmul(proj, offs["ckv"], kv_lora, kv_norm_w[l], w_kv_b, l,
                          tm_big, min(4096, n_mla * 2 * LANES), name="kv_up")
        kr = _krope(proj, offs["kr"], cos, sin)
        tq = min(1024, seq)
        og = _flash(q, kv, kr, proj, offs["gate"], batch, seq, n_mla, tq, min(512, tq))

        part = _ssd_out(ys, w_ssd_b, l, proj, offs["merge"], tm_big, min(512, d))
        merged = _mla_out(og, w_mla_b, l, proj, offs["merge"] + d, part, tm_big, min(1024, d))
        x2 = _out_proj(merged, w_out_b, l, x2, post_norm_w[l], mod4, seq,
                       min(512, seq), min(512, d))
    return x2.reshape(batch, seq, d)
```

```python
import functools
import math

import jax
import jax.numpy as jnp
from jax import lax
from jax.experimental import pallas as pl
from jax.experimental.pallas import tpu as pltpu

SSD_HEAD_DIM = 64
SSD_N_GROUPS = 8
SSD_D_STATE = 128
SSD_CHUNK = 128
SSD_CONV_WIDTH = 4
MLA_QK_NOPE = 128
MLA_QK_ROPE = 64
MLA_V_DIM = 128
ROPE_THETA = 10000.0
NORM_EPS = 1e-6
LOG2_E = math.log2(math.e)

LANES = 128
SUBLANES = 8
CONV_HALO = 16
VMEM_LIMIT = 56 * 1024 * 1024
CAST_BLOCK_BYTES = 8 * 1024 * 1024
W_IN_CHUNK = 64
W_IN_COLS = 512
W_IN_COPY, W_IN_ROTATE, W_IN_ZERO = 0, 1, 2

F32 = jnp.float32
BF16 = jnp.bfloat16


def _cparams(sem):
    return pltpu.CompilerParams(dimension_semantics=sem, vmem_limit_bytes=VMEM_LIMIT)


def _silu(v):
    h = 0.5 * v
    return h + h * jnp.tanh(h)


def _split3(v):
    hi = v.astype(BF16)
    r1 = v - hi.astype(F32)
    mid = r1.astype(BF16)
    lo = (r1 - mid.astype(F32)).astype(BF16)
    return hi, mid, lo


def _dot(a, b):
    return jnp.dot(a, b, preferred_element_type=F32)


def _ada_kernel(c_ref, w_ref, b_ref, o_ref):
    @pl.when(pl.program_id(1) == 0)
    def _():
        o_ref[...] = jnp.broadcast_to(b_ref[...], o_ref.shape)

    a = _silu(c_ref[...]).astype(BF16)
    o_ref[...] += _dot(a, w_ref[...].astype(BF16))


def _ada(c_pad, ada_w, ada_b):
    depth, d, n = ada_w.shape
    tk = min(256, d)
    return pl.pallas_call(
        _ada_kernel,
        grid=(depth, d // tk),
        in_specs=[
            pl.BlockSpec((SUBLANES, tk), lambda l, k: (0, k)),
            pl.BlockSpec((None, tk, n), lambda l, k: (l, k, 0)),
            pl.BlockSpec((None, 1, n), lambda l, k: (l, 0, 0)),
        ],
        out_specs=pl.BlockSpec((None, SUBLANES, n), lambda l, k: (l, 0, 0)),
        out_shape=jax.ShapeDtypeStruct((depth, SUBLANES, n), F32),
        compiler_params=_cparams(("parallel", "arbitrary")),
        name="ada_mod",
    )(c_pad, ada_w, ada_b.reshape(depth, 1, n))


def _prenorm_kernel(x_ref, w_ref, shift_ref, scale_ref, o_ref):
    x = x_ref[...]
    y = x * lax.rsqrt(jnp.mean(x * x, axis=-1, keepdims=True) + NORM_EPS)
    o_ref[...] = (y * w_ref[...] * (1.0 + scale_ref[...]) + shift_ref[...]).astype(BF16)


def _prenorm(x2, w, mod4, seq):
    m, d = x2.shape
    tm = min(512, seq)
    per_b = seq // tm
    return pl.pallas_call(
        _prenorm_kernel,
        grid=(m // tm,),
        in_specs=[
            pl.BlockSpec((tm, d), lambda i: (i, 0)),
            pl.BlockSpec((1, d), lambda i: (0, 0)),
            pl.BlockSpec((None, None, 1, d), lambda i: (i // per_b, 0, 0, 0)),
            pl.BlockSpec((None, None, 1, d), lambda i: (i // per_b, 1, 0, 0)),
        ],
        out_specs=pl.BlockSpec((tm, d), lambda i: (i, 0)),
        out_shape=jax.ShapeDtypeStruct((m, d), BF16),
        compiler_params=_cparams(("parallel",)),
        name="prenorm",
    )(x2, w.reshape(1, d), mod4, mod4)


def _mm_kernel(a_ref, w_ref, o_ref):
    o_ref[...] = _dot(a_ref[...], w_ref[...]).astype(o_ref.dtype)


def _matmul(a, w3, l, out_dtype, tm, tn):
    m, k = a.shape
    n = w3.shape[2]
    return pl.pallas_call(
        _mm_kernel,
        grid=(m // tm, n // tn),
        in_specs=[
            pl.BlockSpec((tm, k), lambda i, j: (i, 0)),
            pl.BlockSpec((None, k, tn), lambda i, j: (l, 0, j)),
        ],
        out_specs=pl.BlockSpec((tm, tn), lambda i, j: (i, j)),
        out_shape=jax.ShapeDtypeStruct((m, n), out_dtype),
        compiler_params=_cparams(("parallel", "parallel")),
        name="in_proj",
    )(a, w3)


def _rope_block(a, cos, sin):
    return a * cos + pltpu.roll(a, LANES // 2, axis=1) * sin


def _norm_mm_kernel(a_ref, nw_ref, w_ref, *rest, rope, tn):
    if rope:
        cos_ref, sin_ref, o_ref, an_sc = rest
    else:
        o_ref, an_sc = rest

    @pl.when(pl.program_id(1) == 0)
    def _():
        a = a_ref[...].astype(F32)
        y = a * lax.rsqrt(jnp.mean(a * a, axis=-1, keepdims=True) + NORM_EPS)
        an_sc[...] = (y * nw_ref[...]).astype(BF16)

    acc = _dot(an_sc[...], w_ref[...])
    if rope:
        cos = cos_ref[...]
        sin = sin_ref[...]
        for hh in range(tn // (2 * LANES)):
            c0 = hh * 2 * LANES
            o_ref[:, c0:c0 + LANES] = acc[:, c0:c0 + LANES].astype(o_ref.dtype)
            o_ref[:, c0 + LANES:c0 + 2 * LANES] = _rope_block(
                acc[:, c0 + LANES:c0 + 2 * LANES], cos, sin).astype(o_ref.dtype)
    else:
        o_ref[...] = acc.astype(o_ref.dtype)


def _norm_matmul(proj, col_off, k, norm_w, w, l, tm, tn, cos=None, sin=None, name="norm_mm"):
    m = proj.shape[0]
    n = w.shape[2]
    rope = cos is not None
    assert col_off % k == 0
    in_specs = [
        pl.BlockSpec((tm, k), lambda i, j: (i, col_off // k)),
        pl.BlockSpec((1, k), lambda i, j: (0, 0)),
        pl.BlockSpec((None, k, tn), lambda i, j: (l, 0, j)),
    ]
    args = [proj, norm_w.reshape(1, k), w]
    if rope:
        in_specs += [pl.BlockSpec((tm, LANES), lambda i, j: (i, 0))] * 2
        args += [cos, sin]
    return pl.pallas_call(
        functools.partial(_norm_mm_kernel, rope=rope, tn=tn),
        grid=(m // tm, n // tn),
        in_specs=in_specs,
        out_specs=pl.BlockSpec((tm, tn), lambda i, j: (i, j)),
        out_shape=jax.ShapeDtypeStruct((m, n), BF16),
        scratch_shapes=[pltpu.VMEM((tm, k), BF16)],
        compiler_params=_cparams(("parallel", "arbitrary")),
        name=name,
    )(*args)


def _rope_tab_kernel(pos_ref, freq_ref, cos_ref, sin_ref):
    ang = pos_ref[...].astype(F32) * freq_ref[...]
    lane = lax.broadcasted_iota(jnp.int32, ang.shape, 1)
    keep = lane < MLA_QK_ROPE
    cos_ref[...] = jnp.where(keep, jnp.cos(ang), 0.0)
    sin_ref[...] = jnp.where(keep, jnp.sin(ang), 0.0)


def _rope_tables(positions):
    m = positions.size
    tm = min(1024, m)
    half = MLA_QK_ROPE // 2
    inv_freq = ROPE_THETA ** (-(jnp.arange(0, MLA_QK_ROPE, 2, dtype=F32) / MLA_QK_ROPE))
    freq = jnp.concatenate([inv_freq, inv_freq, jnp.zeros((LANES - 2 * half,), F32)]).reshape(1, LANES)
    return pl.pallas_call(
        _rope_tab_kernel,
        grid=(m // tm,),
        in_specs=[pl.BlockSpec((tm, 1), lambda i: (i, 0)),
                  pl.BlockSpec((1, LANES), lambda i: (0, 0))],
        out_specs=[pl.BlockSpec((tm, LANES), lambda i: (i, 0))] * 2,
        out_shape=[jax.ShapeDtypeStruct((m, LANES), F32)] * 2,
        compiler_params=_cparams(("parallel",)),
        name="rope_tables",
    )(positions.reshape(m, 1), freq)


def _krope_kernel(a_ref, cos_ref, sin_ref, o_ref):
    o_ref[...] = _rope_block(a_ref[...].astype(F32), cos_ref[...], sin_ref[...]).astype(BF16)


def _krope(proj, col_off, cos, sin):
    m = proj.shape[0]
    tm = min(1024, m)
    return pl.pallas_call(
        _krope_kernel,
        grid=(m // tm,),
        in_specs=[pl.BlockSpec((tm, LANES), lambda i: (i, col_off // LANES)),
                  pl.BlockSpec((tm, LANES), lambda i: (i, 0)),
                  pl.BlockSpec((tm, LANES), lambda i: (i, 0))],
        out_specs=pl.BlockSpec((tm, LANES), lambda i: (i, 0)),
        out_shape=jax.ShapeDtypeStruct((m, LANES), BF16),
        compiler_params=_cparams(("parallel",)),
        name="k_rope",
    )(proj, cos, sin)


def _ssd_pre_kernel(dt_ref, bias_ref, alog_ref, tri_ref, efull_ref,
                    dtb_ref, eacs_ref, toend_ref, acsrot_ref, acst_ref, decend_ref, *, tp, r):
    L = SSD_CHUNK
    dt = jax.nn.softplus(dt_ref[...].astype(F32) + bias_ref[...])
    dtb_ref[...] = dt.astype(BF16)
    a = dt * (-jnp.exp(alog_ref[...])) * LOG2_E
    tri = tri_ref[...]
    lasts = []
    for c in range(tp // L):
        rows = slice(c * L, (c + 1) * L)
        hi, mid, lo = _split3(a[rows])
        acs = _dot(tri, hi) + _dot(tri, mid) + _dot(tri, lo)
        last = acs[L - 1:L, :]
        lasts.append(last)
        eacs_ref[rows, :] = jnp.exp2(acs).astype(BF16)
        toend_ref[rows, :] = jnp.exp2(last - acs).astype(BF16)
        acst_ref[c] = acs.T
        for g in range(SSD_N_GROUPS):
            shift = (LANES - g * r) % LANES
            acsrot_ref[g, rows, :] = acs if shift == 0 else pltpu.roll(acs, shift, axis=1)
    n_c = tp // L
    pad = [jnp.zeros((SUBLANES - n_c % SUBLANES, LANES), F32)] if n_c % SUBLANES else []
    el = jnp.exp2(jnp.concatenate(lasts + pad, axis=0))
    hi, mid, lo = _split3(el)
    e = efull_ref[...]
    dec = _dot(hi, e) + _dot(mid, e) + _dot(lo, e)
    for c in range(n_c):
        decend_ref[c] = dec[c:c + 1, :]


def _ssd_pre(proj, col_off, dt_bias, a_log, d_inner, tp):
    m = proj.shape[0]
    nh = dt_bias.shape[0]
    assert nh == LANES and col_off % LANES == 0
    r = nh // SSD_N_GROUPS
    L = SSD_CHUNK
    tri = (jnp.arange(L)[:, None] >= jnp.arange(L)[None, :]).astype(BF16)
    efull = (jnp.arange(nh)[:, None] == (jnp.arange(d_inner)[None, :] // SSD_HEAD_DIM)).astype(BF16)
    nct = m // L
    return pl.pallas_call(
        functools.partial(_ssd_pre_kernel, tp=tp, r=r),
        grid=(m // tp,),
        in_specs=[
            pl.BlockSpec((tp, LANES), lambda i: (i, col_off // LANES)),
            pl.BlockSpec((1, LANES), lambda i: (0, 0)),
            pl.BlockSpec((1, LANES), lambda i: (0, 0)),
            pl.BlockSpec((L, L), lambda i: (0, 0)),
            pl.BlockSpec((nh, d_inner), lambda i: (0, 0)),
        ],
        out_specs=[
            pl.BlockSpec((tp, LANES), lambda i: (i, 0)),
            pl.BlockSpec((tp, LANES), lambda i: (i, 0)),
            pl.BlockSpec((tp, LANES), lambda i: (i, 0)),
            pl.BlockSpec((SSD_N_GROUPS, tp, LANES), lambda i: (0, i, 0)),
            pl.BlockSpec((tp // L, LANES, L), lambda i: (i, 0, 0)),
            pl.BlockSpec((tp // L, 1, d_inner), lambda i: (i, 0, 0)),
        ],
        out_shape=[
            jax.ShapeDtypeStruct((m, LANES), BF16),
            jax.ShapeDtypeStruct((m, LANES), BF16),
            jax.ShapeDtypeStruct((m, LANES), BF16),
            jax.ShapeDtypeStruct((SSD_N_GROUPS, m, LANES), F32),
            jax.ShapeDtypeStruct((nct, LANES, L), F32),
            jax.ShapeDtypeStruct((nct, 1, d_inner), F32),
        ],
        compiler_params=_cparams(("parallel",)),
        name="ssd_pre",
    )(proj, dt_bias.reshape(1, nh), a_log.reshape(1, nh), tri, efull)


def _ssd_kernel(xs_ref, bm_ref, cm_ref, z_ref, dtb_ref, eacs_ref, toend_ref, acs_ref, acst_ref,
                decend_ref, cwx_ref, cbx_ref, cwbc_ref, cbbc_ref, dskip_ref,
                nw_ref, e_ref, shift_ref, o_ref,
                xin_sc, bcin_sc, state_sc, *, t_blk, r):
    L = SSD_CHUNK
    K = SSD_CONV_WIDTH
    P = SSD_HEAD_DIM
    N = SSD_D_STATE
    gw = r * P
    halo = CONV_HALO

    @pl.when(pl.program_id(2) == 0)
    def _():
        state_sc[...] = jnp.zeros_like(state_sc)
        xin_sc[0:halo, :] = jnp.zeros((halo, gw), BF16)
        bcin_sc[0:halo, :] = jnp.zeros((halo, 2 * N), BF16)

    xin_sc[halo:halo + t_blk, :] = xs_ref[...]
    bcin_sc[halo:halo + t_blk, 0:N] = bm_ref[...]
    bcin_sc[halo:halo + t_blk, N:2 * N] = cm_ref[...]

    def conv(in_sc, rows_ext, w_ref, b_ref):
        xe = in_sc[rows_ext, :]
        sh = _dot(shift_ref[...], xe)
        acc = b_ref[...] + w_ref[K - 1:K, :] * xe[halo:halo + L, :].astype(F32)
        for k in range(K - 1):
            acc = acc + w_ref[k:k + 1, :] * sh[k * L:(k + 1) * L, :]
        return _silu(acc)

    li = lax.broadcasted_iota(jnp.int32, (L, L), 0)
    si = lax.broadcasted_iota(jnp.int32, (L, L), 1)
    causal = li >= si
    lane = lax.broadcasted_iota(jnp.int32, (L, 2 * P), 1)
    lo_half = lane < P

    def chunk(c, carry):
        rows = pl.ds(pl.multiple_of(c * L, L), L)
        rows_ext = pl.ds(pl.multiple_of(c * L, L), halo + L)
        xc = conv(xin_sc, rows_ext, cwx_ref, cbx_ref)
        bc_c = conv(bcin_sc, rows_ext, cwbc_ref, cbbc_ref).astype(BF16)
        b_c = bc_c[:, 0:N]
        c_c = bc_c[:, N:2 * N]
        e = e_ref[...]
        dt_e = _dot(dtb_ref[rows, :], e)
        eacs_e = _dot(eacs_ref[rows, :], e)
        toend_e = _dot(toend_ref[rows, :], e)
        xdt = xc * dt_e
        xdt_b = xdt.astype(BF16)
        xw_b = (xdt * toend_e).astype(BF16)
        cb = lax.dot_general(c_c, b_c, (((1,), (1,)), ((), ())), preferred_element_type=F32)
        acs = acs_ref[rows, :]
        acst = acst_ref[c]
        st = state_sc[...]
        y = _dot(c_c, st.astype(BF16)) * eacs_e
        pieces = []
        for jp in range(r // 2):
            xp = xdt_b[:, jp * 2 * P:(jp + 1) * 2 * P]
            acc = None
            for u in range(2):
                j = 2 * jp + u
                colb = jnp.broadcast_to(acs[:, j:j + 1], (L, L))
                rowb = jnp.broadcast_to(acst[j:j + 1, :], (L, L))
                dec = jnp.exp2(jnp.where(causal, colb - rowb, -jnp.inf))
                m_h = (cb * dec).astype(BF16)
                xm = jnp.where(lo_half if u == 0 else jnp.logical_not(lo_half), xp, jnp.zeros_like(xp))
                d = _dot(m_h, xm)
                acc = d if acc is None else acc + d
            pieces.append(acc)
        y = y + jnp.concatenate(pieces, axis=1)
        upd = lax.dot_general(b_c, xw_b, (((0,), (0,)), ((), ())), preferred_element_type=F32)
        state_sc[...] = st * decend_ref[c] + upd
        y = y + dskip_ref[...] * xc
        yg = y * _silu(z_ref[rows, :].astype(F32))
        ms = jnp.mean(yg * yg, axis=-1, keepdims=True)
        o_ref[rows, :] = (yg * lax.rsqrt(ms + NORM_EPS) * nw_ref[...]).astype(o_ref.dtype)
        return carry

    lax.fori_loop(0, t_blk // L, chunk, 0)
    xin_sc[0:halo, :] = xin_sc[t_blk:t_blk + halo, :]
    bcin_sc[0:halo, :] = bcin_sc[t_blk:t_blk + halo, :]


def _ssd(proj, offs, pre, conv_w, conv_b, d_skip, ssd_norm_w, batch, seq, t_blk):
    m = proj.shape[0]
    d_inner = ssd_norm_w.shape[0]
    nh = d_skip.shape[0]
    G = SSD_N_GROUPS
    N = SSD_D_STATE
    L = SSD_CHUNK
    r = nh // G
    gw = r * SSD_HEAD_DIM
    assert gw % LANES == 0 and r % 2 == 0 and r % SUBLANES == 0
    dtb, eacs, toend, acsrot, acst, decend = pre
    nt = seq // t_blk
    ncb = t_blk // L
    e_all = (jnp.arange(LANES)[None, :, None]
             == (jnp.arange(G)[:, None, None] * r + jnp.arange(gw)[None, None, :] // SSD_HEAD_DIM)).astype(BF16)
    dskip_e = jnp.repeat(d_skip.astype(F32), SSD_HEAD_DIM).reshape(1, d_inner)

    def bc_pairs(a):
        b_part = a[:, d_inner:d_inner + G * N].reshape(-1, G, 1, N)
        c_part = a[:, d_inner + G * N:].reshape(-1, G, 1, N)
        return jnp.concatenate([b_part, c_part], axis=2).reshape(-1, G * 2 * N)

    cb2 = conv_b.reshape(1, -1)
    cwx, cbx = conv_w[:, :d_inner], cb2[:, :d_inner]
    cwbc, cbbc = bc_pairs(conv_w), bc_pairs(cb2)
    kw = SSD_CONV_WIDTH
    tt = jnp.arange((kw - 1) * L)
    shift = (jnp.arange(CONV_HALO + L)[None, :]
             == (CONV_HALO + tt % L - (kw - 1) + tt // L)[:, None]).astype(BF16)
    for o in (offs["xs"], offs["z"]):
        assert o % gw == 0
    row = lambda b, g, t: b * nt + t
    in_specs = [
        pl.BlockSpec((t_blk, gw), lambda b, g, t: (row(b, g, t), offs["xs"] // gw + g)),
        pl.BlockSpec((t_blk, N), lambda b, g, t: (row(b, g, t), offs["bm"] // N + g)),
        pl.BlockSpec((t_blk, N), lambda b, g, t: (row(b, g, t), offs["cm"] // N + g)),
        pl.BlockSpec((t_blk, gw), lambda b, g, t: (row(b, g, t), offs["z"] // gw + g)),
        pl.BlockSpec((t_blk, LANES), lambda b, g, t: (row(b, g, t), 0)),
        pl.BlockSpec((t_blk, LANES), lambda b, g, t: (row(b, g, t), 0)),
        pl.BlockSpec((t_blk, LANES), lambda b, g, t: (row(b, g, t), 0)),
        pl.BlockSpec((None, t_blk, LANES), lambda b, g, t: (g, row(b, g, t), 0)),
        pl.BlockSpec((ncb, r, L), lambda b, g, t: (row(b, g, t), g, 0)),
        pl.BlockSpec((ncb, 1, gw), lambda b, g, t: (row(b, g, t), 0, g)),
        pl.BlockSpec((SSD_CONV_WIDTH, gw), lambda b, g, t: (0, g)),
        pl.BlockSpec((1, gw), lambda b, g, t: (0, g)),
        pl.BlockSpec((SSD_CONV_WIDTH, 2 * N), lambda b, g, t: (0, g)),
        pl.BlockSpec((1, 2 * N), lambda b, g, t: (0, g)),
        pl.BlockSpec((1, gw), lambda b, g, t: (0, g)),
        pl.BlockSpec((1, gw), lambda b, g, t: (0, g)),
        pl.BlockSpec((None, LANES, gw), lambda b, g, t: (g, 0, 0)),
        pl.BlockSpec(((kw - 1) * L, CONV_HALO + L), lambda b, g, t: (0, 0)),
    ]
    return pl.pallas_call(
        functools.partial(_ssd_kernel, t_blk=t_blk, r=r),
        grid=(batch, G, nt),
        in_specs=in_specs,
        out_specs=pl.BlockSpec((t_blk, gw), lambda b, g, t: (row(b, g, t), g)),
        out_shape=jax.ShapeDtypeStruct((m, d_inner), BF16),
        scratch_shapes=[
            pltpu.VMEM((t_blk + CONV_HALO, gw), BF16),
            pltpu.VMEM((t_blk + CONV_HALO, 2 * N), BF16),
            pltpu.VMEM((N, gw), F32),
        ],
        compiler_params=_cparams(("parallel", "parallel", "arbitrary")),
        name="ssd_scan",
    )(proj, proj, proj, proj, dtb, eacs, toend, acsrot, acst, decend,
      cwx, cbx, cwbc, cbbc, dskip_e, ssd_norm_w.reshape(1, d_inner), e_all, shift)


def _flash_kernel(q_ref, kn_ref, kr_ref, v_ref, g_ref, o_ref, vt_sc, qt_sc, sa_sc, sb_sc, xa_sc, xb_sc, m_sc, l_sc,
                  acc_sc, *, tq, tk, seq):
    n_sub = tq // tk
    tqs = min(2 * LANES, tq)
    assert n_sub == 2

    for c in range(seq // tk):
        vt_sc[c] = v_ref[c * tk:(c + 1) * tk, :].astype(F32).T.astype(BF16)

    def step(kb_next, bufs_next, diag_next, kb, bufs, diag):
        nxt, cur = [], []
        if kb_next is not None:
            rows = pl.ds(pl.multiple_of(kb_next * tk, tk), tk)
            k = jnp.concatenate([kn_ref[rows, :], kr_ref[rows, :]], axis=1)
            k0 = 0 if diag_next is None else diag_next * tk
            nxt = [(k, k0, c0) for c0 in range(k0, tq, tqs)]
        if kb is not None:
            vt = vt_sc[kb]
            cur = list(range(0 if diag is None else diag * tk, tq, tqs))
        for i in range(max(len(nxt), len(cur))):
            if i < len(nxt):
                k, k0, c0 = nxt[i]
                st = _dot(k, qt_sc[c0 // tqs])
                if diag_next is not None and c0 < k0 + tk:
                    ki = lax.broadcasted_iota(jnp.int32, (tk, tqs), 0) + k0
                    qj = lax.broadcasted_iota(jnp.int32, (tk, tqs), 1) + c0
                    st = jnp.where(ki <= qj, st, -jnp.inf)
                bufs_next[0][c0 // tqs] = st
                bufs_next[1][c0 // tqs] = jnp.max(st, axis=0, keepdims=True)
            if i < len(cur):
                ci = cur[i] // tqs
                st = bufs[0][ci]
                m_prev = m_sc[ci]
                m_new = jnp.maximum(m_prev, bufs[1][ci])
                alpha = jnp.exp2(m_prev - m_new)
                p = jnp.exp2(st - m_new)
                l_sc[ci] = alpha * l_sc[ci] + jnp.sum(p, axis=0, keepdims=True)
                acc_sc[ci] = acc_sc[ci] * alpha + _dot(vt, p.astype(BF16))
                m_sc[ci] = m_new

    buf_a, buf_b = (sa_sc, xa_sc), (sb_sc, xb_sc)

    def q_block(qi, carry):
        for ci in range(tq // tqs):
            qsub = pl.ds(pl.multiple_of(qi * tq + ci * tqs, tqs), tqs)
            qt_sc[ci] = q_ref[qsub, :].astype(F32).T.astype(BF16)
        m_sc[...] = jnp.full(m_sc.shape, -jnp.inf, F32)
        l_sc[...] = jnp.zeros(l_sc.shape, F32)
        acc_sc[...] = jnp.zeros(acc_sc.shape, F32)
        d0 = qi * n_sub

        @pl.when(qi == 0)
        def _():
            step(d0, buf_a, 0, None, None, None)

        @pl.when(qi > 0)
        def _():
            step(0, buf_a, None, None, None, None)

            def pair(j):
                step(2 * j + 1, buf_b, None, 2 * j, buf_a, None)
                step(2 * j + 2, buf_a, None, 2 * j + 1, buf_b, None)

            def quad(i, c):
                pair(2 * i)
                pair(2 * i + 1)
                return c

            n_pairs = qi - 1
            lax.fori_loop(0, n_pairs // 2, quad, 0)

            @pl.when(n_pairs % 2 == 1)
            def _():
                pair(n_pairs - 1)

            step(d0 - 1, buf_b, None, d0 - 2, buf_a, None)
            step(d0, buf_a, 0, d0 - 1, buf_b, None)

        step(d0 + 1, buf_b, 1, d0, buf_a, 0)
        step(None, None, None, d0 + 1, buf_b, 1)
        for ci in range(tq // tqs):
            qsub = pl.ds(pl.multiple_of(qi * tq + ci * tqs, tqs), tqs)
            o = (acc_sc[ci] / l_sc[ci]).T
            o_ref[qsub, :] = (o * _silu(g_ref[qsub, :].astype(F32))).astype(o_ref.dtype)
        return carry

    lax.fori_loop(0, seq // tq, q_block, 0)


def _flash(q, kv, kr, proj, gate_off, batch, seq, n_heads, tq, tk):
    m = q.shape[0]
    dq = 2 * LANES
    tqs = min(2 * LANES, tq)
    nch = tq // tqs
    assert gate_off % LANES == 0
    return pl.pallas_call(
        functools.partial(_flash_kernel, tq=tq, tk=tk, seq=seq),
        grid=(batch, n_heads),
        in_specs=[
            pl.BlockSpec((seq, dq), lambda b, h: (b, h)),
            pl.BlockSpec((seq, LANES), lambda b, h: (b, 2 * h)),
            pl.BlockSpec((seq, LANES), lambda b, h: (b, 0)),
            pl.BlockSpec((seq, LANES), lambda b, h: (b, 2 * h + 1)),
            pl.BlockSpec((seq, LANES), lambda b, h: (b, gate_off // LANES + h)),
        ],
        out_specs=pl.BlockSpec((seq, LANES), lambda b, h: (b, h)),
        out_shape=jax.ShapeDtypeStruct((m, n_heads * MLA_V_DIM), BF16),
        scratch_shapes=[pltpu.VMEM((seq // tk, MLA_V_DIM, tk), BF16), pltpu.VMEM((nch, dq, tqs), BF16),
                        pltpu.VMEM((nch, tk, tqs), F32), pltpu.VMEM((nch, tk, tqs), F32),
                        pltpu.VMEM((nch, 1, tqs), F32), pltpu.VMEM((nch, 1, tqs), F32),
                        pltpu.VMEM((nch, 1, tqs), F32), pltpu.VMEM((nch, 1, tqs), F32),
                        pltpu.VMEM((nch, MLA_V_DIM, tqs), F32)],
        compiler_params=_cparams(("parallel", "parallel")),
        name="mla_flash",
    )(q, kv, kr, kv, proj)


def _ssd_out_kernel(a_ref, w_ref, g_ref, o_ref):
    y = _dot(a_ref[...], w_ref[...])
    o_ref[...] = (jax.nn.sigmoid(g_ref[...].astype(F32)) * y).astype(o_ref.dtype)


def _ssd_out(ys, w_ssd, l, proj, gate_off, tm, tn):
    m, k1 = ys.shape
    d = w_ssd.shape[2]
    assert gate_off % tn == 0
    return pl.pallas_call(
        _ssd_out_kernel,
        grid=(m // tm, d // tn),
        in_specs=[
            pl.BlockSpec((tm, k1), lambda i, j: (i, 0)),
            pl.BlockSpec((None, k1, tn), lambda i, j: (l, 0, j)),
            pl.BlockSpec((tm, tn), lambda i, j: (i, gate_off // tn + j)),
        ],
        out_specs=pl.BlockSpec((tm, tn), lambda i, j: (i, j)),
        out_shape=jax.ShapeDtypeStruct((m, d), BF16),
        compiler_params=_cparams(("parallel", "parallel")),
        name="ssd_out",
    )(ys, w_ssd, proj)


def _mla_out_kernel(a_ref, w_ref, g_ref, p_ref, o_ref):
    y = _dot(a_ref[...], w_ref[...])
    o_ref[...] = (p_ref[...].astype(F32) + jax.nn.sigmoid(g_ref[...].astype(F32)) * y).astype(o_ref.dtype)


def _mla_out(og, w_mla, l, proj, gate_off, part, tm, tn):
    m, k2 = og.shape
    d = w_mla.shape[2]
    assert gate_off % tn == 0
    return pl.pallas_call(
        _mla_out_kernel,
        grid=(m // tm, d // tn),
        in_specs=[
            pl.BlockSpec((tm, k2), lambda i, j: (i, 0)),
            pl.BlockSpec((None, k2, tn), lambda i, j: (l, 0, j)),
            pl.BlockSpec((tm, tn), lambda i, j: (i, gate_off // tn + j)),
            pl.BlockSpec((tm, tn), lambda i, j: (i, j)),
        ],
        out_specs=pl.BlockSpec((tm, tn), lambda i, j: (i, j)),
        out_shape=jax.ShapeDtypeStruct((m, d), BF16),
        compiler_params=_cparams(("parallel", "parallel")),
        name="mla_out",
    )(og, w_mla, proj, part)


def _out_kernel(a_ref, w_ref, x_ref, pw_ref, gate_ref, o_ref, *, nk):
    k = pl.program_id(1)

    @pl.when(k == 0)
    def _():
        o_ref[...] = jnp.zeros_like(o_ref)

    o_ref[...] += _dot(a_ref[...], w_ref[...])

    @pl.when(k == nk - 1)
    def _():
        y = o_ref[...]
        yn = y * lax.rsqrt(jnp.mean(y * y, axis=-1, keepdims=True) + NORM_EPS) * pw_ref[...]
        o_ref[...] = x_ref[...] + gate_ref[...] * yn


def _out_proj(merged, w_out, l, x2, post_w, mod4, seq, tm, tk):
    m, d = x2.shape
    nk = d // tk
    per_b = seq // tm
    return pl.pallas_call(
        functools.partial(_out_kernel, nk=nk),
        grid=(m // tm, nk),
        in_specs=[
            pl.BlockSpec((tm, tk), lambda i, k: (i, k)),
            pl.BlockSpec((None, tk, d), lambda i, k: (l, k, 0)),
            pl.BlockSpec((tm, d), lambda i, k: (i, 0)),
            pl.BlockSpec((1, d), lambda i, k: (0, 0)),
            pl.BlockSpec((None, None, 1, d), lambda i, k: (i // per_b, 2, 0, 0)),
        ],
        out_specs=pl.BlockSpec((tm, d), lambda i, k: (i, 0)),
        out_shape=jax.ShapeDtypeStruct((m, d), F32),
        compiler_params=_cparams(("parallel", "arbitrary")),
        name="out_proj",
    )(merged, w_out, x2, post_w.reshape(1, d), mod4)


def _proj_layout(d, d_inner, nh, n_mla_heads, q_lora, kv_lora):
    gn = SSD_N_GROUPS * SSD_D_STATE
    widths = [("z", d_inner), ("xs", d_inner), ("bm", gn), ("cm", gn),
              ("gate", n_mla_heads * MLA_V_DIM), ("merge", 2 * d), ("cq", q_lora), ("ckv", kv_lora),
              ("kr", 2 * MLA_QK_ROPE), ("dt", nh)]
    offs, o = {}, 0
    for name, w in widths:
        offs[name] = o
        o += w
    return offs, o


def _cast_kernel(w_ref, o_ref):
    o_ref[...] = w_ref[...].astype(o_ref.dtype)


def _cast_bf16(w3):
    depth, k, n = w3.shape
    tr = max(2 * SUBLANES, min(k, CAST_BLOCK_BYTES // (n * 4)))
    assert k % tr == 0
    return pl.pallas_call(
        _cast_kernel,
        grid=(depth, k // tr),
        in_specs=[pl.BlockSpec((None, tr, n), lambda l, r: (l, r, 0))],
        out_specs=pl.BlockSpec((None, tr, n), lambda l, r: (l, r, 0)),
        out_shape=jax.ShapeDtypeStruct((depth, k, n), BF16),
        compiler_params=_cparams(("parallel", "parallel")),
        name="cast_bf16",
    )(w3)


def _w_in_kernel(src_ref, mode_ref, *refs, n_chunks):
    o_ref = refs[n_chunks]
    j = pl.program_id(1)
    half = MLA_QK_ROPE // 2
    parts = []
    for c in range(n_chunks):
        blk = refs[c][...]
        mode = mode_ref[j * n_chunks + c]
        rot = jnp.concatenate([-blk[half:], blk[:half]], axis=0)
        v = jnp.where(mode == W_IN_ROTATE, rot, blk)
        parts.append(jnp.where(mode == W_IN_ZERO, 0.0, v))
    o_ref[...] = jnp.concatenate(parts, axis=0).T.astype(BF16)


def _prep_w_in(w_in3, offs, n_tot, n_pad, d, d_inner, nh, n_mla_heads, q_lora, kv_lora):
    depth, k, n_src = w_in3.shape
    gn = SSD_N_GROUPS * SSD_D_STATE
    conv_dim = d_inner + 2 * gn
    s_xbc = d_inner
    s_dt = s_xbc + conv_dim
    s_cq = s_dt + nh
    s_ckv = s_cq + q_lora
    s_kr = s_ckv + kv_lora
    s_gate = s_kr + MLA_QK_ROPE
    s_merge = s_gate + n_mla_heads * MLA_V_DIM
    assert s_merge + 2 * d == n_src and MLA_QK_ROPE == W_IN_CHUNK
    segments = (
        (offs["z"], 0, d_inner + conv_dim, W_IN_COPY),
        (offs["gate"], s_gate, n_mla_heads * MLA_V_DIM + 2 * d, W_IN_COPY),
        (offs["cq"], s_cq, q_lora + kv_lora, W_IN_COPY),
        (offs["kr"], s_kr, MLA_QK_ROPE, W_IN_COPY),
        (offs["kr"] + MLA_QK_ROPE, s_kr, MLA_QK_ROPE, W_IN_ROTATE),
        (offs["dt"], s_dt, nh, W_IN_COPY),
    )
    n_dst = n_pad // W_IN_CHUNK
    src_tab = [0] * n_dst
    mode_tab = [W_IN_ZERO] * n_dst
    for dst, src, width, mode in segments:
        assert dst % W_IN_CHUNK == 0 and src % W_IN_CHUNK == 0 and width % W_IN_CHUNK == 0
        for t in range(width // W_IN_CHUNK):
            src_tab[dst // W_IN_CHUNK + t] = src // W_IN_CHUNK + t
            mode_tab[dst // W_IN_CHUNK + t] = mode
    n_chunks = W_IN_COLS // W_IN_CHUNK
    w_t = jnp.swapaxes(w_in3, 1, 2)

    def chunk_spec(c):
        return pl.BlockSpec((None, W_IN_CHUNK, k), lambda l, j, src, mode: (l, src[j * n_chunks + c], 0))

    return pl.pallas_call(
        functools.partial(_w_in_kernel, n_chunks=n_chunks),
        grid_spec=pltpu.PrefetchScalarGridSpec(
            num_scalar_prefetch=2,
            grid=(depth, n_pad // W_IN_COLS),
            in_specs=[chunk_spec(c) for c in range(n_chunks)],
            out_specs=pl.BlockSpec((None, k, W_IN_COLS), lambda l, j, src, mode: (l, 0, j)),
        ),
        out_shape=jax.ShapeDtypeStruct((depth, k, n_pad), BF16),
        compiler_params=_cparams(("parallel", "parallel")),
        name="w_in_prep",
    )(jnp.asarray(src_tab, jnp.int32), jnp.asarray(mode_tab, jnp.int32), *([w_t] * n_chunks))


def _prep_w_q(w_q_up3, n_mla_heads):
    depth, k, _ = w_q_up3.shape
    qk = MLA_QK_NOPE + MLA_QK_ROPE
    half = MLA_QK_ROPE // 2
    w = w_q_up3.reshape(depth, k, n_mla_heads, qk) * (qk ** -0.5 * LOG2_E)
    nope, rope = w[..., :MLA_QK_NOPE], w[..., MLA_QK_NOPE:]
    rot = jnp.concatenate([-rope[..., half:], rope[..., :half]], axis=-1)
    return jnp.concatenate([nope, rope, rot], axis=-1).reshape(
        depth, k, n_mla_heads * 2 * LANES).astype(BF16)


def _round_up(v, mult):
    return (v + mult - 1) // mult * mult


def kernel(x, c, positions, ada_w, ada_b, pre_norm_w, post_norm_w, w_in, conv_w, conv_b, dt_bias,
           a_log, d_skip, ssd_norm_w, q_norm_w, w_q_up, kv_norm_w, w_kv_up, w_ssd_proj, w_mla_proj,
           w_out):
    batch, seq, d = x.shape
    depth = ada_w.shape[0]
    m = batch * seq
    d_inner = ssd_norm_w.shape[1]
    nh = dt_bias.shape[1]
    q_lora = q_norm_w.shape[1]
    kv_lora = kv_norm_w.shape[1]
    n_mla = w_mla_proj.shape[1] // MLA_V_DIM
    offs, n_tot = _proj_layout(d, d_inner, nh, n_mla, q_lora, kv_lora)

    tm_big = min(1024, m)
    tn_proj = 1024
    n_pad = _round_up(n_tot, tn_proj)

    c_pad = jnp.zeros((SUBLANES, d), F32).at[:batch].set(c)
    mod = _ada(c_pad, ada_w, ada_b)
    cos, sin = _rope_tables(positions)

    w_in_b = _prep_w_in(w_in, offs, n_tot, n_pad, d, d_inner, nh, n_mla, q_lora, kv_lora)
    w_q_b = _prep_w_q(w_q_up, n_mla)
    w_kv_b = _cast_bf16(w_kv_up)
    w_ssd_b = _cast_bf16(w_ssd_proj)
    w_mla_b = _cast_bf16(w_mla_proj)
    w_out_b = _cast_bf16(w_out)

    x2 = x.reshape(m, d)
    for l in range(depth):
        mod4 = mod[l, :batch].reshape(batch, 3, 1, d)
        h = _prenorm(x2, pre_norm_w[l], mod4, seq)
        proj = _matmul(h, w_in_b, l, BF16, tm_big, tn_proj)

        pre = _ssd_pre(proj, offs["dt"], dt_bias[l], a_log[l], d_inner, tp=min(1024, seq))
        ys = _ssd(proj, offs, pre, conv_w[l], conv_b[l], d_skip[l], ssd_norm_w[l], batch, seq,
                  t_blk=min(1024, seq))

        q = _norm_matmul(proj, offs["cq"], q_lora, q_norm_w[l], w_q_b, l,
                         tm_big, min(2048, n_mla * 2 * LANES), cos, sin, name="q_up")
        kv = _norm_matmul(proj, offs["ckv"], kv_lora, kv_norm_w[l], w_kv_b, l,
                          tm_big, min(4096, n_mla * 2 * LANES), name="kv_up")
        kr = _krope(proj, offs["kr"], cos, sin)
        tq = min(1024, seq)
        og = _flash(q, kv, kr, proj, offs["gate"], batch, seq, n_mla, tq, min(512, tq))

        part = _ssd_out(ys, w_ssd_b, l, proj, offs["merge"], tm_big, min(512, d))
        merged = _mla_out(og, w_mla_b, l, proj, offs["merge"] + d, part, tm_big, min(1024, d))
        x2 = _out_proj(merged, w_out_b, l, x2, post_norm_w[l], mod4, seq,
                       min(512, seq), min(512, d))
    return x2.reshape(batch, seq, d)
```

```python
import functools
import math

import jax
import jax.numpy as jnp
from jax import lax
from jax.experimental import pallas as pl
from jax.experimental.pallas import tpu as pltpu

SSD_HEAD_DIM = 64
SSD_N_GROUPS = 8
SSD_D_STATE = 128
SSD_CHUNK = 128
SSD_CONV_WIDTH = 4
MLA_QK_NOPE = 128
MLA_QK_ROPE = 64
MLA_V_DIM = 128
ROPE_THETA = 10000.0
NORM_EPS = 1e-6
LOG2_E = math.log2(math.e)

LANES = 128
SUBLANES = 8
CONV_HALO = 16
VMEM_LIMIT = 56 * 1024 * 1024
CAST_BLOCK_BYTES = 8 * 1024 * 1024
W_IN_CHUNK = 64
W_IN_COLS = 512
W_IN_COPY, W_IN_ROTATE, W_IN_ZERO = 0, 1, 2

F32 = jnp.float32
BF16 = jnp.bfloat16


def _cparams(sem):
    return pltpu.CompilerParams(dimension_semantics=sem, vmem_limit_bytes=VMEM_LIMIT)


def _silu(v):
    h = 0.5 * v
    return h + h * jnp.tanh(h)


def _split3(v):
    hi = v.astype(BF16)
    r1 = v - hi.astype(F32)
    mid = r1.astype(BF16)
    lo = (r1 - mid.astype(F32)).astype(BF16)
    return hi, mid, lo


def _dot(a, b):
    return jnp.dot(a, b, preferred_element_type=F32)


def _ada_kernel(c_ref, w_ref, b_ref, o_ref):
    @pl.when(pl.program_id(1) == 0)
    def _():
        o_ref[...] = jnp.broadcast_to(b_ref[...], o_ref.shape)

    a = _silu(c_ref[...]).astype(BF16)
    o_ref[...] += _dot(a, w_ref[...].astype(BF16))


def _ada(c_pad, ada_w, ada_b):
    depth, d, n = ada_w.shape
    tk = min(256, d)
    return pl.pallas_call(
        _ada_kernel,
        grid=(depth, d // tk),
        in_specs=[
            pl.BlockSpec((SUBLANES, tk), lambda l, k: (0, k)),
            pl.BlockSpec((None, tk, n), lambda l, k: (l, k, 0)),
            pl.BlockSpec((None, 1, n), lambda l, k: (l, 0, 0)),
        ],
        out_specs=pl.BlockSpec((None, SUBLANES, n), lambda l, k: (l, 0, 0)),
        out_shape=jax.ShapeDtypeStruct((depth, SUBLANES, n), F32),
        compiler_params=_cparams(("parallel", "arbitrary")),
        name="ada_mod",
    )(c_pad, ada_w, ada_b.reshape(depth, 1, n))


def _prenorm_kernel(x_ref, w_ref, shift_ref, scale_ref, o_ref):
    x = x_ref[...]
    y = x * lax.rsqrt(jnp.mean(x * x, axis=-1, keepdims=True) + NORM_EPS)
    o_ref[...] = (y * w_ref[...] * (1.0 + scale_ref[...]) + shift_ref[...]).astype(BF16)


def _prenorm(x2, w, mod4, seq):
    m, d = x2.shape
    tm = min(512, seq)
    per_b = seq // tm
    return pl.pallas_call(
        _prenorm_kernel,
        grid=(m // tm,),
        in_specs=[
            pl.BlockSpec((tm, d), lambda i: (i, 0)),
            pl.BlockSpec((1, d), lambda i: (0, 0)),
            pl.BlockSpec((None, None, 1, d), lambda i: (i // per_b, 0, 0, 0)),
            pl.BlockSpec((None, None, 1, d), lambda i: (i // per_b, 1, 0, 0)),
        ],
        out_specs=pl.BlockSpec((tm, d), lambda i: (i, 0)),
        out_shape=jax.ShapeDtypeStruct((m, d), BF16),
        compiler_params=_cparams(("parallel",)),
        name="prenorm",
    )(x2, w.reshape(1, d), mod4, mod4)


def _mm_kernel(a_ref, w_ref, o_ref):
    o_ref[...] = _dot(a_ref[...], w_ref[...]).astype(o_ref.dtype)


def _matmul(a, w3, l, out_dtype, tm, tn):
    m, k = a.shape
    n = w3.shape[2]
    return pl.pallas_call(
        _mm_kernel,
        grid=(m // tm, n // tn),
        in_specs=[
            pl.BlockSpec((tm, k), lambda i, j: (i, 0)),
            pl.BlockSpec((None, k, tn), lambda i, j: (l, 0, j)),
        ],
        out_specs=pl.BlockSpec((tm, tn), lambda i, j: (i, j)),
        out_shape=jax.ShapeDtypeStruct((m, n), out_dtype),
        compiler_params=_cparams(("parallel", "parallel")),
        name="in_proj",
    )(a, w3)


def _rope_block(a, cos, sin):
    return a * cos + pltpu.roll(a, LANES // 2, axis=1) * sin


def _norm_mm_kernel(a_ref, nw_ref, w_ref, *rest, rope, tn):
    if rope:
        cos_ref, sin_ref, o_ref, an_sc = rest
    else:
        o_ref, an_sc = rest

    @pl.when(pl.program_id(1) == 0)
    def _():
        a = a_ref[...].astype(F32)
        y = a * lax.rsqrt(jnp.mean(a * a, axis=-1, keepdims=True) + NORM_EPS)
        an_sc[...] = (y * nw_ref[...]).astype(BF16)

    acc = _dot(an_sc[...], w_ref[...])
    if rope:
        cos = cos_ref[...]
        sin = sin_ref[...]
        for hh in range(tn // (2 * LANES)):
            c0 = hh * 2 * LANES
            o_ref[:, c0:c0 + LANES] = acc[:, c0:c0 + LANES].astype(o_ref.dtype)
            o_ref[:, c0 + LANES:c0 + 2 * LANES] = _rope_block(
                acc[:, c0 + LANES:c0 + 2 * LANES], cos, sin).astype(o_ref.dtype)
    else:
        o_ref[...] = acc.astype(o_ref.dtype)


def _norm_matmul(proj, col_off, k, norm_w, w, l, tm, tn, cos=None, sin=None, name="norm_mm"):
    m = proj.shape[0]
    n = w.shape[2]
    rope = cos is not None
    assert col_off % k == 0
    in_specs = [
        pl.BlockSpec((tm, k), lambda i, j: (i, col_off // k)),
        pl.BlockSpec((1, k), lambda i, j: (0, 0)),
        pl.BlockSpec((None, k, tn), lambda i, j: (l, 0, j)),
    ]
    args = [proj, norm_w.reshape(1, k), w]
    if rope:
        in_specs += [pl.BlockSpec((tm, LANES), lambda i, j: (i, 0))] * 2
        args += [cos, sin]
    return pl.pallas_call(
        functools.partial(_norm_mm_kernel, rope=rope, tn=tn),
        grid=(m // tm, n // tn),
        in_specs=in_specs,
        out_specs=pl.BlockSpec((tm, tn), lambda i, j: (i, j)),
        out_shape=jax.ShapeDtypeStruct((m, n), BF16),
        scratch_shapes=[pltpu.VMEM((tm, k), BF16)],
        compiler_params=_cparams(("parallel", "arbitrary")),
        name=name,
    )(*args)


def _rope_tab_kernel(pos_ref, freq_ref, cos_ref, sin_ref):
    ang = pos_ref[...].astype(F32) * freq_ref[...]
    lane = lax.broadcasted_iota(jnp.int32, ang.shape, 1)
    keep = lane < MLA_QK_ROPE
    cos_ref[...] = jnp.where(keep, jnp.cos(ang), 0.0)
    sin_ref[...] = jnp.where(keep, jnp.sin(ang), 0.0)


def _rope_tables(positions):
    m = positions.size
    tm = min(1024, m)
    half = MLA_QK_ROPE // 2
    inv_freq = ROPE_THETA ** (-(jnp.arange(0, MLA_QK_ROPE, 2, dtype=F32) / MLA_QK_ROPE))
    freq = jnp.concatenate([inv_freq, inv_freq, jnp.zeros((LANES - 2 * half,), F32)]).reshape(1, LANES)
    return pl.pallas_call(
        _rope_tab_kernel,
        grid=(m // tm,),
        in_specs=[pl.BlockSpec((tm, 1), lambda i: (i, 0)),
                  pl.BlockSpec((1, LANES), lambda i: (0, 0))],
        out_specs=[pl.BlockSpec((tm, LANES), lambda i: (i, 0))] * 2,
        out_shape=[jax.ShapeDtypeStruct((m, LANES), F32)] * 2,
        compiler_params=_cparams(("parallel",)),
        name="rope_tables",
    )(positions.reshape(m, 1), freq)


def _krope_kernel(a_ref, cos_ref, sin_ref, o_ref):
    o_ref[...] = _rope_block(a_ref[...].astype(F32), cos_ref[...], sin_ref[...]).astype(BF16)


def _krope(proj, col_off, cos, sin):
    m = proj.shape[0]
    tm = min(1024, m)
    return pl.pallas_call(
        _krope_kernel,
        grid=(m // tm,),
        in_specs=[pl.BlockSpec((tm, LANES), lambda i: (i, col_off // LANES)),
                  pl.BlockSpec((tm, LANES), lambda i: (i, 0)),
                  pl.BlockSpec((tm, LANES), lambda i: (i, 0))],
        out_specs=pl.BlockSpec((tm, LANES), lambda i: (i, 0)),
        out_shape=jax.ShapeDtypeStruct((m, LANES), BF16),
        compiler_params=_cparams(("parallel",)),
        name="k_rope",
    )(proj, cos, sin)


def _ssd_pre_kernel(dt_ref, bias_ref, alog_ref, tri_ref, efull_ref,
                    dtb_ref, eacs_ref, toend_ref, acsrot_ref, acst_ref, decend_ref, *, tp, r):
    L = SSD_CHUNK
    dt = jax.nn.softplus(dt_ref[...].astype(F32) + bias_ref[...])
    dtb_ref[...] = dt.astype(BF16)
    a = dt * (-jnp.exp(alog_ref[...])) * LOG2_E
    tri = tri_ref[...]
    lasts = []
    for c in range(tp // L):
        rows = slice(c * L, (c + 1) * L)
        hi, mid, lo = _split3(a[rows])
        acs = _dot(tri, hi) + _dot(tri, mid) + _dot(tri, lo)
        last = acs[L - 1:L, :]
        lasts.append(last)
        eacs_ref[rows, :] = jnp.exp2(acs).astype(BF16)
        toend_ref[rows, :] = jnp.exp2(last - acs).astype(BF16)
        acst_ref[c] = acs.T
        for g in range(SSD_N_GROUPS):
            shift = (LANES - g * r) % LANES
            acsrot_ref[g, rows, :] = acs if shift == 0 else pltpu.roll(acs, shift, axis=1)
    n_c = tp // L
    pad = [jnp.zeros((SUBLANES - n_c % SUBLANES, LANES), F32)] if n_c % SUBLANES else []
    el = jnp.exp2(jnp.concatenate(lasts + pad, axis=0))
    hi, mid, lo = _split3(el)
    e = efull_ref[...]
    dec = _dot(hi, e) + _dot(mid, e) + _dot(lo, e)
    for c in range(n_c):
        decend_ref[c] = dec[c:c + 1, :]


def _ssd_pre(proj, col_off, dt_bias, a_log, d_inner, tp):
    m = proj.shape[0]
    nh = dt_bias.shape[0]
    assert nh == LANES and col_off % LANES == 0
    r = nh // SSD_N_GROUPS
    L = SSD_CHUNK
    tri = (jnp.arange(L)[:, None] >= jnp.arange(L)[None, :]).astype(BF16)
    efull = (jnp.arange(nh)[:, None] == (jnp.arange(d_inner)[None, :] // SSD_HEAD_DIM)).astype(BF16)
    nct = m // L
    return pl.pallas_call(
        functools.partial(_ssd_pre_kernel, tp=tp, r=r),
        grid=(m // tp,),
        in_specs=[
            pl.BlockSpec((tp, LANES), lambda i: (i, col_off // LANES)),
            pl.BlockSpec((1, LANES), lambda i: (0, 0)),
            pl.BlockSpec((1, LANES), lambda i: (0, 0)),
            pl.BlockSpec((L, L), lambda i: (0, 0)),
            pl.BlockSpec((nh, d_inner), lambda i: (0, 0)),
        ],
        out_specs=[
            pl.BlockSpec((tp, LANES), lambda i: (i, 0)),
            pl.BlockSpec((tp, LANES), lambda i: (i, 0)),
            pl.BlockSpec((tp, LANES), lambda i: (i, 0)),
            pl.BlockSpec((SSD_N_GROUPS, tp, LANES), lambda i: (0, i, 0)),
            pl.BlockSpec((tp // L, LANES, L), lambda i: (i, 0, 0)),
            pl.BlockSpec((tp // L, 1, d_inner), lambda i: (i, 0, 0)),
        ],
        out_shape=[
            jax.ShapeDtypeStruct((m, LANES), BF16),
            jax.ShapeDtypeStruct((m, LANES), BF16),
            jax.ShapeDtypeStruct((m, LANES), BF16),
            jax.ShapeDtypeStruct((SSD_N_GROUPS, m, LANES), F32),
            jax.ShapeDtypeStruct((nct, LANES, L), F32),
            jax.ShapeDtypeStruct((nct, 1, d_inner), F32),
        ],
        compiler_params=_cparams(("parallel",)),
        name="ssd_pre",
    )(proj, dt_bias.reshape(1, nh), a_log.reshape(1, nh), tri, efull)


def _ssd_kernel(xs_ref, bm_ref, cm_ref, z_ref, dtb_ref, eacs_ref, toend_ref, acs_ref, acst_ref,
                decend_ref, cwx_ref, cbx_ref, cwbc_ref, cbbc_ref, dskip_ref,
                nw_ref, e_ref, shift_ref, o_ref,
                xin_sc, bcin_sc, state_sc, *, t_blk, r):
    L = SSD_CHUNK
    K = SSD_CONV_WIDTH
    P = SSD_HEAD_DIM
    N = SSD_D_STATE
    gw = r * P
    halo = CONV_HALO

    @pl.when(pl.program_id(2) == 0)
    def _():
        state_sc[...] = jnp.zeros_like(state_sc)
        xin_sc[0:halo, :] = jnp.zeros((halo, gw), BF16)
        bcin_sc[0:halo, :] = jnp.zeros((halo, 2 * N), BF16)

    xin_sc[halo:halo + t_blk, :] = xs_ref[...]
    bcin_sc[halo:halo + t_blk, 0:N] = bm_ref[...]
    bcin_sc[halo:halo + t_blk, N:2 * N] = cm_ref[...]

    def conv(in_sc, rows_ext, w_ref, b_ref):
        xe = in_sc[rows_ext, :]
        sh = _dot(shift_ref[...], xe)
        acc = b_ref[...] + w_ref[K - 1:K, :] * xe[halo:halo + L, :].astype(F32)
        for k in range(K - 1):
            acc = acc + w_ref[k:k + 1, :] * sh[k * L:(k + 1) * L, :]
        return _silu(acc)

    li = lax.broadcasted_iota(jnp.int32, (L, L), 0)
    si = lax.broadcasted_iota(jnp.int32, (L, L), 1)
    causal = li >= si
    lane = lax.broadcasted_iota(jnp.int32, (L, 2 * P), 1)
    lo_half = lane < P

    def chunk(c, carry):
        rows = pl.ds(pl.multiple_of(c * L, L), L)
        rows_ext = pl.ds(pl.multiple_of(c * L, L), halo + L)
        xc = conv(xin_sc, rows_ext, cwx_ref, cbx_ref)
        bc_c = conv(bcin_sc, rows_ext, cwbc_ref, cbbc_ref).astype(BF16)
        b_c = bc_c[:, 0:N]
        c_c = bc_c[:, N:2 * N]
        e = e_ref[...]
        dt_e = _dot(dtb_ref[rows, :], e)
        eacs_e = _dot(eacs_ref[rows, :], e)
        toend_e = _dot(toend_ref[rows, :], e)
        xdt = xc * dt_e
        xdt_b = xdt.astype(BF16)
        xw_b = (xdt * toend_e).astype(BF16)
        cb = lax.dot_general(c_c, b_c, (((1,), (1,)), ((), ())), preferred_element_type=F32)
        acs = acs_ref[rows, :]
        acst = acst_ref[c]
        st = state_sc[...]
        y = _dot(c_c, st.astype(BF16)) * eacs_e
        pieces = []
        for jp in range(r // 2):
            xp = xdt_b[:, jp * 2 * P:(jp + 1) * 2 * P]
            acc = None
            for u in range(2):
                j = 2 * jp + u
                colb = jnp.broadcast_to(acs[:, j:j + 1], (L, L))
                rowb = jnp.broadcast_to(acst[j:j + 1, :], (L, L))
                dec = jnp.exp2(jnp.where(causal, colb - rowb, -jnp.inf))
                m_h = (cb * dec).astype(BF16)
                xm = jnp.where(lo_half if u == 0 else jnp.logical_not(lo_half), xp, jnp.zeros_like(xp))
                d = _dot(m_h, xm)
                acc = d if acc is None else acc + d
            pieces.append(acc)
        y = y + jnp.concatenate(pieces, axis=1)
        upd = lax.dot_general(b_c, xw_b, (((0,), (0,)), ((), ())), preferred_element_type=F32)
        state_sc[...] = st * decend_ref[c] + upd
        y = y + dskip_ref[...] * xc
        yg = y * _silu(z_ref[rows, :].astype(F32))
        ms = jnp.mean(yg * yg, axis=-1, keepdims=True)
        o_ref[rows, :] = (yg * lax.rsqrt(ms + NORM_EPS) * nw_ref[...]).astype(o_ref.dtype)
        return carry

    lax.fori_loop(0, t_blk // L, chunk, 0)
    xin_sc[0:halo, :] = xin_sc[t_blk:t_blk + halo, :]
    bcin_sc[0:halo, :] = bcin_sc[t_blk:t_blk + halo, :]


def _ssd(proj, offs, pre, conv_w, conv_b, d_skip, ssd_norm_w, batch, seq, t_blk):
    m = proj.shape[0]
    d_inner = ssd_norm_w.shape[0]
    nh = d_skip.shape[0]
    G = SSD_N_GROUPS
    N = SSD_D_STATE
    L = SSD_CHUNK
    r = nh // G
    gw = r * SSD_HEAD_DIM
    assert gw % LANES == 0 and r % 2 == 0 and r % SUBLANES == 0
    dtb, eacs, toend, acsrot, acst, decend = pre
    nt = seq // t_blk
    ncb = t_blk // L
    e_all = (jnp.arange(LANES)[None, :, None]
             == (jnp.arange(G)[:, None, None] * r + jnp.arange(gw)[None, None, :] // SSD_HEAD_DIM)).astype(BF16)
    dskip_e = jnp.repeat(d_skip.astype(F32), SSD_HEAD_DIM).reshape(1, d_inner)

    def bc_pairs(a):
        b_part = a[:, d_inner:d_inner + G * N].reshape(-1, G, 1, N)
        c_part = a[:, d_inner + G * N:].reshape(-1, G, 1, N)
        return jnp.concatenate([b_part, c_part], axis=2).reshape(-1, G * 2 * N)

    cb2 = conv_b.reshape(1, -1)
    cwx, cbx = conv_w[:, :d_inner], cb2[:, :d_inner]
    cwbc, cbbc = bc_pairs(conv_w), bc_pairs(cb2)
    kw = SSD_CONV_WIDTH
    tt = jnp.arange((kw - 1) * L)
    shift = (jnp.arange(CONV_HALO + L)[None, :]
             == (CONV_HALO + tt % L - (kw - 1) + tt // L)[:, None]).astype(BF16)
    for o in (offs["xs"], offs["z"]):
        assert o % gw == 0
    row = lambda b, g, t: b * nt + t
    in_specs = [
        pl.BlockSpec((t_blk, gw), lambda b, g, t: (row(b, g, t), offs["xs"] // gw + g)),
        pl.BlockSpec((t_blk, N), lambda b, g, t: (row(b, g, t), offs["bm"] // N + g)),
        pl.BlockSpec((t_blk, N), lambda b, g, t: (row(b, g, t), offs["cm"] // N + g)),
        pl.BlockSpec((t_blk, gw), lambda b, g, t: (row(b, g, t), offs["z"] // gw + g)),
        pl.BlockSpec((t_blk, LANES), lambda b, g, t: (row(b, g, t), 0)),
        pl.BlockSpec((t_blk, LANES), lambda b, g, t: (row(b, g, t), 0)),
        pl.BlockSpec((t_blk, LANES), lambda b, g, t: (row(b, g, t), 0)),
        pl.BlockSpec((None, t_blk, LANES), lambda b, g, t: (g, row(b, g, t), 0)),
        pl.BlockSpec((ncb, r, L), lambda b, g, t: (row(b, g, t), g, 0)),
        pl.BlockSpec((ncb, 1, gw), lambda b, g, t: (row(b, g, t), 0, g)),
        pl.BlockSpec((SSD_CONV_WIDTH, gw), lambda b, g, t: (0, g)),
        pl.BlockSpec((1, gw), lambda b, g, t: (0, g)),
        pl.BlockSpec((SSD_CONV_WIDTH, 2 * N), lambda b, g, t: (0, g)),
        pl.BlockSpec((1, 2 * N), lambda b, g, t: (0, g)),
        pl.BlockSpec((1, gw), lambda b, g, t: (0, g)),
        pl.BlockSpec((1, gw), lambda b, g, t: (0, g)),
        pl.BlockSpec((None, LANES, gw), lambda b, g, t: (g, 0, 0)),
        pl.BlockSpec(((kw - 1) * L, CONV_HALO + L), lambda b, g, t: (0, 0)),
    ]
    return pl.pallas_call(
        functools.partial(_ssd_kernel, t_blk=t_blk, r=r),
        grid=(batch, G, nt),
        in_specs=in_specs,
        out_specs=pl.BlockSpec((t_blk, gw), lambda b, g, t: (row(b, g, t), g)),
        out_shape=jax.ShapeDtypeStruct((m, d_inner), BF16),
        scratch_shapes=[
            pltpu.VMEM((t_blk + CONV_HALO, gw), BF16),
            pltpu.VMEM((t_blk + CONV_HALO, 2 * N), BF16),
            pltpu.VMEM((N, gw), F32),
        ],
        compiler_params=_cparams(("parallel", "parallel", "arbitrary")),
        name="ssd_scan",
    )(proj, proj, proj, proj, dtb, eacs, toend, acsrot, acst, decend,
      cwx, cbx, cwbc, cbbc, dskip_e, ssd_norm_w.reshape(1, d_inner), e_all, shift)


def _flash_kernel(q_ref, kn_ref, kr_ref, v_ref, g_ref, o_ref, vt_sc, sa_sc, sb_sc, xa_sc, xb_sc,
                  qt0_sc, qt1_sc, m0_sc, m1_sc, l0_sc, l1_sc, acc0_sc, acc1_sc, *, tq, tk, seq):
    n_sub = tq // tk
    nq = seq // tq
    tqs = min(2 * LANES, tq)
    nch = tq // tqs
    assert n_sub == 2 and nq % 2 == 0 and nq >= 4
    buf_a, buf_b = (sa_sc, xa_sc), (sb_sc, xb_sc)
    state = ((qt0_sc, m0_sc, l0_sc, acc0_sc), (qt1_sc, m1_sc, l1_sc, acc1_sc))

    for c in range(seq // tk):
        vt_sc[c] = v_ref[c * tk:(c + 1) * tk, :].astype(F32).T.astype(BF16)

    def step(nxt_blk, cur_blk):
        nxt, cur = [], []
        if nxt_blk is not None:
            kb_next, bufs_next, diag_next, qt = nxt_blk
            rows = pl.ds(pl.multiple_of(kb_next * tk, tk), tk)
            k = jnp.concatenate([kn_ref[rows, :], kr_ref[rows, :]], axis=1)
            k0 = 0 if diag_next is None else diag_next * tk
            nxt = list(range(k0, tq, tqs))
        if cur_blk is not None:
            kb, bufs, diag, (m_sc, l_sc, acc_sc) = cur_blk
            vt = vt_sc[kb]
            cur = list(range(0 if diag is None else diag * tk, tq, tqs))
        for i in range(max(len(nxt), len(cur))):
            if i < len(nxt):
                c0 = nxt[i]
                st = _dot(k, qt[c0 // tqs])
                if diag_next is not None and c0 < k0 + tk:
                    ki = lax.broadcasted_iota(jnp.int32, (tk, tqs), 0) + k0
                    qj = lax.broadcasted_iota(jnp.int32, (tk, tqs), 1) + c0
                    st = jnp.where(ki <= qj, st, -jnp.inf)
                bufs_next[0][c0 // tqs] = st
                bufs_next[1][c0 // tqs] = jnp.max(st, axis=0, keepdims=True)
            if i < len(cur):
                ci = cur[i] // tqs
                st = bufs[0][ci]
                m_prev = m_sc[ci]
                m_new = jnp.maximum(m_prev, bufs[1][ci])
                alpha = jnp.exp2(m_prev - m_new)
                p = jnp.exp2(st - m_new)
                l_sc[ci] = alpha * l_sc[ci] + jnp.sum(p, axis=0, keepdims=True)
                acc_sc[ci] = acc_sc[ci] * alpha + _dot(vt, p.astype(BF16))
                m_sc[ci] = m_new

    def init(qi, par):
        qt, m_sc, l_sc, acc_sc = state[par]
        for ci in range(nch):
            qsub = pl.ds(pl.multiple_of(qi * tq + ci * tqs, tqs), tqs)
            qt[ci] = q_ref[qsub, :].astype(F32).T.astype(BF16)
        m_sc[...] = jnp.full(m_sc.shape, -jnp.inf, F32)
        l_sc[...] = jnp.zeros(l_sc.shape, F32)
        acc_sc[...] = jnp.zeros(acc_sc.shape, F32)

    def finalize(qi, par):
        _, _, l_sc, acc_sc = state[par]
        for ci in range(nch):
            qsub = pl.ds(pl.multiple_of(qi * tq + ci * tqs, tqs), tqs)
            o = (acc_sc[ci] / l_sc[ci]).T
            o_ref[qsub, :] = (o * _silu(g_ref[qsub, :].astype(F32))).astype(o_ref.dtype)

    def body(qi, par):
        qt, stats = state[par][0], state[par][1:]
        d0 = qi * n_sub

        def pair(j):
            step((2 * j + 1, buf_b, None, qt), (2 * j, buf_a, None, stats))
            step((2 * j + 2, buf_a, None, qt), (2 * j + 1, buf_b, None, stats))

        def quad(i, c):
            pair(2 * i)
            pair(2 * i + 1)
            return c

        n_pairs = qi - 1
        lax.fori_loop(0, n_pairs // 2, quad, 0)

        @pl.when(n_pairs % 2 == 1)
        def _():
            pair(n_pairs - 1)

        step((d0 - 1, buf_b, None, qt), (d0 - 2, buf_a, None, stats))
        step((d0, buf_a, 0, qt), (d0 - 1, buf_b, None, stats))
        step((d0 + 1, buf_b, 1, qt), (d0, buf_a, 0, stats))

    def handover(qi, par):
        init(qi + 1, 1 - par)
        step((0, buf_a, None, state[1 - par][0]), (qi * n_sub + 1, buf_b, 1, state[par][1:]))
        finalize(qi, par)

    init(0, 0)
    step((0, buf_a, 0, state[0][0]), None)
    step((1, buf_b, 1, state[0][0]), (0, buf_a, 0, state[0][1:]))
    handover(0, 0)

    def two_blocks(i, carry):
        qi = 2 * i + 1
        body(qi, 1)
        handover(qi, 1)
        body(qi + 1, 0)
        handover(qi + 1, 0)
        return carry

    lax.fori_loop(0, nq // 2 - 1, two_blocks, 0)
    body(nq - 1, 1)
    step(None, ((nq - 1) * n_sub + 1, buf_b, 1, state[1][1:]))
    finalize(nq - 1, 1)


def _flash(q, kv, kr, proj, gate_off, batch, seq, n_heads, tq, tk):
    m = q.shape[0]
    dq = 2 * LANES
    tqs = min(2 * LANES, tq)
    nch = tq // tqs
    assert gate_off % LANES == 0
    return pl.pallas_call(
        functools.partial(_flash_kernel, tq=tq, tk=tk, seq=seq),
        grid=(batch, n_heads),
        in_specs=[
            pl.BlockSpec((seq, dq), lambda b, h: (b, h)),
            pl.BlockSpec((seq, LANES), lambda b, h: (b, 2 * h)),
            pl.BlockSpec((seq, LANES), lambda b, h: (b, 0)),
            pl.BlockSpec((seq, LANES), lambda b, h: (b, 2 * h + 1)),
            pl.BlockSpec((seq, LANES), lambda b, h: (b, gate_off // LANES + h)),
        ],
        out_specs=pl.BlockSpec((seq, LANES), lambda b, h: (b, h)),
        out_shape=jax.ShapeDtypeStruct((m, n_heads * MLA_V_DIM), BF16),
        scratch_shapes=[pltpu.VMEM((seq // tk, MLA_V_DIM, tk), BF16),
                        pltpu.VMEM((nch, tk, tqs), F32), pltpu.VMEM((nch, tk, tqs), F32),
                        pltpu.VMEM((nch, 1, tqs), F32), pltpu.VMEM((nch, 1, tqs), F32)]
                       + [pltpu.VMEM((nch, dq, tqs), BF16)] * 2 + [pltpu.VMEM((nch, 1, tqs), F32)] * 4
                       + [pltpu.VMEM((nch, MLA_V_DIM, tqs), F32)] * 2,
        compiler_params=_cparams(("parallel", "parallel")),
        name="mla_flash",
    )(q, kv, kr, kv, proj)


def _ssd_out_kernel(a_ref, w_ref, g_ref, o_ref):
    y = _dot(a_ref[...], w_ref[...])
    o_ref[...] = (jax.nn.sigmoid(g_ref[...].astype(F32)) * y).astype(o_ref.dtype)


def _ssd_out(ys, w_ssd, l, proj, gate_off, tm, tn):
    m, k1 = ys.shape
    d = w_ssd.shape[2]
    assert gate_off % tn == 0
    return pl.pallas_call(
        _ssd_out_kernel,
        grid=(m // tm, d // tn),
        in_specs=[
            pl.BlockSpec((tm, k1), lambda i, j: (i, 0)),
            pl.BlockSpec((None, k1, tn), lambda i, j: (l, 0, j)),
            pl.BlockSpec((tm, tn), lambda i, j: (i, gate_off // tn + j)),
        ],
        out_specs=pl.BlockSpec((tm, tn), lambda i, j: (i, j)),
        out_shape=jax.ShapeDtypeStruct((m, d), BF16),
        compiler_params=_cparams(("parallel", "parallel")),
        name="ssd_out",
    )(ys, w_ssd, proj)


def _mla_out_kernel(a_ref, w_ref, g_ref, p_ref, o_ref):
    y = _dot(a_ref[...], w_ref[...])
    o_ref[...] = (p_ref[...].astype(F32) + jax.nn.sigmoid(g_ref[...].astype(F32)) * y).astype(o_ref.dtype)


def _mla_out(og, w_mla, l, proj, gate_off, part, tm, tn):
    m, k2 = og.shape
    d = w_mla.shape[2]
    assert gate_off % tn == 0
    return pl.pallas_call(
        _mla_out_kernel,
        grid=(m // tm, d // tn),
        in_specs=[
            pl.BlockSpec((tm, k2), lambda i, j: (i, 0)),
            pl.BlockSpec((None, k2, tn), lambda i, j: (l, 0, j)),
            pl.BlockSpec((tm, tn), lambda i, j: (i, gate_off // tn + j)),
            pl.BlockSpec((tm, tn), lambda i, j: (i, j)),
        ],
        out_specs=pl.BlockSpec((tm, tn), lambda i, j: (i, j)),
        out_shape=jax.ShapeDtypeStruct((m, d), BF16),
        compiler_params=_cparams(("parallel", "parallel")),
        name="mla_out",
    )(og, w_mla, proj, part)


def _out_kernel(a_ref, w_ref, x_ref, pw_ref, gate_ref, o_ref, *, nk):
    k = pl.program_id(1)

    @pl.when(k == 0)
    def _():
        o_ref[...] = jnp.zeros_like(o_ref)

    o_ref[...] += _dot(a_ref[...], w_ref[...])

    @pl.when(k == nk - 1)
    def _():
        y = o_ref[...]
        yn = y * lax.rsqrt(jnp.mean(y * y, axis=-1, keepdims=True) + NORM_EPS) * pw_ref[...]
        o_ref[...] = x_ref[...] + gate_ref[...] * yn


def _out_proj(merged, w_out, l, x2, post_w, mod4, seq, tm, tk):
    m, d = x2.shape
    nk = d // tk
    per_b = seq // tm
    return pl.pallas_call(
        functools.partial(_out_kernel, nk=nk),
        grid=(m // tm, nk),
        in_specs=[
            pl.BlockSpec((tm, tk), lambda i, k: (i, k)),
            pl.BlockSpec((None, tk, d), lambda i, k: (l, k, 0)),
            pl.BlockSpec((tm, d), lambda i, k: (i, 0)),
            pl.BlockSpec((1, d), lambda i, k: (0, 0)),
            pl.BlockSpec((None, None, 1, d), lambda i, k: (i // per_b, 2, 0, 0)),
        ],
        out_specs=pl.BlockSpec((tm, d), lambda i, k: (i, 0)),
        out_shape=jax.ShapeDtypeStruct((m, d), F32),
        compiler_params=_cparams(("parallel", "arbitrary")),
        name="out_proj",
    )(merged, w_out, x2, post_w.reshape(1, d), mod4)


def _proj_layout(d, d_inner, nh, n_mla_heads, q_lora, kv_lora):
    gn = SSD_N_GROUPS * SSD_D_STATE
    widths = [("z", d_inner), ("xs", d_inner), ("bm", gn), ("cm", gn),
              ("gate", n_mla_heads * MLA_V_DIM), ("merge", 2 * d), ("cq", q_lora), ("ckv", kv_lora),
              ("kr", 2 * MLA_QK_ROPE), ("dt", nh)]
    offs, o = {}, 0
    for name, w in widths:
        offs[name] = o
        o += w
    return offs, o


def _cast_kernel(w_ref, o_ref):
    o_ref[...] = w_ref[...].astype(o_ref.dtype)


def _cast_bf16(w3):
    depth, k, n = w3.shape
    tr = max(2 * SUBLANES, min(k, CAST_BLOCK_BYTES // (n * 4)))
    assert k % tr == 0
    return pl.pallas_call(
        _cast_kernel,
        grid=(depth, k // tr),
        in_specs=[pl.BlockSpec((None, tr, n), lambda l, r: (l, r, 0))],
        out_specs=pl.BlockSpec((None, tr, n), lambda l, r: (l, r, 0)),
        out_shape=jax.ShapeDtypeStruct((depth, k, n), BF16),
        compiler_params=_cparams(("parallel", "parallel")),
        name="cast_bf16",
    )(w3)


def _w_in_kernel(src_ref, mode_ref, *refs, n_chunks):
    o_ref = refs[n_chunks]
    j = pl.program_id(1)
    half = MLA_QK_ROPE // 2
    parts = []
    for c in range(n_chunks):
        blk = refs[c][...]
        mode = mode_ref[j * n_chunks + c]
        rot = jnp.concatenate([-blk[half:], blk[:half]], axis=0)
        v = jnp.where(mode == W_IN_ROTATE, rot, blk)
        parts.append(jnp.where(mode == W_IN_ZERO, 0.0, v))
    o_ref[...] = jnp.concatenate(parts, axis=0).T.astype(BF16)


def _prep_w_in(w_in3, offs, n_tot, n_pad, d, d_inner, nh, n_mla_heads, q_lora, kv_lora):
    depth, k, n_src = w_in3.shape
    gn = SSD_N_GROUPS * SSD_D_STATE
    conv_dim = d_inner + 2 * gn
    s_xbc = d_inner
    s_dt = s_xbc + conv_dim
    s_cq = s_dt + nh
    s_ckv = s_cq + q_lora
    s_kr = s_ckv + kv_lora
    s_gate = s_kr + MLA_QK_ROPE
    s_merge = s_gate + n_mla_heads * MLA_V_DIM
    assert s_merge + 2 * d == n_src and MLA_QK_ROPE == W_IN_CHUNK
    segments = (
        (offs["z"], 0, d_inner + conv_dim, W_IN_COPY),
        (offs["gate"], s_gate, n_mla_heads * MLA_V_DIM + 2 * d, W_IN_COPY),
        (offs["cq"], s_cq, q_lora + kv_lora, W_IN_COPY),
        (offs["kr"], s_kr, MLA_QK_ROPE, W_IN_COPY),
        (offs["kr"] + MLA_QK_ROPE, s_kr, MLA_QK_ROPE, W_IN_ROTATE),
        (offs["dt"], s_dt, nh, W_IN_COPY),
    )
    n_dst = n_pad // W_IN_CHUNK
    src_tab = [0] * n_dst
    mode_tab = [W_IN_ZERO] * n_dst
    for dst, src, width, mode in segments:
        assert dst % W_IN_CHUNK == 0 and src % W_IN_CHUNK == 0 and width % W_IN_CHUNK == 0
        for t in range(width // W_IN_CHUNK):
            src_tab[dst // W_IN_CHUNK + t] = src // W_IN_CHUNK + t
            mode_tab[dst // W_IN_CHUNK + t] = mode
    n_chunks = W_IN_COLS // W_IN_CHUNK
    w_t = jnp.swapaxes(w_in3, 1, 2)

    def chunk_spec(c):
        return pl.BlockSpec((None, W_IN_CHUNK, k), lambda l, j, src, mode: (l, src[j * n_chunks + c], 0))

    return pl.pallas_call(
        functools.partial(_w_in_kernel, n_chunks=n_chunks),
        grid_spec=pltpu.PrefetchScalarGridSpec(
            num_scalar_prefetch=2,
            grid=(depth, n_pad // W_IN_COLS),
            in_specs=[chunk_spec(c) for c in range(n_chunks)],
            out_specs=pl.BlockSpec((None, k, W_IN_COLS), lambda l, j, src, mode: (l, 0, j)),
        ),
        out_shape=jax.ShapeDtypeStruct((depth, k, n_pad), BF16),
        compiler_params=_cparams(("parallel", "parallel")),
        name="w_in_prep",
    )(jnp.asarray(src_tab, jnp.int32), jnp.asarray(mode_tab, jnp.int32), *([w_t] * n_chunks))


def _w_q_kernel(w_ref, o_ref, *, n_heads):
    qk = MLA_QK_NOPE + MLA_QK_ROPE
    half = MLA_QK_ROPE // 2
    scale = qk ** -0.5 * LOG2_E

    def load(src, width):
        lead = src % LANES
        v = w_ref[:, src - lead:src + width]
        return v[:, lead:] if lead else v

    for h in range(n_heads):
        nope = load(h * qk, MLA_QK_NOPE)
        rope = load(h * qk + MLA_QK_NOPE, MLA_QK_ROPE)
        blk = jnp.concatenate([nope, rope, -rope[:, half:], rope[:, :half]], axis=1)
        o_ref[:, h * 2 * LANES:(h + 1) * 2 * LANES] = (blk * scale).astype(BF16)


def _prep_w_q(w_q_up3, n_mla_heads):
    depth, k, n = w_q_up3.shape
    tr = min(k, 256)
    return pl.pallas_call(
        functools.partial(_w_q_kernel, n_heads=n_mla_heads),
        grid=(depth, k // tr),
        in_specs=[pl.BlockSpec((None, tr, n), lambda l, r: (l, r, 0))],
        out_specs=pl.BlockSpec((None, tr, n_mla_heads * 2 * LANES), lambda l, r: (l, r, 0)),
        out_shape=jax.ShapeDtypeStruct((depth, k, n_mla_heads * 2 * LANES), BF16),
        compiler_params=_cparams(("parallel", "parallel")),
        name="w_q_prep",
    )(w_q_up3)


def _round_up(v, mult):
    return (v + mult - 1) // mult * mult


def _tile_plan(m, seq, d, n_mla):
    qkv_n = n_mla * 2 * LANES
    tq = min(1024, seq)
    return dict(
        tm=min(1024, m),
        proj_tn=1024,
        q_tn=min(2048, qkv_n), kv_tn=min(4096, qkv_n),
        ssd_pre_rows=min(1024, seq), ssd_rows=min(1024, seq),
        flash_tq=tq, flash_tk=min(512, tq),
        ssd_out_tn=min(512, d), mla_out_tn=min(1024, d),
        out_tm=min(512, seq), out_tk=min(512, d),
    )


def kernel(x, c, positions, ada_w, ada_b, pre_norm_w, post_norm_w, w_in, conv_w, conv_b, dt_bias,
           a_log, d_skip, ssd_norm_w, q_norm_w, w_q_up, kv_norm_w, w_kv_up, w_ssd_proj, w_mla_proj,
           w_out):
    batch, seq, d = x.shape
    depth = ada_w.shape[0]
    m = batch * seq
    d_inner = ssd_norm_w.shape[1]
    nh = dt_bias.shape[1]
    q_lora = q_norm_w.shape[1]
    kv_lora = kv_norm_w.shape[1]
    n_mla = w_mla_proj.shape[1] // MLA_V_DIM
    offs, n_tot = _proj_layout(d, d_inner, nh, n_mla, q_lora, kv_lora)
    t = _tile_plan(m, seq, d, n_mla)
    n_pad = _round_up(n_tot, t["proj_tn"])

    c_pad = jnp.zeros((SUBLANES, d), F32).at[:batch].set(c)
    mod = _ada(c_pad, ada_w, ada_b)
    cos, sin = _rope_tables(positions)

    w_in_b = _prep_w_in(w_in, offs, n_tot, n_pad, d, d_inner, nh, n_mla, q_lora, kv_lora)
    w_q_b = _prep_w_q(w_q_up, n_mla)
    w_kv_b = _cast_bf16(w_kv_up)
    w_ssd_b = _cast_bf16(w_ssd_proj)
    w_mla_b = _cast_bf16(w_mla_proj)
    w_out_b = _cast_bf16(w_out)

    x2 = x.reshape(m, d)
    for l in range(depth):
        mod4 = mod[l, :batch].reshape(batch, 3, 1, d)
        h = _prenorm(x2, pre_norm_w[l], mod4, seq)
        proj = _matmul(h, w_in_b, l, BF16, t["tm"], t["proj_tn"])

        pre = _ssd_pre(proj, offs["dt"], dt_bias[l], a_log[l], d_inner, tp=t["ssd_pre_rows"])
        ys = _ssd(proj, offs, pre, conv_w[l], conv_b[l], d_skip[l], ssd_norm_w[l], batch, seq,
                  t_blk=t["ssd_rows"])

        q = _norm_matmul(proj, offs["cq"], q_lora, q_norm_w[l], w_q_b, l,
                         t["tm"], t["q_tn"], cos, sin, name="q_up")
        kv = _norm_matmul(proj, offs["ckv"], kv_lora, kv_norm_w[l], w_kv_b, l,
                          t["tm"], t["kv_tn"], name="kv_up")
        kr = _krope(proj, offs["kr"], cos, sin)
        og = _flash(q, kv, kr, proj, offs["gate"], batch, seq, n_mla, t["flash_tq"], t["flash_tk"])

        part = _ssd_out(ys, w_ssd_b, l, proj, offs["merge"], t["tm"], t["ssd_out_tn"])
        merged = _mla_out(og, w_mla_b, l, proj, offs["merge"] + d, part, t["tm"], t["mla_out_tn"])
        x2 = _out_proj(merged, w_out_b, l, x2, post_norm_w[l], mod4, seq, t["out_tm"], t["out_tk"])
    return x2.reshape(batch, seq, d)
```

```python
import functools
import math

import jax
import jax.numpy as jnp
from jax import lax
from jax.experimental import pallas as pl
from jax.experimental.pallas import tpu as pltpu

SSD_HEAD_DIM = 64
SSD_N_GROUPS = 8
SSD_D_STATE = 128
SSD_CHUNK = 128
SSD_CONV_WIDTH = 4
MLA_QK_NOPE = 128
MLA_QK_ROPE = 64
MLA_V_DIM = 128
ROPE_THETA = 10000.0
NORM_EPS = 1e-6
LOG2_E = math.log2(math.e)

LANES = 128
SUBLANES = 8
CONV_HALO = 16
VMEM_LIMIT = 56 * 1024 * 1024
CAST_BLOCK_BYTES = 8 * 1024 * 1024
W_IN_CHUNK = 64
W_IN_COLS = 512
W_IN_COPY, W_IN_ROTATE, W_IN_ZERO = 0, 1, 2

F32 = jnp.float32
BF16 = jnp.bfloat16


def _cparams(sem):
    return pltpu.CompilerParams(dimension_semantics=sem, vmem_limit_bytes=VMEM_LIMIT)


def _silu(v):
    h = 0.5 * v
    return h + h * jnp.tanh(h)


def _split3(v):
    hi = v.astype(BF16)
    r1 = v - hi.astype(F32)
    mid = r1.astype(BF16)
    lo = (r1 - mid.astype(F32)).astype(BF16)
    return hi, mid, lo


def _dot(a, b):
    return jnp.dot(a, b, preferred_element_type=F32)


def _ada_kernel(c_ref, w_ref, b_ref, o_ref):
    @pl.when(pl.program_id(1) == 0)
    def _():
        o_ref[...] = jnp.broadcast_to(b_ref[...], o_ref.shape)

    a = _silu(c_ref[...]).astype(BF16)
    o_ref[...] += _dot(a, w_ref[...].astype(BF16))


def _ada(c_pad, ada_w, ada_b):
    depth, d, n = ada_w.shape
    tk = min(256, d)
    return pl.pallas_call(
        _ada_kernel,
        grid=(depth, d // tk),
        in_specs=[
            pl.BlockSpec((SUBLANES, tk), lambda l, k: (0, k)),
            pl.BlockSpec((None, tk, n), lambda l, k: (l, k, 0)),
            pl.BlockSpec((None, 1, n), lambda l, k: (l, 0, 0)),
        ],
        out_specs=pl.BlockSpec((None, SUBLANES, n), lambda l, k: (l, 0, 0)),
        out_shape=jax.ShapeDtypeStruct((depth, SUBLANES, n), F32),
        compiler_params=_cparams(("parallel", "arbitrary")),
        name="ada_mod",
    )(c_pad, ada_w, ada_b.reshape(depth, 1, n))


def _prenorm_kernel(x_ref, w_ref, shift_ref, scale_ref, o_ref):
    x = x_ref[...]
    y = x * lax.rsqrt(jnp.mean(x * x, axis=-1, keepdims=True) + NORM_EPS)
    o_ref[...] = (y * w_ref[...] * (1.0 + scale_ref[...]) + shift_ref[...]).astype(BF16)


def _prenorm(x2, w, mod4, seq):
    m, d = x2.shape
    tm = min(512, seq)
    per_b = seq // tm
    return pl.pallas_call(
        _prenorm_kernel,
        grid=(m // tm,),
        in_specs=[
            pl.BlockSpec((tm, d), lambda i: (i, 0)),
            pl.BlockSpec((1, d), lambda i: (0, 0)),
            pl.BlockSpec((None, None, 1, d), lambda i: (i // per_b, 0, 0, 0)),
            pl.BlockSpec((None, None, 1, d), lambda i: (i // per_b, 1, 0, 0)),
        ],
        out_specs=pl.BlockSpec((tm, d), lambda i: (i, 0)),
        out_shape=jax.ShapeDtypeStruct((m, d), BF16),
        compiler_params=_cparams(("parallel",)),
        name="prenorm",
    )(x2, w.reshape(1, d), mod4, mod4)


def _mm_kernel(a_ref, w_ref, *rest, n_side):
    o_ref = rest[n_side]
    o_ref[...] = _dot(a_ref[...], w_ref[...]).astype(o_ref.dtype)
    for s in range(n_side):
        rest[n_side + 1 + s][...] = rest[s][...].astype(BF16)


def _rides_along(w3, steps):
    rows = w3.shape[0] * w3.shape[1]
    return rows % steps == 0 and (rows // steps) % (2 * SUBLANES) == 0


def _matmul(a, w3, l, out_dtype, tm, tn, side_casts=()):
    m, k = a.shape
    n = w3.shape[2]
    gi, gj = m // tm, n // tn
    side_specs, side_shapes = [], []
    for w in side_casts:
        rows, cols = w.shape
        spec = pl.BlockSpec((rows // (gi * gj), cols), lambda i, j: (i * gj + j, 0))
        side_specs.append(spec)
        side_shapes.append(jax.ShapeDtypeStruct((rows, cols), BF16))
    res = pl.pallas_call(
        functools.partial(_mm_kernel, n_side=len(side_casts)),
        grid=(gi, gj),
        in_specs=[
            pl.BlockSpec((tm, k), lambda i, j: (i, 0)),
            pl.BlockSpec((None, k, tn), lambda i, j: (l, 0, j)),
        ] + side_specs,
        out_specs=[pl.BlockSpec((tm, tn), lambda i, j: (i, j))] + side_specs,
        out_shape=[jax.ShapeDtypeStruct((m, n), out_dtype)] + side_shapes,
        compiler_params=_cparams(("parallel", "parallel")),
        name="in_proj",
    )(a, w3, *side_casts)
    return res[0], res[1:]


def _rope_block(a, cos, sin):
    return a * cos + pltpu.roll(a, LANES // 2, axis=1) * sin


def _norm_mm_kernel(a_ref, nw_ref, w_ref, *rest, rope, tn):
    if rope:
        cos_ref, sin_ref, o_ref, an_sc = rest
    else:
        o_ref, an_sc = rest

    @pl.when(pl.program_id(1) == 0)
    def _():
        a = a_ref[...].astype(F32)
        y = a * lax.rsqrt(jnp.mean(a * a, axis=-1, keepdims=True) + NORM_EPS)
        an_sc[...] = (y * nw_ref[...]).astype(BF16)

    acc = _dot(an_sc[...], w_ref[...])
    if rope:
        cos = cos_ref[...]
        sin = sin_ref[...]
        for hh in range(tn // (2 * LANES)):
            c0 = hh * 2 * LANES
            o_ref[:, c0:c0 + LANES] = acc[:, c0:c0 + LANES].astype(o_ref.dtype)
            o_ref[:, c0 + LANES:c0 + 2 * LANES] = _rope_block(
                acc[:, c0 + LANES:c0 + 2 * LANES], cos, sin).astype(o_ref.dtype)
    else:
        o_ref[...] = acc.astype(o_ref.dtype)


def _norm_matmul(proj, col_off, k, norm_w, w, l, tm, tn, cos=None, sin=None, name="norm_mm"):
    m = proj.shape[0]
    n = w.shape[2]
    rope = cos is not None
    assert col_off % k == 0
    in_specs = [
        pl.BlockSpec((tm, k), lambda i, j: (i, col_off // k)),
        pl.BlockSpec((1, k), lambda i, j: (0, 0)),
        pl.BlockSpec((None, k, tn), lambda i, j: (l, 0, j)),
    ]
    args = [proj, norm_w.reshape(1, k), w]
    if rope:
        in_specs += [pl.BlockSpec((tm, LANES), lambda i, j: (i, 0))] * 2
        args += [cos, sin]
    return pl.pallas_call(
        functools.partial(_norm_mm_kernel, rope=rope, tn=tn),
        grid=(m // tm, n // tn),
        in_specs=in_specs,
        out_specs=pl.BlockSpec((tm, tn), lambda i, j: (i, j)),
        out_shape=jax.ShapeDtypeStruct((m, n), BF16),
        scratch_shapes=[pltpu.VMEM((tm, k), BF16)],
        compiler_params=_cparams(("parallel", "arbitrary")),
        name=name,
    )(*args)


def _rope_tab_kernel(pos_ref, freq_ref, cos_ref, sin_ref):
    ang = pos_ref[...].astype(F32) * freq_ref[...]
    lane = lax.broadcasted_iota(jnp.int32, ang.shape, 1)
    keep = lane < MLA_QK_ROPE
    cos_ref[...] = jnp.where(keep, jnp.cos(ang), 0.0)
    sin_ref[...] = jnp.where(keep, jnp.sin(ang), 0.0)


def _rope_tables(positions):
    m = positions.size
    tm = min(1024, m)
    half = MLA_QK_ROPE // 2
    inv_freq = ROPE_THETA ** (-(jnp.arange(0, MLA_QK_ROPE, 2, dtype=F32) / MLA_QK_ROPE))
    freq = jnp.concatenate([inv_freq, inv_freq, jnp.zeros((LANES - 2 * half,), F32)]).reshape(1, LANES)
    return pl.pallas_call(
        _rope_tab_kernel,
        grid=(m // tm,),
        in_specs=[pl.BlockSpec((tm, 1), lambda i: (i, 0)),
                  pl.BlockSpec((1, LANES), lambda i: (0, 0))],
        out_specs=[pl.BlockSpec((tm, LANES), lambda i: (i, 0))] * 2,
        out_shape=[jax.ShapeDtypeStruct((m, LANES), F32)] * 2,
        compiler_params=_cparams(("parallel",)),
        name="rope_tables",
    )(positions.reshape(m, 1), freq)


def _krope_kernel(a_ref, cos_ref, sin_ref, o_ref):
    o_ref[...] = _rope_block(a_ref[...].astype(F32), cos_ref[...], sin_ref[...]).astype(BF16)


def _krope(proj, col_off, cos, sin):
    m = proj.shape[0]
    tm = min(1024, m)
    return pl.pallas_call(
        _krope_kernel,
        grid=(m // tm,),
        in_specs=[pl.BlockSpec((tm, LANES), lambda i: (i, col_off // LANES)),
                  pl.BlockSpec((tm, LANES), lambda i: (i, 0)),
                  pl.BlockSpec((tm, LANES), lambda i: (i, 0))],
        out_specs=pl.BlockSpec((tm, LANES), lambda i: (i, 0)),
        out_shape=jax.ShapeDtypeStruct((m, LANES), BF16),
        compiler_params=_cparams(("parallel",)),
        name="k_rope",
    )(proj, cos, sin)


def _ssd_pre_kernel(dt_ref, bias_ref, alog_ref, tri_ref, efull_ref,
                    dtb_ref, eacs_ref, toend_ref, acsrot_ref, acst_ref, decend_ref, *, tp, r):
    L = SSD_CHUNK
    dt = jax.nn.softplus(dt_ref[...].astype(F32) + bias_ref[...])
    dtb_ref[...] = dt.astype(BF16)
    a = dt * (-jnp.exp(alog_ref[...])) * LOG2_E
    tri = tri_ref[...]
    lasts = []
    for c in range(tp // L):
        rows = slice(c * L, (c + 1) * L)
        hi, mid, lo = _split3(a[rows])
        acs = _dot(tri, hi) + _dot(tri, mid) + _dot(tri, lo)
        last = acs[L - 1:L, :]
        lasts.append(last)
        eacs_ref[rows, :] = jnp.exp2(acs).astype(BF16)
        toend_ref[rows, :] = jnp.exp2(last - acs).astype(BF16)
        acst_ref[c] = acs.T
        for g in range(SSD_N_GROUPS):
            shift = (LANES - g * r) % LANES
            acsrot_ref[g, rows, :] = acs if shift == 0 else pltpu.roll(acs, shift, axis=1)
    n_c = tp // L
    pad = [jnp.zeros((SUBLANES - n_c % SUBLANES, LANES), F32)] if n_c % SUBLANES else []
    el = jnp.exp2(jnp.concatenate(lasts + pad, axis=0))
    hi, mid, lo = _split3(el)
    e = efull_ref[...]
    dec = _dot(hi, e) + _dot(mid, e) + _dot(lo, e)
    for c in range(n_c):
        decend_ref[c] = dec[c:c + 1, :]


def _ssd_pre(proj, col_off, dt_bias, a_log, d_inner, tp):
    m = proj.shape[0]
    nh = dt_bias.shape[0]
    assert nh == LANES and col_off % LANES == 0
    r = nh // SSD_N_GROUPS
    L = SSD_CHUNK
    tri = (jnp.arange(L)[:, None] >= jnp.arange(L)[None, :]).astype(BF16)
    efull = (jnp.arange(nh)[:, None] == (jnp.arange(d_inner)[None, :] // SSD_HEAD_DIM)).astype(BF16)
    nct = m // L
    return pl.pallas_call(
        functools.partial(_ssd_pre_kernel, tp=tp, r=r),
        grid=(m // tp,),
        in_specs=[
            pl.BlockSpec((tp, LANES), lambda i: (i, col_off // LANES)),
            pl.BlockSpec((1, LANES), lambda i: (0, 0)),
            pl.BlockSpec((1, LANES), lambda i: (0, 0)),
            pl.BlockSpec((L, L), lambda i: (0, 0)),
            pl.BlockSpec((nh, d_inner), lambda i: (0, 0)),
        ],
        out_specs=[
            pl.BlockSpec((tp, LANES), lambda i: (i, 0)),
            pl.BlockSpec((tp, LANES), lambda i: (i, 0)),
            pl.BlockSpec((tp, LANES), lambda i: (i, 0)),
            pl.BlockSpec((SSD_N_GROUPS, tp, LANES), lambda i: (0, i, 0)),
            pl.BlockSpec((tp // L, LANES, L), lambda i: (i, 0, 0)),
            pl.BlockSpec((tp // L, 1, d_inner), lambda i: (i, 0, 0)),
        ],
        out_shape=[
            jax.ShapeDtypeStruct((m, LANES), BF16),
            jax.ShapeDtypeStruct((m, LANES), BF16),
            jax.ShapeDtypeStruct((m, LANES), BF16),
            jax.ShapeDtypeStruct((SSD_N_GROUPS, m, LANES), F32),
            jax.ShapeDtypeStruct((nct, LANES, L), F32),
            jax.ShapeDtypeStruct((nct, 1, d_inner), F32),
        ],
        compiler_params=_cparams(("parallel",)),
        name="ssd_pre",
    )(proj, dt_bias.reshape(1, nh), a_log.reshape(1, nh), tri, efull)


def _ssd_kernel(xs_ref, bm_ref, cm_ref, z_ref, dtb_ref, eacs_ref, toend_ref, acs_ref, acst_ref,
                decend_ref, cwx_ref, cbx_ref, cwbc_ref, cbbc_ref, dskip_ref,
                nw_ref, e_ref, shift_ref, o_ref,
                xin_sc, bcin_sc, state_sc, *, t_blk, r):
    L = SSD_CHUNK
    K = SSD_CONV_WIDTH
    P = SSD_HEAD_DIM
    N = SSD_D_STATE
    gw = r * P
    halo = CONV_HALO

    @pl.when(pl.program_id(2) == 0)
    def _():
        state_sc[...] = jnp.zeros_like(state_sc)
        xin_sc[0:halo, :] = jnp.zeros((halo, gw), BF16)
        bcin_sc[0:halo, :] = jnp.zeros((halo, 2 * N), BF16)

    xin_sc[halo:halo + t_blk, :] = xs_ref[...]
    bcin_sc[halo:halo + t_blk, 0:N] = bm_ref[...]
    bcin_sc[halo:halo + t_blk, N:2 * N] = cm_ref[...]

    def conv(in_sc, rows_ext, w_ref, b_ref):
        xe = in_sc[rows_ext, :]
        sh = _dot(shift_ref[...], xe)
        acc = b_ref[...] + w_ref[K - 1:K, :] * xe[halo:halo + L, :].astype(F32)
        for k in range(K - 1):
            acc = acc + w_ref[k:k + 1, :] * sh[k * L:(k + 1) * L, :]
        return _silu(acc)

    li = lax.broadcasted_iota(jnp.int32, (L, L), 0)
    si = lax.broadcasted_iota(jnp.int32, (L, L), 1)
    causal = li >= si
    lane = lax.broadcasted_iota(jnp.int32, (L, 2 * P), 1)
    lo_half = lane < P

    def chunk(c, carry):
        rows = pl.ds(pl.multiple_of(c * L, L), L)
        rows_ext = pl.ds(pl.multiple_of(c * L, L), halo + L)
        xc = conv(xin_sc, rows_ext, cwx_ref, cbx_ref)
        bc_c = conv(bcin_sc, rows_ext, cwbc_ref, cbbc_ref).astype(BF16)
        b_c = bc_c[:, 0:N]
        c_c = bc_c[:, N:2 * N]
        e = e_ref[...]
        dt_e = _dot(dtb_ref[rows, :], e)
        eacs_e = _dot(eacs_ref[rows, :], e)
        toend_e = _dot(toend_ref[rows, :], e)
        xdt = xc * dt_e
        xdt_b = xdt.astype(BF16)
        xw_b = (xdt * toend_e).astype(BF16)
        cb_b = lax.dot_general(c_c, b_c, (((1,), (1,)), ((), ())), preferred_element_type=F32).astype(BF16)
        acs = acs_ref[rows, :]
        acst = acst_ref[c]
        st = state_sc[...]
        y = _dot(c_c, st.astype(BF16)) * eacs_e
        pieces = []
        for jp in range(r // 2):
            xp = xdt_b[:, jp * 2 * P:(jp + 1) * 2 * P]
            acc = None
            for u in range(2):
                j = 2 * jp + u
                colb = jnp.broadcast_to(acs[:, j:j + 1], (L, L))
                rowb = jnp.broadcast_to(acst[j:j + 1, :], (L, L))
                seg = (colb - rowb).astype(BF16)
                m_h = cb_b * jnp.exp2(jnp.where(causal, seg, -jnp.inf))
                xm = jnp.where(lo_half if u == 0 else jnp.logical_not(lo_half), xp, jnp.zeros_like(xp))
                d = _dot(m_h, xm)
                acc = d if acc is None else acc + d
            pieces.append(acc)
        y = y + jnp.concatenate(pieces, axis=1)
        upd = lax.dot_general(b_c, xw_b, (((0,), (0,)), ((), ())), preferred_element_type=F32)
        state_sc[...] = st * decend_ref[c] + upd
        y = y + dskip_ref[...] * xc
        yg = y * _silu(z_ref[rows, :].astype(F32))
        ms = jnp.mean(yg * yg, axis=-1, keepdims=True)
        o_ref[rows, :] = (yg * lax.rsqrt(ms + NORM_EPS) * nw_ref[...]).astype(o_ref.dtype)
        return carry

    lax.fori_loop(0, t_blk // L, chunk, 0)
    xin_sc[0:halo, :] = xin_sc[t_blk:t_blk + halo, :]
    bcin_sc[0:halo, :] = bcin_sc[t_blk:t_blk + halo, :]


def _ssd(proj, offs, pre, conv_w, conv_b, d_skip, ssd_norm_w, batch, seq, t_blk):
    m = proj.shape[0]
    d_inner = ssd_norm_w.shape[0]
    nh = d_skip.shape[0]
    G = SSD_N_GROUPS
    N = SSD_D_STATE
    L = SSD_CHUNK
    r = nh // G
    gw = r * SSD_HEAD_DIM
    assert gw % LANES == 0 and r % 2 == 0 and r % SUBLANES == 0
    dtb, eacs, toend, acsrot, acst, decend = pre
    nt = seq // t_blk
    ncb = t_blk // L
    e_all = (jnp.arange(LANES)[None, :, None]
             == (jnp.arange(G)[:, None, None] * r + jnp.arange(gw)[None, None, :] // SSD_HEAD_DIM)).astype(BF16)
    dskip_e = jnp.repeat(d_skip.astype(F32), SSD_HEAD_DIM).reshape(1, d_inner)

    def bc_pairs(a):
        b_part = a[:, d_inner:d_inner + G * N].reshape(-1, G, 1, N)
        c_part = a[:, d_inner + G * N:].reshape(-1, G, 1, N)
        return jnp.concatenate([b_part, c_part], axis=2).reshape(-1, G * 2 * N)

    cb2 = conv_b.reshape(1, -1)
    cwx, cbx = conv_w[:, :d_inner], cb2[:, :d_inner]
    cwbc, cbbc = bc_pairs(conv_w), bc_pairs(cb2)
    kw = SSD_CONV_WIDTH
    tt = jnp.arange((kw - 1) * L)
    shift = (jnp.arange(CONV_HALO + L)[None, :]
             == (CONV_HALO + tt % L - (kw - 1) + tt // L)[:, None]).astype(BF16)
    for o in (offs["xs"], offs["z"]):
        assert o % gw == 0
    row = lambda b, g, t: b * nt + t
    in_specs = [
        pl.BlockSpec((t_blk, gw), lambda b, g, t: (row(b, g, t), offs["xs"] // gw + g)),
        pl.BlockSpec((t_blk, N), lambda b, g, t: (row(b, g, t), offs["bm"] // N + g)),
        pl.BlockSpec((t_blk, N), lambda b, g, t: (row(b, g, t), offs["cm"] // N + g)),
        pl.BlockSpec((t_blk, gw), lambda b, g, t: (row(b, g, t), offs["z"] // gw + g)),
        pl.BlockSpec((t_blk, LANES), lambda b, g, t: (row(b, g, t), 0)),
        pl.BlockSpec((t_blk, LANES), lambda b, g, t: (row(b, g, t), 0)),
        pl.BlockSpec((t_blk, LANES), lambda b, g, t: (row(b, g, t), 0)),
        pl.BlockSpec((None, t_blk, LANES), lambda b, g, t: (g, row(b, g, t), 0)),
        pl.BlockSpec((ncb, r, L), lambda b, g, t: (row(b, g, t), g, 0)),
        pl.BlockSpec((ncb, 1, gw), lambda b, g, t: (row(b, g, t), 0, g)),
        pl.BlockSpec((SSD_CONV_WIDTH, gw), lambda b, g, t: (0, g)),
        pl.BlockSpec((1, gw), lambda b, g, t: (0, g)),
        pl.BlockSpec((SSD_CONV_WIDTH, 2 * N), lambda b, g, t: (0, g)),
        pl.BlockSpec((1, 2 * N), lambda b, g, t: (0, g)),
        pl.BlockSpec((1, gw), lambda b, g, t: (0, g)),
        pl.BlockSpec((1, gw), lambda b, g, t: (0, g)),
        pl.BlockSpec((None, LANES, gw), lambda b, g, t: (g, 0, 0)),
        pl.BlockSpec(((kw - 1) * L, CONV_HALO + L), lambda b, g, t: (0, 0)),
    ]
    return pl.pallas_call(
        functools.partial(_ssd_kernel, t_blk=t_blk, r=r),
        grid=(batch, G, nt),
        in_specs=in_specs,
        out_specs=pl.BlockSpec((t_blk, gw), lambda b, g, t: (row(b, g, t), g)),
        out_shape=jax.ShapeDtypeStruct((m, d_inner), BF16),
        scratch_shapes=[
            pltpu.VMEM((t_blk + CONV_HALO, gw), BF16),
            pltpu.VMEM((t_blk + CONV_HALO, 2 * N), BF16),
            pltpu.VMEM((N, gw), F32),
        ],
        compiler_params=_cparams(("parallel", "parallel", "arbitrary")),
        name="ssd_scan",
    )(proj, proj, proj, proj, dtb, eacs, toend, acsrot, acst, decend,
      cwx, cbx, cwbc, cbbc, dskip_e, ssd_norm_w.reshape(1, d_inner), e_all, shift)


def _flash_kernel(q_ref, kn_ref, kr_ref, v_ref, g_ref, o_ref, vt_sc, sa_sc, sb_sc, xa_sc, xb_sc,
                  qt0_sc, qt1_sc, m0_sc, m1_sc, l0_sc, l1_sc, acc0_sc, acc1_sc, *, tq, tk, seq):
    n_sub = tq // tk
    nq = seq // tq
    tqs = min(2 * LANES, tq)
    nch = tq // tqs
    assert n_sub == 2 and nq % 2 == 0 and nq >= 4
    buf_a, buf_b = (sa_sc, xa_sc), (sb_sc, xb_sc)
    state = ((qt0_sc, m0_sc, l0_sc, acc0_sc), (qt1_sc, m1_sc, l1_sc, acc1_sc))

    for c in range(seq // tk):
        vt_sc[c] = v_ref[c * tk:(c + 1) * tk, :].astype(F32).T.astype(BF16)

    def step(nxt_blk, cur_blk):
        nxt, cur = [], []
        if nxt_blk is not None:
            kb_next, bufs_next, diag_next, qt = nxt_blk
            rows = pl.ds(pl.multiple_of(kb_next * tk, tk), tk)
            k = jnp.concatenate([kn_ref[rows, :], kr_ref[rows, :]], axis=1)
            k0 = 0 if diag_next is None else diag_next * tk
            nxt = list(range(k0, tq, tqs))
        if cur_blk is not None:
            kb, bufs, diag, (m_sc, l_sc, acc_sc) = cur_blk
            vt = vt_sc[kb]
            cur = list(range(0 if diag is None else diag * tk, tq, tqs))
        def n_keys(blk_diag, blk_k0, c0):
            return tqs if (blk_diag is not None and c0 == blk_k0) else tk

        for i in range(max(len(nxt), len(cur))):
            if i < len(nxt):
                c0 = nxt[i]
                nk = n_keys(diag_next, k0, c0)
                st = _dot(k[0:nk], qt[c0 // tqs])
                if diag_next is not None and c0 < k0 + tk:
                    ki = lax.broadcasted_iota(jnp.int32, (nk, tqs), 0) + k0
                    qj = lax.broadcasted_iota(jnp.int32, (nk, tqs), 1) + c0
                    st = jnp.where(ki <= qj, st, -jnp.inf)
                bufs_next[0][c0 // tqs, 0:nk] = st
                bufs_next[1][c0 // tqs] = jnp.max(st, axis=0, keepdims=True)
            if i < len(cur):
                ci = cur[i] // tqs
                nk = n_keys(diag, 0 if diag is None else diag * tk, cur[i])
                st = bufs[0][ci, 0:nk]
                m_prev = m_sc[ci]
                m_new = jnp.maximum(m_prev, bufs[1][ci])
                alpha = jnp.exp2(m_prev - m_new)
                p = jnp.exp2(st - m_new)
                l_sc[ci] = alpha * l_sc[ci] + jnp.sum(p, axis=0, keepdims=True)
                acc_sc[ci] = acc_sc[ci] * alpha + _dot(vt[:, 0:nk], p.astype(BF16))
                m_sc[ci] = m_new

    def init(qi, par):
        qt, m_sc, l_sc, acc_sc = state[par]
        for ci in range(nch):
            qsub = pl.ds(pl.multiple_of(qi * tq + ci * tqs, tqs), tqs)
            qt[ci] = q_ref[qsub, :].astype(F32).T.astype(BF16)
        m_sc[...] = jnp.full(m_sc.shape, -jnp.inf, F32)
        l_sc[...] = jnp.zeros(l_sc.shape, F32)
        acc_sc[...] = jnp.zeros(acc_sc.shape, F32)

    def finalize(qi, par):
        _, _, l_sc, acc_sc = state[par]
        for ci in range(nch):
            qsub = pl.ds(pl.multiple_of(qi * tq + ci * tqs, tqs), tqs)
            o = (acc_sc[ci] / l_sc[ci]).T
            o_ref[qsub, :] = (o * _silu(g_ref[qsub, :].astype(F32))).astype(o_ref.dtype)

    def body(qi, par):
        qt, stats = state[par][0], state[par][1:]
        d0 = qi * n_sub

        def pair(j):
            step((2 * j + 1, buf_b, None, qt), (2 * j, buf_a, None, stats))
            step((2 * j + 2, buf_a, None, qt), (2 * j + 1, buf_b, None, stats))

        def quad(i, c):
            pair(2 * i)
            pair(2 * i + 1)
            return c

        n_pairs = qi - 1
        lax.fori_loop(0, n_pairs // 2, quad, 0)

        @pl.when(n_pairs % 2 == 1)
        def _():
            pair(n_pairs - 1)

        step((d0 - 1, buf_b, None, qt), (d0 - 2, buf_a, None, stats))
        step((d0, buf_a, 0, qt), (d0 - 1, buf_b, None, stats))
        step((d0 + 1, buf_b, 1, qt), (d0, buf_a, 0, stats))

    def handover(qi, par):
        init(qi + 1, 1 - par)
        step((0, buf_a, None, state[1 - par][0]), (qi * n_sub + 1, buf_b, 1, state[par][1:]))
        finalize(qi, par)

    init(0, 0)
    step((0, buf_a, 0, state[0][0]), None)
    step((1, buf_b, 1, state[0][0]), (0, buf_a, 0, state[0][1:]))
    handover(0, 0)

    def two_blocks(i, carry):
        qi = 2 * i + 1
        body(qi, 1)
        handover(qi, 1)
        body(qi + 1, 0)
        handover(qi + 1, 0)
        return carry

    lax.fori_loop(0, nq // 2 - 1, two_blocks, 0)
    body(nq - 1, 1)
    step(None, ((nq - 1) * n_sub + 1, buf_b, 1, state[1][1:]))
    finalize(nq - 1, 1)


def _flash(q, kv, kr, proj, gate_off, batch, seq, n_heads, tq, tk):
    m = q.shape[0]
    dq = 2 * LANES
    tqs = min(2 * LANES, tq)
    nch = tq // tqs
    assert gate_off % LANES == 0
    return pl.pallas_call(
        functools.partial(_flash_kernel, tq=tq, tk=tk, seq=seq),
        grid=(batch, n_heads),
        in_specs=[
            pl.BlockSpec((seq, dq), lambda b, h: (b, h)),
            pl.BlockSpec((seq, LANES), lambda b, h: (b, 2 * h)),
            pl.BlockSpec((seq, LANES), lambda b, h: (b, 0)),
            pl.BlockSpec((seq, LANES), lambda b, h: (b, 2 * h + 1)),
            pl.BlockSpec((seq, LANES), lambda b, h: (b, gate_off // LANES + h)),
        ],
        out_specs=pl.BlockSpec((seq, LANES), lambda b, h: (b, h)),
        out_shape=jax.ShapeDtypeStruct((m, n_heads * MLA_V_DIM), BF16),
        scratch_shapes=[pltpu.VMEM((seq // tk, MLA_V_DIM, tk), BF16),
                        pltpu.VMEM((nch, tk, tqs), F32), pltpu.VMEM((nch, tk, tqs), F32),
                        pltpu.VMEM((nch, 1, tqs), F32), pltpu.VMEM((nch, 1, tqs), F32)]
                       + [pltpu.VMEM((nch, dq, tqs), BF16)] * 2 + [pltpu.VMEM((nch, 1, tqs), F32)] * 4
                       + [pltpu.VMEM((nch, MLA_V_DIM, tqs), F32)] * 2,
        compiler_params=_cparams(("parallel", "parallel")),
        name="mla_flash",
    )(q, kv, kr, kv, proj)


def _ssd_out_kernel(a_ref, w_ref, g_ref, o_ref):
    y = _dot(a_ref[...], w_ref[...])
    o_ref[...] = (jax.nn.sigmoid(g_ref[...].astype(F32)) * y).astype(o_ref.dtype)


def _ssd_out(ys, w_ssd, l, proj, gate_off, tm, tn):
    m, k1 = ys.shape
    d = w_ssd.shape[2]
    assert gate_off % tn == 0
    return pl.pallas_call(
        _ssd_out_kernel,
        grid=(m // tm, d // tn),
        in_specs=[
            pl.BlockSpec((tm, k1), lambda i, j: (i, 0)),
            pl.BlockSpec((None, k1, tn), lambda i, j: (l, 0, j)),
            pl.BlockSpec((tm, tn), lambda i, j: (i, gate_off // tn + j)),
        ],
        out_specs=pl.BlockSpec((tm, tn), lambda i, j: (i, j)),
        out_shape=jax.ShapeDtypeStruct((m, d), BF16),
        compiler_params=_cparams(("parallel", "parallel")),
        name="ssd_out",
    )(ys, w_ssd, proj)


def _mla_out_kernel(a_ref, w_ref, g_ref, p_ref, o_ref):
    y = _dot(a_ref[...], w_ref[...])
    o_ref[...] = (p_ref[...].astype(F32) + jax.nn.sigmoid(g_ref[...].astype(F32)) * y).astype(o_ref.dtype)


def _mla_out(og, w_mla, l, proj, gate_off, part, tm, tn):
    m, k2 = og.shape
    d = w_mla.shape[2]
    assert gate_off % tn == 0
    return pl.pallas_call(
        _mla_out_kernel,
        grid=(m // tm, d // tn),
        in_specs=[
            pl.BlockSpec((tm, k2), lambda i, j: (i, 0)),
            pl.BlockSpec((None, k2, tn), lambda i, j: (l, 0, j)),
            pl.BlockSpec((tm, tn), lambda i, j: (i, gate_off // tn + j)),
            pl.BlockSpec((tm, tn), lambda i, j: (i, j)),
        ],
        out_specs=pl.BlockSpec((tm, tn), lambda i, j: (i, j)),
        out_shape=jax.ShapeDtypeStruct((m, d), BF16),
        compiler_params=_cparams(("parallel", "parallel")),
        name="mla_out",
    )(og, w_mla, proj, part)


def _out_kernel(a_ref, w_ref, x_ref, pw_ref, gate_ref, o_ref, *, nk):
    k = pl.program_id(1)

    @pl.when(k == 0)
    def _():
        o_ref[...] = jnp.zeros_like(o_ref)

    o_ref[...] += _dot(a_ref[...], w_ref[...])

    @pl.when(k == nk - 1)
    def _():
        y = o_ref[...]
        yn = y * lax.rsqrt(jnp.mean(y * y, axis=-1, keepdims=True) + NORM_EPS) * pw_ref[...]
        o_ref[...] = x_ref[...] + gate_ref[...] * yn


def _out_proj(merged, w_out, l, x2, post_w, mod4, seq, tm, tk):
    m, d = x2.shape
    nk = d // tk
    per_b = seq // tm
    return pl.pallas_call(
        functools.partial(_out_kernel, nk=nk),
        grid=(m // tm, nk),
        in_specs=[
            pl.BlockSpec((tm, tk), lambda i, k: (i, k)),
            pl.BlockSpec((None, tk, d), lambda i, k: (l, k, 0)),
            pl.BlockSpec((tm, d), lambda i, k: (i, 0)),
            pl.BlockSpec((1, d), lambda i, k: (0, 0)),
            pl.BlockSpec((None, None, 1, d), lambda i, k: (i // per_b, 2, 0, 0)),
        ],
        out_specs=pl.BlockSpec((tm, d), lambda i, k: (i, 0)),
        out_shape=jax.ShapeDtypeStruct((m, d), F32),
        compiler_params=_cparams(("parallel", "arbitrary")),
        name="out_proj",
    )(merged, w_out, x2, post_w.reshape(1, d), mod4)


def _proj_layout(d, d_inner, nh, n_mla_heads, q_lora, kv_lora):
    gn = SSD_N_GROUPS * SSD_D_STATE
    widths = [("z", d_inner), ("xs", d_inner), ("bm", gn), ("cm", gn),
              ("gate", n_mla_heads * MLA_V_DIM), ("merge", 2 * d), ("cq", q_lora), ("ckv", kv_lora),
              ("kr", 2 * MLA_QK_ROPE), ("dt", nh)]
    offs, o = {}, 0
    for name, w in widths:
        offs[name] = o
        o += w
    return offs, o


def _cast_kernel(w_ref, o_ref):
    o_ref[...] = w_ref[...].astype(o_ref.dtype)


def _cast_bf16(w3):
    depth, k, n = w3.shape
    tr = max(2 * SUBLANES, min(k, CAST_BLOCK_BYTES // (n * 4)))
    assert k % tr == 0
    return pl.pallas_call(
        _cast_kernel,
        grid=(depth, k // tr),
        in_specs=[pl.BlockSpec((None, tr, n), lambda l, r: (l, r, 0))],
        out_specs=pl.BlockSpec((None, tr, n), lambda l, r: (l, r, 0)),
        out_shape=jax.ShapeDtypeStruct((depth, k, n), BF16),
        compiler_params=_cparams(("parallel", "parallel")),
        name="cast_bf16",
    )(w3)


def _w_in_kernel(src_ref, mode_ref, *refs, n_chunks):
    o_ref = refs[n_chunks]
    j = pl.program_id(1)
    half = MLA_QK_ROPE // 2
    parts = []
    for c in range(n_chunks):
        blk = refs[c][...]
        mode = mode_ref[j * n_chunks + c]
        rot = jnp.concatenate([-blk[half:], blk[:half]], axis=0)
        v = jnp.where(mode == W_IN_ROTATE, rot, blk)
        parts.append(jnp.where(mode == W_IN_ZERO, 0.0, v))
    o_ref[...] = jnp.concatenate(parts, axis=0).T.astype(BF16)


def _prep_w_in(w_in3, offs, n_tot, n_pad, d, d_inner, nh, n_mla_heads, q_lora, kv_lora):
    depth, k, n_src = w_in3.shape
    gn = SSD_N_GROUPS * SSD_D_STATE
    conv_dim = d_inner + 2 * gn
    s_xbc = d_inner
    s_dt = s_xbc + conv_dim
    s_cq = s_dt + nh
    s_ckv = s_cq + q_lora
    s_kr = s_ckv + kv_lora
    s_gate = s_kr + MLA_QK_ROPE
    s_merge = s_gate + n_mla_heads * MLA_V_DIM
    assert s_merge + 2 * d == n_src and MLA_QK_ROPE == W_IN_CHUNK
    segments = (
        (offs["z"], 0, d_inner + conv_dim, W_IN_COPY),
        (offs["gate"], s_gate, n_mla_heads * MLA_V_DIM + 2 * d, W_IN_COPY),
        (offs["cq"], s_cq, q_lora + kv_lora, W_IN_COPY),
        (offs["kr"], s_kr, MLA_QK_ROPE, W_IN_COPY),
        (offs["kr"] + MLA_QK_ROPE, s_kr, MLA_QK_ROPE, W_IN_ROTATE),
        (offs["dt"], s_dt, nh, W_IN_COPY),
    )
    n_dst = n_pad // W_IN_CHUNK
    src_tab = [0] * n_dst
    mode_tab = [W_IN_ZERO] * n_dst
    for dst, src, width, mode in segments:
        assert dst % W_IN_CHUNK == 0 and src % W_IN_CHUNK == 0 and width % W_IN_CHUNK == 0
        for t in range(width // W_IN_CHUNK):
            src_tab[dst // W_IN_CHUNK + t] = src // W_IN_CHUNK + t
            mode_tab[dst // W_IN_CHUNK + t] = mode
    n_chunks = W_IN_COLS // W_IN_CHUNK
    w_t = jnp.swapaxes(w_in3, 1, 2)

    def chunk_spec(c):
        return pl.BlockSpec((None, W_IN_CHUNK, k), lambda l, j, src, mode: (l, src[j * n_chunks + c], 0))

    return pl.pallas_call(
        functools.partial(_w_in_kernel, n_chunks=n_chunks),
        grid_spec=pltpu.PrefetchScalarGridSpec(
            num_scalar_prefetch=2,
            grid=(depth, n_pad // W_IN_COLS),
            in_specs=[chunk_spec(c) for c in range(n_chunks)],
            out_specs=pl.BlockSpec((None, k, W_IN_COLS), lambda l, j, src, mode: (l, 0, j)),
        ),
        out_shape=jax.ShapeDtypeStruct((depth, k, n_pad), BF16),
        compiler_params=_cparams(("parallel", "parallel")),
        name="w_in_prep",
    )(jnp.asarray(src_tab, jnp.int32), jnp.asarray(mode_tab, jnp.int32), *([w_t] * n_chunks))


def _w_q_kernel(w_ref, o_ref, *, n_heads):
    qk = MLA_QK_NOPE + MLA_QK_ROPE
    half = MLA_QK_ROPE // 2
    scale = qk ** -0.5 * LOG2_E

    def load(src, width):
        lead = src % LANES
        v = w_ref[:, src - lead:src + width]
        return v[:, lead:] if lead else v

    for h in range(n_heads):
        nope = load(h * qk, MLA_QK_NOPE)
        rope = load(h * qk + MLA_QK_NOPE, MLA_QK_ROPE)
        blk = jnp.concatenate([nope, rope, -rope[:, half:], rope[:, :half]], axis=1)
        o_ref[:, h * 2 * LANES:(h + 1) * 2 * LANES] = (blk * scale).astype(BF16)


def _prep_w_q(w_q_up3, n_mla_heads):
    depth, k, n = w_q_up3.shape
    tr = min(k, 256)
    return pl.pallas_call(
        functools.partial(_w_q_kernel, n_heads=n_mla_heads),
        grid=(depth, k // tr),
        in_specs=[pl.BlockSpec((None, tr, n), lambda l, r: (l, r, 0))],
        out_specs=pl.BlockSpec((None, tr, n_mla_heads * 2 * LANES), lambda l, r: (l, r, 0)),
        out_shape=jax.ShapeDtypeStruct((depth, k, n_mla_heads * 2 * LANES), BF16),
        compiler_params=_cparams(("parallel", "parallel")),
        name="w_q_prep",
    )(w_q_up3)


def _round_up(v, mult):
    return (v + mult - 1) // mult * mult


def _tile_plan(m, seq, d, n_mla):
    qkv_n = n_mla * 2 * LANES
    tq = min(1024, seq)
    return dict(
        tm=min(1024, m),
        proj_tn=1024,
        q_tn=min(2048, qkv_n), kv_tn=min(4096, qkv_n),
        ssd_pre_rows=min(1024, seq), ssd_rows=min(1024, seq),
        flash_tq=tq, flash_tk=min(512, tq),
        ssd_out_tn=min(512, d), mla_out_tn=min(1024, d),
        out_tm=min(512, seq), out_tk=min(512, d),
    )


def kernel(x, c, positions, ada_w, ada_b, pre_norm_w, post_norm_w, w_in, conv_w, conv_b, dt_bias,
           a_log, d_skip, ssd_norm_w, q_norm_w, w_q_up, kv_norm_w, w_kv_up, w_ssd_proj, w_mla_proj,
           w_out):
    batch, seq, d = x.shape
    depth = ada_w.shape[0]
    m = batch * seq
    d_inner = ssd_norm_w.shape[1]
    nh = dt_bias.shape[1]
    q_lora = q_norm_w.shape[1]
    kv_lora = kv_norm_w.shape[1]
    n_mla = w_mla_proj.shape[1] // MLA_V_DIM
    offs, n_tot = _proj_layout(d, d_inner, nh, n_mla, q_lora, kv_lora)
    t = _tile_plan(m, seq, d, n_mla)
    n_pad = _round_up(n_tot, t["proj_tn"])

    c_pad = jnp.zeros((SUBLANES, d), F32).at[:batch].set(c)
    mod = _ada(c_pad, ada_w, ada_b)
    cos, sin = _rope_tables(positions)

    w_in_b = _prep_w_in(w_in, offs, n_tot, n_pad, d, d_inner, nh, n_mla, q_lora, kv_lora)
    w_q_b = _prep_w_q(w_q_up, n_mla)
    w_kv_b = _cast_bf16(w_kv_up)
    out_side = (w_ssd_proj, w_mla_proj, w_out)

    x2 = x.reshape(m, d)
    for l in range(depth):
        mod4 = mod[l, :batch].reshape(batch, 3, 1, d)
        h = _prenorm(x2, pre_norm_w[l], mod4, seq)
        if l == 0:
            steps = (m // t["tm"]) * (n_pad // t["proj_tn"])
            ride = [_rides_along(w, steps) for w in out_side]
            proj, cast = _matmul(h, w_in_b, l, BF16, t["tm"], t["proj_tn"],
                                 side_casts=[w.reshape(-1, w.shape[-1]) for w, r in zip(out_side, ride) if r])
            cast = iter(cast)
            w_ssd_b, w_mla_b, w_out_b = (next(cast).reshape(w.shape) if r else _cast_bf16(w)
                                         for w, r in zip(out_side, ride))
        else:
            proj, _ = _matmul(h, w_in_b, l, BF16, t["tm"], t["proj_tn"])

        pre = _ssd_pre(proj, offs["dt"], dt_bias[l], a_log[l], d_inner, tp=t["ssd_pre_rows"])
        ys = _ssd(proj, offs, pre, conv_w[l], conv_b[l], d_skip[l], ssd_norm_w[l], batch, seq,
                  t_blk=t["ssd_rows"])

        q = _norm_matmul(proj, offs["cq"], q_lora, q_norm_w[l], w_q_b, l,
                         t["tm"], t["q_tn"], cos, sin, name="q_up")
        kv = _norm_matmul(proj, offs["ckv"], kv_lora, kv_norm_w[l], w_kv_b, l,
                          t["tm"], t["kv_tn"], name="kv_up")
        kr = _krope(proj, offs["kr"], cos, sin)
        og = _flash(q, kv, kr, proj, offs["gate"], batch, seq, n_mla, t["flash_tq"], t["flash_tk"])

        part = _ssd_out(ys, w_ssd_b, l, proj, offs["merge"], t["tm"], t["ssd_out_tn"])
        merged = _mla_out(og, w_mla_b, l, proj, offs["merge"] + d, part, t["tm"], t["mla_out_tn"])
        x2 = _out_proj(merged, w_out_b, l, x2, post_norm_w[l], mod4, seq, t["out_tm"], t["out_tk"])
    return x2.reshape(batch, seq, d)
```

```python
import functools
import math

import jax
import jax.numpy as jnp
from jax import lax
from jax.experimental import pallas as pl
from jax.experimental.pallas import tpu as pltpu

SSD_HEAD_DIM = 64
SSD_N_GROUPS = 8
SSD_D_STATE = 128
SSD_CHUNK = 128
SSD_CONV_WIDTH = 4
MLA_QK_NOPE = 128
MLA_QK_ROPE = 64
MLA_V_DIM = 128
ROPE_THETA = 10000.0
NORM_EPS = 1e-6
LOG2_E = math.log2(math.e)

LANES = 128
SUBLANES = 8
CONV_HALO = 16
VMEM_LIMIT = 56 * 1024 * 1024
OUT_PROJ_VMEM_LIMIT = 58 * 1024 * 1024
CAST_BLOCK_BYTES = 8 * 1024 * 1024
W_IN_CHUNK = 64
W_IN_COLS = 512
W_IN_COPY, W_IN_ROTATE, W_IN_ZERO = 0, 1, 2

F32 = jnp.float32
BF16 = jnp.bfloat16


def _cparams(sem):
    return pltpu.CompilerParams(dimension_semantics=sem, vmem_limit_bytes=VMEM_LIMIT)


def _silu(v):
    h = 0.5 * v
    return h + h * jnp.tanh(h)


def _split3(v):
    hi = v.astype(BF16)
    r1 = v - hi.astype(F32)
    mid = r1.astype(BF16)
    lo = (r1 - mid.astype(F32)).astype(BF16)
    return hi, mid, lo


def _dot(a, b):
    return jnp.dot(a, b, preferred_element_type=F32)


def _ada_kernel(c_ref, w_ref, b_ref, o_ref):
    @pl.when(pl.program_id(1) == 0)
    def _():
        o_ref[...] = jnp.broadcast_to(b_ref[...], o_ref.shape)

    a = _silu(c_ref[...]).astype(BF16)
    o_ref[...] += _dot(a, w_ref[...].astype(BF16))


def _ada(c_pad, ada_w, ada_b):
    depth, d, n = ada_w.shape
    tk = min(256, d)
    return pl.pallas_call(
        _ada_kernel,
        grid=(depth, d // tk),
        in_specs=[
            pl.BlockSpec((SUBLANES, tk), lambda l, k: (0, k)),
            pl.BlockSpec((None, tk, n), lambda l, k: (l, k, 0)),
            pl.BlockSpec((None, 1, n), lambda l, k: (l, 0, 0)),
        ],
        out_specs=pl.BlockSpec((None, SUBLANES, n), lambda l, k: (l, 0, 0)),
        out_shape=jax.ShapeDtypeStruct((depth, SUBLANES, n), F32),
        compiler_params=_cparams(("parallel", "arbitrary")),
        name="ada_mod",
    )(c_pad, ada_w, ada_b.reshape(depth, 1, n))


def _prenorm_kernel(x_ref, w_ref, shift_ref, scale_ref, o_ref):
    x = x_ref[...]
    y = x * lax.rsqrt(jnp.mean(x * x, axis=-1, keepdims=True) + NORM_EPS)
    o_ref[...] = (y * w_ref[...] * (1.0 + scale_ref[...]) + shift_ref[...]).astype(BF16)


def _prenorm(x2, w, mod4, seq):
    m, d = x2.shape
    tm = min(512, seq)
    per_b = seq // tm
    return pl.pallas_call(
        _prenorm_kernel,
        grid=(m // tm,),
        in_specs=[
            pl.BlockSpec((tm, d), lambda i: (i, 0)),
            pl.BlockSpec((1, d), lambda i: (0, 0)),
            pl.BlockSpec((None, None, 1, d), lambda i: (i // per_b, 0, 0, 0)),
            pl.BlockSpec((None, None, 1, d), lambda i: (i // per_b, 1, 0, 0)),
        ],
        out_specs=pl.BlockSpec((tm, d), lambda i: (i, 0)),
        out_shape=jax.ShapeDtypeStruct((m, d), BF16),
        compiler_params=_cparams(("parallel",)),
        name="prenorm",
    )(x2, w.reshape(1, d), mod4, mod4)


def _mm_kernel(a_ref, w_ref, *rest, n_side):
    o_ref = rest[n_side]
    o_ref[...] = _dot(a_ref[...], w_ref[...]).astype(o_ref.dtype)
    for s in range(n_side):
        rest[n_side + 1 + s][...] = rest[s][...].astype(BF16)


def _rides_along(w3, steps):
    rows = w3.shape[0] * w3.shape[1]
    return rows % steps == 0 and (rows // steps) % (2 * SUBLANES) == 0


def _matmul(a, w3, l, out_dtype, tm, tn, side_casts=()):
    m, k = a.shape
    n = w3.shape[2]
    gi, gj = m // tm, n // tn
    side_specs, side_shapes = [], []
    for w in side_casts:
        rows, cols = w.shape
        spec = pl.BlockSpec((rows // (gi * gj), cols), lambda i, j: (i * gj + j, 0))
        side_specs.append(spec)
        side_shapes.append(jax.ShapeDtypeStruct((rows, cols), BF16))
    res = pl.pallas_call(
        functools.partial(_mm_kernel, n_side=len(side_casts)),
        grid=(gi, gj),
        in_specs=[
            pl.BlockSpec((tm, k), lambda i, j: (i, 0)),
            pl.BlockSpec((None, k, tn), lambda i, j: (l, 0, j)),
        ] + side_specs,
        out_specs=[pl.BlockSpec((tm, tn), lambda i, j: (i, j))] + side_specs,
        out_shape=[jax.ShapeDtypeStruct((m, n), out_dtype)] + side_shapes,
        compiler_params=_cparams(("parallel", "parallel")),
        name="in_proj",
    )(a, w3, *side_casts)
    return res[0], res[1:]


def _rope_block(a, cos, sin):
    return a * cos + pltpu.roll(a, LANES // 2, axis=1) * sin


def _norm_mm_kernel(a_ref, nw_ref, w_ref, *rest, rope, tn):
    if rope:
        cos_ref, sin_ref, o_ref, an_sc = rest
    else:
        o_ref, an_sc = rest

    @pl.when(pl.program_id(1) == 0)
    def _():
        a = a_ref[...].astype(F32)
        y = a * lax.rsqrt(jnp.mean(a * a, axis=-1, keepdims=True) + NORM_EPS)
        an_sc[...] = (y * nw_ref[...]).astype(BF16)

    acc = _dot(an_sc[...], w_ref[...])
    if rope:
        cos = cos_ref[...]
        sin = sin_ref[...]
        for hh in range(tn // (2 * LANES)):
            c0 = hh * 2 * LANES
            o_ref[:, c0:c0 + LANES] = acc[:, c0:c0 + LANES].astype(o_ref.dtype)
            o_ref[:, c0 + LANES:c0 + 2 * LANES] = _rope_block(
                acc[:, c0 + LANES:c0 + 2 * LANES], cos, sin).astype(o_ref.dtype)
    else:
        o_ref[...] = acc.astype(o_ref.dtype)


def _norm_matmul(proj, col_off, k, norm_w, w, l, tm, tn, cos=None, sin=None, name="norm_mm"):
    m = proj.shape[0]
    n = w.shape[2]
    rope = cos is not None
    assert col_off % k == 0
    in_specs = [
        pl.BlockSpec((tm, k), lambda i, j: (i, col_off // k)),
        pl.BlockSpec((1, k), lambda i, j: (0, 0)),
        pl.BlockSpec((None, k, tn), lambda i, j: (l, 0, j)),
    ]
    args = [proj, norm_w.reshape(1, k), w]
    if rope:
        in_specs += [pl.BlockSpec((tm, LANES), lambda i, j: (i, 0))] * 2
        args += [cos, sin]
    return pl.pallas_call(
        functools.partial(_norm_mm_kernel, rope=rope, tn=tn),
        grid=(m // tm, n // tn),
        in_specs=in_specs,
        out_specs=pl.BlockSpec((tm, tn), lambda i, j: (i, j)),
        out_shape=jax.ShapeDtypeStruct((m, n), BF16),
        scratch_shapes=[pltpu.VMEM((tm, k), BF16)],
        compiler_params=_cparams(("parallel", "arbitrary")),
        name=name,
    )(*args)


def _rope_tab_kernel(pos_ref, freq_ref, cos_ref, sin_ref):
    ang = pos_ref[...].astype(F32) * freq_ref[...]
    lane = lax.broadcasted_iota(jnp.int32, ang.shape, 1)
    keep = lane < MLA_QK_ROPE
    cos_ref[...] = jnp.where(keep, jnp.cos(ang), 0.0)
    sin_ref[...] = jnp.where(keep, jnp.sin(ang), 0.0)


def _rope_tables(positions):
    m = positions.size
    tm = min(1024, m)
    half = MLA_QK_ROPE // 2
    inv_freq = ROPE_THETA ** (-(jnp.arange(0, MLA_QK_ROPE, 2, dtype=F32) / MLA_QK_ROPE))
    freq = jnp.concatenate([inv_freq, inv_freq, jnp.zeros((LANES - 2 * half,), F32)]).reshape(1, LANES)
    return pl.pallas_call(
        _rope_tab_kernel,
        grid=(m // tm,),
        in_specs=[pl.BlockSpec((tm, 1), lambda i: (i, 0)),
                  pl.BlockSpec((1, LANES), lambda i: (0, 0))],
        out_specs=[pl.BlockSpec((tm, LANES), lambda i: (i, 0))] * 2,
        out_shape=[jax.ShapeDtypeStruct((m, LANES), F32)] * 2,
        compiler_params=_cparams(("parallel",)),
        name="rope_tables",
    )(positions.reshape(m, 1), freq)


def _krope_kernel(a_ref, cos_ref, sin_ref, o_ref):
    o_ref[...] = _rope_block(a_ref[...].astype(F32), cos_ref[...], sin_ref[...]).astype(BF16)


def _krope(proj, col_off, cos, sin):
    m = proj.shape[0]
    tm = min(1024, m)
    return pl.pallas_call(
        _krope_kernel,
        grid=(m // tm,),
        in_specs=[pl.BlockSpec((tm, LANES), lambda i: (i, col_off // LANES)),
                  pl.BlockSpec((tm, LANES), lambda i: (i, 0)),
                  pl.BlockSpec((tm, LANES), lambda i: (i, 0))],
        out_specs=pl.BlockSpec((tm, LANES), lambda i: (i, 0)),
        out_shape=jax.ShapeDtypeStruct((m, LANES), BF16),
        compiler_params=_cparams(("parallel",)),
        name="k_rope",
    )(proj, cos, sin)


def _ssd_pre_kernel(dt_ref, bias_ref, alog_ref, tri_ref, efull_ref,
                    dtb_ref, eacs_ref, toend_ref, acsrot_ref, acst_ref, decend_ref, *, tp, r):
    L = SSD_CHUNK
    dt = jax.nn.softplus(dt_ref[...].astype(F32) + bias_ref[...])
    dtb_ref[...] = dt.astype(BF16)
    a = dt * (-jnp.exp(alog_ref[...])) * LOG2_E
    tri = tri_ref[...]
    lasts = []
    for c in range(tp // L):
        rows = slice(c * L, (c + 1) * L)
        hi, mid, lo = _split3(a[rows])
        acs = _dot(tri, hi) + _dot(tri, mid) + _dot(tri, lo)
        last = acs[L - 1:L, :]
        lasts.append(last)
        eacs_ref[rows, :] = jnp.exp2(acs).astype(BF16)
        toend_ref[rows, :] = jnp.exp2(last - acs).astype(BF16)
        acst_ref[c] = acs.T
        for g in range(SSD_N_GROUPS):
            shift = (LANES - g * r) % LANES
            acsrot_ref[g, rows, :] = acs if shift == 0 else pltpu.roll(acs, shift, axis=1)
    n_c = tp // L
    pad = [jnp.zeros((SUBLANES - n_c % SUBLANES, LANES), F32)] if n_c % SUBLANES else []
    el = jnp.exp2(jnp.concatenate(lasts + pad, axis=0))
    hi, mid, lo = _split3(el)
    e = efull_ref[...]
    dec = _dot(hi, e) + _dot(mid, e) + _dot(lo, e)
    for c in range(n_c):
        decend_ref[c] = dec[c:c + 1, :]


def _ssd_pre(proj, col_off, dt_bias, a_log, d_inner, tp):
    m = proj.shape[0]
    nh = dt_bias.shape[0]
    assert nh == LANES and col_off % LANES == 0
    r = nh // SSD_N_GROUPS
    L = SSD_CHUNK
    tri = (jnp.arange(L)[:, None] >= jnp.arange(L)[None, :]).astype(BF16)
    efull = (jnp.arange(nh)[:, None] == (jnp.arange(d_inner)[None, :] // SSD_HEAD_DIM)).astype(BF16)
    nct = m // L
    return pl.pallas_call(
        functools.partial(_ssd_pre_kernel, tp=tp, r=r),
        grid=(m // tp,),
        in_specs=[
            pl.BlockSpec((tp, LANES), lambda i: (i, col_off // LANES)),
            pl.BlockSpec((1, LANES), lambda i: (0, 0)),
            pl.BlockSpec((1, LANES), lambda i: (0, 0)),
            pl.BlockSpec((L, L), lambda i: (0, 0)),
            pl.BlockSpec((nh, d_inner), lambda i: (0, 0)),
        ],
        out_specs=[
            pl.BlockSpec((tp, LANES), lambda i: (i, 0)),
            pl.BlockSpec((tp, LANES), lambda i: (i, 0)),
            pl.BlockSpec((tp, LANES), lambda i: (i, 0)),
            pl.BlockSpec((SSD_N_GROUPS, tp, LANES), lambda i: (0, i, 0)),
            pl.BlockSpec((tp // L, LANES, L), lambda i: (i, 0, 0)),
            pl.BlockSpec((tp // L, 1, d_inner), lambda i: (i, 0, 0)),
        ],
        out_shape=[
            jax.ShapeDtypeStruct((m, LANES), BF16),
            jax.ShapeDtypeStruct((m, LANES), BF16),
            jax.ShapeDtypeStruct((m, LANES), BF16),
            jax.ShapeDtypeStruct((SSD_N_GROUPS, m, LANES), F32),
            jax.ShapeDtypeStruct((nct, LANES, L), F32),
            jax.ShapeDtypeStruct((nct, 1, d_inner), F32),
        ],
        compiler_params=_cparams(("parallel",)),
        name="ssd_pre",
    )(proj, dt_bias.reshape(1, nh), a_log.reshape(1, nh), tri, efull)


def _ssd_kernel(xs_ref, bm_ref, cm_ref, z_ref, dtb_ref, eacs_ref, toend_ref, acs_ref, acst_ref,
                decend_ref, cwx_ref, cbx_ref, cwbc_ref, cbbc_ref, dskip_ref,
                nw_ref, e_ref, shift_ref, o_ref,
                xin_sc, bcin_sc, state_sc, *, t_blk, r):
    L = SSD_CHUNK
    K = SSD_CONV_WIDTH
    P = SSD_HEAD_DIM
    N = SSD_D_STATE
    gw = r * P
    halo = CONV_HALO

    @pl.when(pl.program_id(2) == 0)
    def _():
        state_sc[...] = jnp.zeros_like(state_sc)
        xin_sc[0:halo, :] = jnp.zeros((halo, gw), BF16)
        bcin_sc[0:halo, :] = jnp.zeros((halo, 2 * N), BF16)

    xin_sc[halo:halo + t_blk, :] = xs_ref[...]
    bcin_sc[halo:halo + t_blk, 0:N] = bm_ref[...]
    bcin_sc[halo:halo + t_blk, N:2 * N] = cm_ref[...]

    def conv(in_sc, rows_ext, w_ref, b_ref):
        xe = in_sc[rows_ext, :]
        sh = _dot(shift_ref[...], xe)
        acc = b_ref[...] + w_ref[K - 1:K, :] * xe[halo:halo + L, :].astype(F32)
        for k in range(K - 1):
            acc = acc + w_ref[k:k + 1, :] * sh[k * L:(k + 1) * L, :]
        return _silu(acc)

    li = lax.broadcasted_iota(jnp.int32, (L, L), 0)
    si = lax.broadcasted_iota(jnp.int32, (L, L), 1)
    causal = li >= si
    lane = lax.broadcasted_iota(jnp.int32, (L, 2 * P), 1)
    lo_half = lane < P

    def chunk(c, carry):
        rows = pl.ds(pl.multiple_of(c * L, L), L)
        rows_ext = pl.ds(pl.multiple_of(c * L, L), halo + L)
        xc = conv(xin_sc, rows_ext, cwx_ref, cbx_ref)
        bc_c = conv(bcin_sc, rows_ext, cwbc_ref, cbbc_ref).astype(BF16)
        b_c = bc_c[:, 0:N]
        c_c = bc_c[:, N:2 * N]
        e = e_ref[...]
        dt_e = _dot(dtb_ref[rows, :], e)
        eacs_e = _dot(eacs_ref[rows, :], e)
        toend_e = _dot(toend_ref[rows, :], e)
        xdt = xc * dt_e
        xdt_b = xdt.astype(BF16)
        xw_b = (xdt * toend_e).astype(BF16)
        cb_b = lax.dot_general(c_c, b_c, (((1,), (1,)), ((), ())), preferred_element_type=F32).astype(BF16)
        acs = acs_ref[rows, :]
        acst = acst_ref[c]
        st = state_sc[...]
        y = _dot(c_c, st.astype(BF16)) * eacs_e
        pieces = []
        for jp in range(r // 2):
            xp = xdt_b[:, jp * 2 * P:(jp + 1) * 2 * P]
            acc = None
            for u in range(2):
                j = 2 * jp + u
                colb = jnp.broadcast_to(acs[:, j:j + 1], (L, L))
                rowb = jnp.broadcast_to(acst[j:j + 1, :], (L, L))
                seg = (colb - rowb).astype(BF16)
                m_h = cb_b * jnp.exp2(jnp.where(causal, seg, -jnp.inf))
                xm = jnp.where(lo_half if u == 0 else jnp.logical_not(lo_half), xp, jnp.zeros_like(xp))
                d = _dot(m_h, xm)
                acc = d if acc is None else acc + d
            pieces.append(acc)
        y = y + jnp.concatenate(pieces, axis=1)
        upd = lax.dot_general(b_c, xw_b, (((0,), (0,)), ((), ())), preferred_element_type=F32)
        state_sc[...] = st * decend_ref[c] + upd
        y = y + dskip_ref[...] * xc
        yg = y * _silu(z_ref[rows, :].astype(F32))
        ms = jnp.mean(yg * yg, axis=-1, keepdims=True)
        o_ref[rows, :] = (yg * lax.rsqrt(ms + NORM_EPS) * nw_ref[...]).astype(o_ref.dtype)
        return carry

    lax.fori_loop(0, t_blk // L, chunk, 0)
    xin_sc[0:halo, :] = xin_sc[t_blk:t_blk + halo, :]
    bcin_sc[0:halo, :] = bcin_sc[t_blk:t_blk + halo, :]


def _ssd(proj, offs, pre, conv_w, conv_b, d_skip, ssd_norm_w, batch, seq, t_blk):
    m = proj.shape[0]
    d_inner = ssd_norm_w.shape[0]
    nh = d_skip.shape[0]
    G = SSD_N_GROUPS
    N = SSD_D_STATE
    L = SSD_CHUNK
    r = nh // G
    gw = r * SSD_HEAD_DIM
    assert gw % LANES == 0 and r % 2 == 0 and r % SUBLANES == 0
    dtb, eacs, toend, acsrot, acst, decend = pre
    nt = seq // t_blk
    ncb = t_blk // L
    e_all = (jnp.arange(LANES)[None, :, None]
             == (jnp.arange(G)[:, None, None] * r + jnp.arange(gw)[None, None, :] // SSD_HEAD_DIM)).astype(BF16)
    dskip_e = jnp.repeat(d_skip.astype(F32), SSD_HEAD_DIM).reshape(1, d_inner)

    def bc_pairs(a):
        b_part = a[:, d_inner:d_inner + G * N].reshape(-1, G, 1, N)
        c_part = a[:, d_inner + G * N:].reshape(-1, G, 1, N)
        return jnp.concatenate([b_part, c_part], axis=2).reshape(-1, G * 2 * N)

    cb2 = conv_b.reshape(1, -1)
    cwx, cbx = conv_w[:, :d_inner], cb2[:, :d_inner]
    cwbc, cbbc = bc_pairs(conv_w), bc_pairs(cb2)
    kw = SSD_CONV_WIDTH
    tt = jnp.arange((kw - 1) * L)
    shift = (jnp.arange(CONV_HALO + L)[None, :]
             == (CONV_HALO + tt % L - (kw - 1) + tt // L)[:, None]).astype(BF16)
    for o in (offs["xs"], offs["z"]):
        assert o % gw == 0
    row = lambda b, g, t: b * nt + t
    in_specs = [
        pl.BlockSpec((t_blk, gw), lambda b, g, t: (row(b, g, t), offs["xs"] // gw + g)),
        pl.BlockSpec((t_blk, N), lambda b, g, t: (row(b, g, t), offs["bm"] // N + g)),
        pl.BlockSpec((t_blk, N), lambda b, g, t: (row(b, g, t), offs["cm"] // N + g)),
        pl.BlockSpec((t_blk, gw), lambda b, g, t: (row(b, g, t), offs["z"] // gw + g)),
        pl.BlockSpec((t_blk, LANES), lambda b, g, t: (row(b, g, t), 0)),
        pl.BlockSpec((t_blk, LANES), lambda b, g, t: (row(b, g, t), 0)),
        pl.BlockSpec((t_blk, LANES), lambda b, g, t: (row(b, g, t), 0)),
        pl.BlockSpec((None, t_blk, LANES), lambda b, g, t: (g, row(b, g, t), 0)),
        pl.BlockSpec((ncb, r, L), lambda b, g, t: (row(b, g, t), g, 0)),
        pl.BlockSpec((ncb, 1, gw), lambda b, g, t: (row(b, g, t), 0, g)),
        pl.BlockSpec((SSD_CONV_WIDTH, gw), lambda b, g, t: (0, g)),
        pl.BlockSpec((1, gw), lambda b, g, t: (0, g)),
        pl.BlockSpec((SSD_CONV_WIDTH, 2 * N), lambda b, g, t: (0, g)),
        pl.BlockSpec((1, 2 * N), lambda b, g, t: (0, g)),
        pl.BlockSpec((1, gw), lambda b, g, t: (0, g)),
        pl.BlockSpec((1, gw), lambda b, g, t: (0, g)),
        pl.BlockSpec((None, LANES, gw), lambda b, g, t: (g, 0, 0)),
        pl.BlockSpec(((kw - 1) * L, CONV_HALO + L), lambda b, g, t: (0, 0)),
    ]
    return pl.pallas_call(
        functools.partial(_ssd_kernel, t_blk=t_blk, r=r),
        grid=(batch, G, nt),
        in_specs=in_specs,
        out_specs=pl.BlockSpec((t_blk, gw), lambda b, g, t: (row(b, g, t), g)),
        out_shape=jax.ShapeDtypeStruct((m, d_inner), BF16),
        scratch_shapes=[
            pltpu.VMEM((t_blk + CONV_HALO, gw), BF16),
            pltpu.VMEM((t_blk + CONV_HALO, 2 * N), BF16),
            pltpu.VMEM((N, gw), F32),
        ],
        compiler_params=_cparams(("parallel", "parallel", "arbitrary")),
        name="ssd_scan",
    )(proj, proj, proj, proj, dtb, eacs, toend, acsrot, acst, decend,
      cwx, cbx, cwbc, cbbc, dskip_e, ssd_norm_w.reshape(1, d_inner), e_all, shift)


def _flash_kernel(q_ref, kn_ref, kr_ref, v_ref, g_ref, o_ref, vt_sc, sa_sc, sb_sc, xa_sc, xb_sc,
                  qt0_sc, qt1_sc, m0_sc, m1_sc, l0_sc, l1_sc, acc0_sc, acc1_sc, *, tq, tk, seq):
    n_sub = tq // tk
    nq = seq // tq
    tqs = min(2 * LANES, tq)
    nch = tq // tqs
    assert n_sub == 2 and nq % 2 == 0 and nq >= 4
    buf_a, buf_b = (sa_sc, xa_sc), (sb_sc, xb_sc)
    state = ((qt0_sc, m0_sc, l0_sc, acc0_sc), (qt1_sc, m1_sc, l1_sc, acc1_sc))

    for c in range(seq // tk):
        vt_sc[c] = v_ref[c * tk:(c + 1) * tk, :].astype(F32).T.astype(BF16)

    def step(nxt_blk, cur_blk):
        nxt, cur = [], []
        if nxt_blk is not None:
            kb_next, bufs_next, diag_next, qt = nxt_blk
            rows = pl.ds(pl.multiple_of(kb_next * tk, tk), tk)
            k = jnp.concatenate([kn_ref[rows, :], kr_ref[rows, :]], axis=1)
            k0 = 0 if diag_next is None else diag_next * tk
            nxt = list(range(k0, tq, tqs))
        if cur_blk is not None:
            kb, bufs, diag, (m_sc, l_sc, acc_sc) = cur_blk
            vt = vt_sc[kb]
            cur = list(range(0 if diag is None else diag * tk, tq, tqs))
        def n_keys(blk_diag, blk_k0, c0):
            return tqs if (blk_diag is not None and c0 == blk_k0) else tk

        for i in range(max(len(nxt), len(cur))):
            if i < len(nxt):
                c0 = nxt[i]
                nk = n_keys(diag_next, k0, c0)
                st = _dot(k[0:nk], qt[c0 // tqs])
                if diag_next is not None and c0 < k0 + tk:
                    ki = lax.broadcasted_iota(jnp.int32, (nk, tqs), 0) + k0
                    qj = lax.broadcasted_iota(jnp.int32, (nk, tqs), 1) + c0
                    st = jnp.where(ki <= qj, st, -jnp.inf)
                bufs_next[0][c0 // tqs, 0:nk] = st
                bufs_next[1][c0 // tqs] = jnp.max(st, axis=0, keepdims=True)
            if i < len(cur):
                ci = cur[i] // tqs
                nk = n_keys(diag, 0 if diag is None else diag * tk, cur[i])
                st = bufs[0][ci, 0:nk]
                m_prev = m_sc[ci]
                m_new = jnp.maximum(m_prev, bufs[1][ci])
                alpha = jnp.exp2(m_prev - m_new)
                p = jnp.exp2(st - m_new)
                l_sc[ci] = alpha * l_sc[ci] + jnp.sum(p, axis=0, keepdims=True)
                acc_sc[ci] = acc_sc[ci] * alpha + _dot(vt[:, 0:nk], p.astype(BF16))
                m_sc[ci] = m_new

    def init(qi, par):
        qt, m_sc, l_sc, acc_sc = state[par]
        for ci in range(nch):
            qsub = pl.ds(pl.multiple_of(qi * tq + ci * tqs, tqs), tqs)
            qt[ci] = q_ref[qsub, :].astype(F32).T.astype(BF16)
        m_sc[...] = jnp.full(m_sc.shape, -jnp.inf, F32)
        l_sc[...] = jnp.zeros(l_sc.shape, F32)
        acc_sc[...] = jnp.zeros(acc_sc.shape, F32)

    def finalize(qi, par):
        _, _, l_sc, acc_sc = state[par]
        for ci in range(nch):
            qsub = pl.ds(pl.multiple_of(qi * tq + ci * tqs, tqs), tqs)
            o = (acc_sc[ci] / l_sc[ci]).T
            o_ref[qsub, :] = (o * _silu(g_ref[qsub, :].astype(F32))).astype(o_ref.dtype)

    def body(qi, par):
        qt, stats = state[par][0], state[par][1:]
        d0 = qi * n_sub

        def pair(j):
            step((2 * j + 1, buf_b, None, qt), (2 * j, buf_a, None, stats))
            step((2 * j + 2, buf_a, None, qt), (2 * j + 1, buf_b, None, stats))

        def quad(i, c):
            pair(2 * i)
            pair(2 * i + 1)
            return c

        n_pairs = qi - 1
        lax.fori_loop(0, n_pairs // 2, quad, 0)

        @pl.when(n_pairs % 2 == 1)
        def _():
            pair(n_pairs - 1)

        step((d0 - 1, buf_b, None, qt), (d0 - 2, buf_a, None, stats))
        step((d0, buf_a, 0, qt), (d0 - 1, buf_b, None, stats))
        step((d0 + 1, buf_b, 1, qt), (d0, buf_a, 0, stats))

    def handover(qi, par):
        init(qi + 1, 1 - par)
        step((0, buf_a, None, state[1 - par][0]), (qi * n_sub + 1, buf_b, 1, state[par][1:]))
        finalize(qi, par)

    init(0, 0)
    step((0, buf_a, 0, state[0][0]), None)
    step((1, buf_b, 1, state[0][0]), (0, buf_a, 0, state[0][1:]))
    handover(0, 0)

    def two_blocks(i, carry):
        qi = 2 * i + 1
        body(qi, 1)
        handover(qi, 1)
        body(qi + 1, 0)
        handover(qi + 1, 0)
        return carry

    lax.fori_loop(0, nq // 2 - 1, two_blocks, 0)
    body(nq - 1, 1)
    step(None, ((nq - 1) * n_sub + 1, buf_b, 1, state[1][1:]))
    finalize(nq - 1, 1)


def _flash(q, kv, kr, proj, gate_off, batch, seq, n_heads, tq, tk):
    m = q.shape[0]
    dq = 2 * LANES
    tqs = min(2 * LANES, tq)
    nch = tq // tqs
    assert gate_off % LANES == 0
    return pl.pallas_call(
        functools.partial(_flash_kernel, tq=tq, tk=tk, seq=seq),
        grid=(batch, n_heads),
        in_specs=[
            pl.BlockSpec((seq, dq), lambda b, h: (b, h)),
            pl.BlockSpec((seq, LANES), lambda b, h: (b, 2 * h)),
            pl.BlockSpec((seq, LANES), lambda b, h: (b, 0)),
            pl.BlockSpec((seq, LANES), lambda b, h: (b, 2 * h + 1)),
            pl.BlockSpec((seq, LANES), lambda b, h: (b, gate_off // LANES + h)),
        ],
        out_specs=pl.BlockSpec((seq, LANES), lambda b, h: (b, h)),
        out_shape=jax.ShapeDtypeStruct((m, n_heads * MLA_V_DIM), BF16),
        scratch_shapes=[pltpu.VMEM((seq // tk, MLA_V_DIM, tk), BF16),
                        pltpu.VMEM((nch, tk, tqs), F32), pltpu.VMEM((nch, tk, tqs), F32),
                        pltpu.VMEM((nch, 1, tqs), F32), pltpu.VMEM((nch, 1, tqs), F32)]
                       + [pltpu.VMEM((nch, dq, tqs), BF16)] * 2 + [pltpu.VMEM((nch, 1, tqs), F32)] * 4
                       + [pltpu.VMEM((nch, MLA_V_DIM, tqs), F32)] * 2,
        compiler_params=_cparams(("parallel", "parallel")),
        name="mla_flash",
    )(q, kv, kr, kv, proj)


def _ssd_out_kernel(a_ref, w_ref, g_ref, o_ref):
    y = _dot(a_ref[...], w_ref[...])
    o_ref[...] = (jax.nn.sigmoid(g_ref[...].astype(F32)) * y).astype(o_ref.dtype)


def _ssd_out(ys, w_ssd, l, proj, gate_off, tm, tn):
    m, k1 = ys.shape
    d = w_ssd.shape[2]
    assert gate_off % tn == 0
    return pl.pallas_call(
        _ssd_out_kernel,
        grid=(m // tm, d // tn),
        in_specs=[
            pl.BlockSpec((tm, k1), lambda i, j: (i, 0)),
            pl.BlockSpec((None, k1, tn), lambda i, j: (l, 0, j)),
            pl.BlockSpec((tm, tn), lambda i, j: (i, gate_off // tn + j)),
        ],
        out_specs=pl.BlockSpec((tm, tn), lambda i, j: (i, j)),
        out_shape=jax.ShapeDtypeStruct((m, d), BF16),
        compiler_params=_cparams(("parallel", "parallel")),
        name="ssd_out",
    )(ys, w_ssd, proj)


def _mla_out_kernel(a_ref, w_ref, g_ref, p_ref, o_ref):
    y = _dot(a_ref[...], w_ref[...])
    o_ref[...] = (p_ref[...].astype(F32) + jax.nn.sigmoid(g_ref[...].astype(F32)) * y).astype(o_ref.dtype)


def _mla_out(og, w_mla, l, proj, gate_off, part, tm, tn):
    m, k2 = og.shape
    d = w_mla.shape[2]
    assert gate_off % tn == 0
    return pl.pallas_call(
        _mla_out_kernel,
        grid=(m // tm, d // tn),
        in_specs=[
            pl.BlockSpec((tm, k2), lambda i, j: (i, 0)),
            pl.BlockSpec((None, k2, tn), lambda i, j: (l, 0, j)),
            pl.BlockSpec((tm, tn), lambda i, j: (i, gate_off // tn + j)),
            pl.BlockSpec((tm, tn), lambda i, j: (i, j)),
        ],
        out_specs=pl.BlockSpec((tm, tn), lambda i, j: (i, j)),
        out_shape=jax.ShapeDtypeStruct((m, d), BF16),
        compiler_params=_cparams(("parallel", "parallel")),
        name="mla_out",
    )(og, w_mla, proj, part)


def _out_kernel(a_ref, w_ref, x_ref, pw_ref, gate_ref, o_ref):
    y = _dot(a_ref[...], w_ref[...])
    yn = y * lax.rsqrt(jnp.mean(y * y, axis=-1, keepdims=True) + NORM_EPS) * pw_ref[...]
    o_ref[...] = x_ref[...] + gate_ref[...] * yn


def _out_proj(merged, w_out, l, x2, post_w, mod4, seq, tm):
    m, d = x2.shape
    per_b = seq // tm
    return pl.pallas_call(
        _out_kernel,
        grid=(m // tm,),
        in_specs=[
            pl.BlockSpec((tm, d), lambda i: (i, 0)),
            pl.BlockSpec((None, d, d), lambda i: (l, 0, 0), pipeline_mode=pl.Buffered(1)),
            pl.BlockSpec((tm, d), lambda i: (i, 0)),
            pl.BlockSpec((1, d), lambda i: (0, 0)),
            pl.BlockSpec((None, None, 1, d), lambda i: (i // per_b, 2, 0, 0)),
        ],
        out_specs=pl.BlockSpec((tm, d), lambda i: (i, 0)),
        out_shape=jax.ShapeDtypeStruct((m, d), F32),
        compiler_params=pltpu.CompilerParams(dimension_semantics=("parallel",),
                                             vmem_limit_bytes=OUT_PROJ_VMEM_LIMIT),
        name="out_proj",
    )(merged, w_out, x2, post_w.reshape(1, d), mod4)


def _proj_layout(d, d_inner, nh, n_mla_heads, q_lora, kv_lora):
    gn = SSD_N_GROUPS * SSD_D_STATE
    widths = [("z", d_inner), ("xs", d_inner), ("bm", gn), ("cm", gn),
              ("gate", n_mla_heads * MLA_V_DIM), ("merge", 2 * d), ("cq", q_lora), ("ckv", kv_lora),
              ("kr", 2 * MLA_QK_ROPE), ("dt", nh)]
    offs, o = {}, 0
    for name, w in widths:
        offs[name] = o
        o += w
    return offs, o


def _cast_kernel(w_ref, o_ref):
    o_ref[...] = w_ref[...].astype(o_ref.dtype)


def _cast_bf16(w3):
    depth, k, n = w3.shape
    tr = max(2 * SUBLANES, min(k, CAST_BLOCK_BYTES // (n * 4)))
    assert k % tr == 0
    return pl.pallas_call(
        _cast_kernel,
        grid=(depth, k // tr),
        in_specs=[pl.BlockSpec((None, tr, n), lambda l, r: (l, r, 0))],
        out_specs=pl.BlockSpec((None, tr, n), lambda l, r: (l, r, 0)),
        out_shape=jax.ShapeDtypeStruct((depth, k, n), BF16),
        compiler_params=_cparams(("parallel", "parallel")),
        name="cast_bf16",
    )(w3)


def _w_in_kernel(src_ref, mode_ref, *refs, n_chunks):
    o_ref = refs[n_chunks]
    j = pl.program_id(1)
    half = MLA_QK_ROPE // 2
    parts = []
    for c in range(n_chunks):
        blk = refs[c][...]
        mode = mode_ref[j * n_chunks + c]
        rot = jnp.concatenate([-blk[half:], blk[:half]], axis=0)
        v = jnp.where(mode == W_IN_ROTATE, rot, blk)
        parts.append(jnp.where(mode == W_IN_ZERO, 0.0, v))
    o_ref[...] = jnp.concatenate(parts, axis=0).T.astype(BF16)


def _prep_w_in(w_in3, offs, n_tot, n_pad, d, d_inner, nh, n_mla_heads, q_lora, kv_lora):
    depth, k, n_src = w_in3.shape
    gn = SSD_N_GROUPS * SSD_D_STATE
    conv_dim = d_inner + 2 * gn
    s_xbc = d_inner
    s_dt = s_xbc + conv_dim
    s_cq = s_dt + nh
    s_ckv = s_cq + q_lora
    s_kr = s_ckv + kv_lora
    s_gate = s_kr + MLA_QK_ROPE
    s_merge = s_gate + n_mla_heads * MLA_V_DIM
    assert s_merge + 2 * d == n_src and MLA_QK_ROPE == W_IN_CHUNK
    segments = (
        (offs["z"], 0, d_inner + conv_dim, W_IN_COPY),
        (offs["gate"], s_gate, n_mla_heads * MLA_V_DIM + 2 * d, W_IN_COPY),
        (offs["cq"], s_cq, q_lora + kv_lora, W_IN_COPY),
        (offs["kr"], s_kr, MLA_QK_ROPE, W_IN_COPY),
        (offs["kr"] + MLA_QK_ROPE, s_kr, MLA_QK_ROPE, W_IN_ROTATE),
        (offs["dt"], s_dt, nh, W_IN_COPY),
    )
    n_dst = n_pad // W_IN_CHUNK
    src_tab = [0] * n_dst
    mode_tab = [W_IN_ZERO] * n_dst
    for dst, src, width, mode in segments:
        assert dst % W_IN_CHUNK == 0 and src % W_IN_CHUNK == 0 and width % W_IN_CHUNK == 0
        for t in range(width // W_IN_CHUNK):
            src_tab[dst // W_IN_CHUNK + t] = src // W_IN_CHUNK + t
            mode_tab[dst // W_IN_CHUNK + t] = mode
    n_chunks = W_IN_COLS // W_IN_CHUNK
    w_t = jnp.swapaxes(w_in3, 1, 2)

    def chunk_spec(c):
        return pl.BlockSpec((None, W_IN_CHUNK, k), lambda l, j, src, mode: (l, src[j * n_chunks + c], 0))

    return pl.pallas_call(
        functools.partial(_w_in_kernel, n_chunks=n_chunks),
        grid_spec=pltpu.PrefetchScalarGridSpec(
            num_scalar_prefetch=2,
            grid=(depth, n_pad // W_IN_COLS),
            in_specs=[chunk_spec(c) for c in range(n_chunks)],
            out_specs=pl.BlockSpec((None, k, W_IN_COLS), lambda l, j, src, mode: (l, 0, j)),
        ),
        out_shape=jax.ShapeDtypeStruct((depth, k, n_pad), BF16),
        compiler_params=_cparams(("parallel", "parallel")),
        name="w_in_prep",
    )(jnp.asarray(src_tab, jnp.int32), jnp.asarray(mode_tab, jnp.int32), *([w_t] * n_chunks))


def _w_q_kernel(w_ref, o_ref, *, n_heads):
    qk = MLA_QK_NOPE + MLA_QK_ROPE
    half = MLA_QK_ROPE // 2
    scale = qk ** -0.5 * LOG2_E

    def load(src, width):
        lead = src % LANES
        v = w_ref[:, src - lead:src + width]
        return v[:, lead:] if lead else v

    for h in range(n_heads):
        nope = load(h * qk, MLA_QK_NOPE)
        rope = load(h * qk + MLA_QK_NOPE, MLA_QK_ROPE)
        blk = jnp.concatenate([nope, rope, -rope[:, half:], rope[:, :half]], axis=1)
        o_ref[:, h * 2 * LANES:(h + 1) * 2 * LANES] = (blk * scale).astype(BF16)


def _prep_w_q(w_q_up3, n_mla_heads):
    depth, k, n = w_q_up3.shape
    tr = min(k, 256)
    return pl.pallas_call(
        functools.partial(_w_q_kernel, n_heads=n_mla_heads),
        grid=(depth, k // tr),
        in_specs=[pl.BlockSpec((None, tr, n), lambda l, r: (l, r, 0))],
        out_specs=pl.BlockSpec((None, tr, n_mla_heads * 2 * LANES), lambda l, r: (l, r, 0)),
        out_shape=jax.ShapeDtypeStruct((depth, k, n_mla_heads * 2 * LANES), BF16),
        compiler_params=_cparams(("parallel", "parallel")),
        name="w_q_prep",
    )(w_q_up3)


def _round_up(v, mult):
    return (v + mult - 1) // mult * mult


def _tile_plan(m, seq, d, n_mla):
    qkv_n = n_mla * 2 * LANES
    tq = min(1024, seq)
    return dict(
        tm=min(1024, m),
        proj_tn=1024,
        q_tn=min(2048, qkv_n), kv_tn=min(4096, qkv_n),
        ssd_pre_rows=min(1024, seq), ssd_rows=min(1024, seq),
        flash_tq=tq, flash_tk=min(512, tq),
        ssd_out_tn=min(512, d), mla_out_tn=min(1024, d),
        out_tm=min(256, seq),
    )


def kernel(x, c, positions, ada_w, ada_b, pre_norm_w, post_norm_w, w_in, conv_w, conv_b, dt_bias,
           a_log, d_skip, ssd_norm_w, q_norm_w, w_q_up, kv_norm_w, w_kv_up, w_ssd_proj, w_mla_proj,
           w_out):
    batch, seq, d = x.shape
    depth = ada_w.shape[0]
    m = batch * seq
    d_inner = ssd_norm_w.shape[1]
    nh = dt_bias.shape[1]
    q_lora = q_norm_w.shape[1]
    kv_lora = kv_norm_w.shape[1]
    n_mla = w_mla_proj.shape[1] // MLA_V_DIM
    offs, n_tot = _proj_layout(d, d_inner, nh, n_mla, q_lora, kv_lora)
    t = _tile_plan(m, seq, d, n_mla)
    n_pad = _round_up(n_tot, t["proj_tn"])

    c_pad = jnp.zeros((SUBLANES, d), F32).at[:batch].set(c)
    mod = _ada(c_pad, ada_w, ada_b)
    cos, sin = _rope_tables(positions)

    w_in_b = _prep_w_in(w_in, offs, n_tot, n_pad, d, d_inner, nh, n_mla, q_lora, kv_lora)
    w_q_b = _prep_w_q(w_q_up, n_mla)
    w_kv_b = _cast_bf16(w_kv_up)
    out_side = (w_ssd_proj, w_mla_proj, w_out)

    x2 = x.reshape(m, d)
    for l in range(depth):
        mod4 = mod[l, :batch].reshape(batch, 3, 1, d)
        h = _prenorm(x2, pre_norm_w[l], mod4, seq)
        if l == 0:
            steps = (m // t["tm"]) * (n_pad // t["proj_tn"])
            ride = [_rides_along(w, steps) for w in out_side]
            proj, cast = _matmul(h, w_in_b, l, BF16, t["tm"], t["proj_tn"],
                                 side_casts=[w.reshape(-1, w.shape[-1]) for w, r in zip(out_side, ride) if r])
            cast = iter(cast)
            w_ssd_b, w_mla_b, w_out_b = (next(cast).reshape(w.shape) if r else _cast_bf16(w)
                                         for w, r in zip(out_side, ride))
        else:
            proj, _ = _matmul(h, w_in_b, l, BF16, t["tm"], t["proj_tn"])

        pre = _ssd_pre(proj, offs["dt"], dt_bias[l], a_log[l], d_inner, tp=t["ssd_pre_rows"])
        ys = _ssd(proj, offs, pre, conv_w[l], conv_b[l], d_skip[l], ssd_norm_w[l], batch, seq,
                  t_blk=t["ssd_rows"])

        q = _norm_matmul(proj, offs["cq"], q_lora, q_norm_w[l], w_q_b, l,
                         t["tm"], t["q_tn"], cos, sin, name="q_up")
        kv = _norm_matmul(proj, offs["ckv"], kv_lora, kv_norm_w[l], w_kv_b, l,
                          t["tm"], t["kv_tn"], name="kv_up")
        kr = _krope(proj, offs["kr"], cos, sin)
        og = _flash(q, kv, kr, proj, offs["gate"], batch, seq, n_mla, t["flash_tq"], t["flash_tk"])

        part = _ssd_out(ys, w_ssd_b, l, proj, offs["merge"], t["tm"], t["ssd_out_tn"])
        merged = _mla_out(og, w_mla_b, l, proj, offs["merge"] + d, part, t["tm"], t["mla_out_tn"])
        x2 = _out_proj(merged, w_out_b, l, x2, post_norm_w[l], mod4, seq, t["out_tm"])
    return x2.reshape(batch, seq, d)
```

```python
import functools
import math

import jax
import jax.numpy as jnp
from jax import lax
from jax.experimental import pallas as pl
from jax.experimental.pallas import tpu as pltpu

SSD_HEAD_DIM = 64
SSD_N_GROUPS = 8
SSD_D_STATE = 128
SSD_CHUNK = 128
SSD_CONV_WIDTH = 4
MLA_QK_NOPE = 128
MLA_QK_ROPE = 64
MLA_V_DIM = 128
ROPE_THETA = 10000.0
NORM_EPS = 1e-6
LOG2_E = math.log2(math.e)

LANES = 128
SUBLANES = 8
CONV_HALO = 16
SSD_CHUNK_UNROLL = 4
VMEM_LIMIT = 56 * 1024 * 1024
OUT_PROJ_VMEM_LIMIT = 58 * 1024 * 1024
CAST_BLOCK_BYTES = 8 * 1024 * 1024
W_IN_CHUNK = 64
W_IN_COLS = 512
W_IN_COPY, W_IN_ROTATE, W_IN_ZERO = 0, 1, 2

F32 = jnp.float32
BF16 = jnp.bfloat16


def _cparams(sem):
    return pltpu.CompilerParams(dimension_semantics=sem, vmem_limit_bytes=VMEM_LIMIT)


def _silu(v):
    h = 0.5 * v
    return h + h * jnp.tanh(h)


def _split3(v):
    hi = v.astype(BF16)
    r1 = v - hi.astype(F32)
    mid = r1.astype(BF16)
    lo = (r1 - mid.astype(F32)).astype(BF16)
    return hi, mid, lo


def _dot(a, b):
    return jnp.dot(a, b, preferred_element_type=F32)


def _ada_kernel(c_ref, w_ref, b_ref, o_ref):
    @pl.when(pl.program_id(1) == 0)
    def _():
        o_ref[...] = jnp.broadcast_to(b_ref[...], o_ref.shape)

    a = _silu(c_ref[...]).astype(BF16)
    o_ref[...] += _dot(a, w_ref[...].astype(BF16))


def _ada(c_pad, ada_w, ada_b):
    depth, d, n = ada_w.shape
    tk = min(256, d)
    return pl.pallas_call(
        _ada_kernel,
        grid=(depth, d // tk),
        in_specs=[
            pl.BlockSpec((SUBLANES, tk), lambda l, k: (0, k)),
            pl.BlockSpec((None, tk, n), lambda l, k: (l, k, 0)),
            pl.BlockSpec((None, 1, n), lambda l, k: (l, 0, 0)),
        ],
        out_specs=pl.BlockSpec((None, SUBLANES, n), lambda l, k: (l, 0, 0)),
        out_shape=jax.ShapeDtypeStruct((depth, SUBLANES, n), F32),
        compiler_params=_cparams(("parallel", "arbitrary")),
        name="ada_mod",
    )(c_pad, ada_w, ada_b.reshape(depth, 1, n))


def _prenorm_kernel(x_ref, w_ref, shift_ref, scale_ref, o_ref):
    x = x_ref[...]
    y = x * lax.rsqrt(jnp.mean(x * x, axis=-1, keepdims=True) + NORM_EPS)
    o_ref[...] = (y * w_ref[...] * (1.0 + scale_ref[...]) + shift_ref[...]).astype(BF16)


def _prenorm(x2, w, mod4, seq):
    m, d = x2.shape
    tm = min(512, seq)
    per_b = seq // tm
    return pl.pallas_call(
        _prenorm_kernel,
        grid=(m // tm,),
        in_specs=[
            pl.BlockSpec((tm, d), lambda i: (i, 0)),
            pl.BlockSpec((1, d), lambda i: (0, 0)),
            pl.BlockSpec((None, None, 1, d), lambda i: (i // per_b, 0, 0, 0)),
            pl.BlockSpec((None, None, 1, d), lambda i: (i // per_b, 1, 0, 0)),
        ],
        out_specs=pl.BlockSpec((tm, d), lambda i: (i, 0)),
        out_shape=jax.ShapeDtypeStruct((m, d), BF16),
        compiler_params=_cparams(("parallel",)),
        name="prenorm",
    )(x2, w.reshape(1, d), mod4, mod4)


def _mm_kernel(a_ref, w_ref, *rest, n_side):
    o_ref = rest[n_side]
    o_ref[...] = _dot(a_ref[...], w_ref[...]).astype(o_ref.dtype)
    for s in range(n_side):
        rest[n_side + 1 + s][...] = rest[s][...].astype(BF16)


def _rides_along(w3, steps):
    rows = w3.shape[0] * w3.shape[1]
    return rows % steps == 0 and (rows // steps) % (2 * SUBLANES) == 0


def _matmul(a, w3, l, out_dtype, tm, tn, side_casts=()):
    m, k = a.shape
    n = w3.shape[2]
    gi, gj = m // tm, n // tn
    side_specs, side_shapes = [], []
    for w in side_casts:
        rows, cols = w.shape
        spec = pl.BlockSpec((rows // (gi * gj), cols), lambda i, j: (i * gj + j, 0))
        side_specs.append(spec)
        side_shapes.append(jax.ShapeDtypeStruct((rows, cols), BF16))
    res = pl.pallas_call(
        functools.partial(_mm_kernel, n_side=len(side_casts)),
        grid=(gi, gj),
        in_specs=[
            pl.BlockSpec((tm, k), lambda i, j: (i, 0)),
            pl.BlockSpec((None, k, tn), lambda i, j: (l, 0, j)),
        ] + side_specs,
        out_specs=[pl.BlockSpec((tm, tn), lambda i, j: (i, j))] + side_specs,
        out_shape=[jax.ShapeDtypeStruct((m, n), out_dtype)] + side_shapes,
        compiler_params=_cparams(("parallel", "parallel")),
        name="in_proj",
    )(a, w3, *side_casts)
    return res[0], res[1:]


def _rope_block(a, cos, sin):
    return a * cos + pltpu.roll(a, LANES // 2, axis=1) * sin


def _norm_mm_kernel(a_ref, nw_ref, w_ref, *rest, rope, tn):
    if rope:
        cos_ref, sin_ref, o_ref, an_sc = rest
    else:
        o_ref, an_sc = rest

    @pl.when(pl.program_id(1) == 0)
    def _():
        a = a_ref[...].astype(F32)
        y = a * lax.rsqrt(jnp.mean(a * a, axis=-1, keepdims=True) + NORM_EPS)
        an_sc[...] = (y * nw_ref[...]).astype(BF16)

    acc = _dot(an_sc[...], w_ref[...])
    if rope:
        cos = cos_ref[...]
        sin = sin_ref[...]
        for hh in range(tn // (2 * LANES)):
            c0 = hh * 2 * LANES
            o_ref[:, c0:c0 + LANES] = acc[:, c0:c0 + LANES].astype(o_ref.dtype)
            o_ref[:, c0 + LANES:c0 + 2 * LANES] = _rope_block(
                acc[:, c0 + LANES:c0 + 2 * LANES], cos, sin).astype(o_ref.dtype)
    else:
        o_ref[...] = acc.astype(o_ref.dtype)


def _norm_matmul(proj, col_off, k, norm_w, w, l, tm, tn, cos=None, sin=None, name="norm_mm"):
    m = proj.shape[0]
    n = w.shape[2]
    rope = cos is not None
    assert col_off % k == 0
    in_specs = [
        pl.BlockSpec((tm, k), lambda i, j: (i, col_off // k)),
        pl.BlockSpec((1, k), lambda i, j: (0, 0)),
        pl.BlockSpec((None, k, tn), lambda i, j: (l, 0, j)),
    ]
    args = [proj, norm_w.reshape(1, k), w]
    if rope:
        in_specs += [pl.BlockSpec((tm, LANES), lambda i, j: (i, 0))] * 2
        args += [cos, sin]
    return pl.pallas_call(
        functools.partial(_norm_mm_kernel, rope=rope, tn=tn),
        grid=(m // tm, n // tn),
        in_specs=in_specs,
        out_specs=pl.BlockSpec((tm, tn), lambda i, j: (i, j)),
        out_shape=jax.ShapeDtypeStruct((m, n), BF16),
        scratch_shapes=[pltpu.VMEM((tm, k), BF16)],
        compiler_params=_cparams(("parallel", "arbitrary")),
        name=name,
    )(*args)


def _rope_tab_kernel(pos_ref, freq_ref, cos_ref, sin_ref):
    ang = pos_ref[...].astype(F32) * freq_ref[...]
    lane = lax.broadcasted_iota(jnp.int32, ang.shape, 1)
    keep = lane < MLA_QK_ROPE
    cos_ref[...] = jnp.where(keep, jnp.cos(ang), 0.0)
    sin_ref[...] = jnp.where(keep, jnp.sin(ang), 0.0)


def _rope_tables(positions):
    m = positions.size
    tm = min(1024, m)
    half = MLA_QK_ROPE // 2
    inv_freq = ROPE_THETA ** (-(jnp.arange(0, MLA_QK_ROPE, 2, dtype=F32) / MLA_QK_ROPE))
    freq = jnp.concatenate([inv_freq, inv_freq, jnp.zeros((LANES - 2 * half,), F32)]).reshape(1, LANES)
    return pl.pallas_call(
        _rope_tab_kernel,
        grid=(m // tm,),
        in_specs=[pl.BlockSpec((tm, 1), lambda i: (i, 0)),
                  pl.BlockSpec((1, LANES), lambda i: (0, 0))],
        out_specs=[pl.BlockSpec((tm, LANES), lambda i: (i, 0))] * 2,
        out_shape=[jax.ShapeDtypeStruct((m, LANES), F32)] * 2,
        compiler_params=_cparams(("parallel",)),
        name="rope_tables",
    )(positions.reshape(m, 1), freq)


def _krope_kernel(a_ref, cos_ref, sin_ref, o_ref):
    o_ref[...] = _rope_block(a_ref[...].astype(F32), cos_ref[...], sin_ref[...]).astype(BF16)


def _krope(proj, col_off, cos, sin):
    m = proj.shape[0]
    tm = min(1024, m)
    return pl.pallas_call(
        _krope_kernel,
        grid=(m // tm,),
        in_specs=[pl.BlockSpec((tm, LANES), lambda i: (i, col_off // LANES)),
                  pl.BlockSpec((tm, LANES), lambda i: (i, 0)),
                  pl.BlockSpec((tm, LANES), lambda i: (i, 0))],
        out_specs=pl.BlockSpec((tm, LANES), lambda i: (i, 0)),
        out_shape=jax.ShapeDtypeStruct((m, LANES), BF16),
        compiler_params=_cparams(("parallel",)),
        name="k_rope",
    )(proj, cos, sin)


def _ssd_pre_kernel(dt_ref, bias_ref, alog_ref, tri_ref, efull_ref,
                    dtb_ref, eacs_ref, toend_ref, acsrot_ref, acst_ref, decend_ref, *, tp, r):
    L = SSD_CHUNK
    dt = jax.nn.softplus(dt_ref[...].astype(F32) + bias_ref[...])
    dtb_ref[...] = dt.astype(BF16)
    a = dt * (-jnp.exp(alog_ref[...])) * LOG2_E
    tri = tri_ref[...]
    lasts = []
    for c in range(tp // L):
        rows = slice(c * L, (c + 1) * L)
        hi, mid, lo = _split3(a[rows])
        acs = _dot(tri, hi) + _dot(tri, mid) + _dot(tri, lo)
        last = acs[L - 1:L, :]
        lasts.append(last)
        eacs_ref[rows, :] = jnp.exp2(acs).astype(BF16)
        toend_ref[rows, :] = jnp.exp2(last - acs).astype(BF16)
        acst_ref[c] = acs.T
        for g in range(SSD_N_GROUPS):
            shift = (LANES - g * r) % LANES
            acsrot_ref[g, rows, :] = acs if shift == 0 else pltpu.roll(acs, shift, axis=1)
    n_c = tp // L
    pad = [jnp.zeros((SUBLANES - n_c % SUBLANES, LANES), F32)] if n_c % SUBLANES else []
    el = jnp.exp2(jnp.concatenate(lasts + pad, axis=0))
    hi, mid, lo = _split3(el)
    e = efull_ref[...]
    dec = _dot(hi, e) + _dot(mid, e) + _dot(lo, e)
    for c in range(n_c):
        decend_ref[c] = dec[c:c + 1, :]


def _ssd_pre(proj, col_off, dt_bias, a_log, d_inner, tp):
    m = proj.shape[0]
    nh = dt_bias.shape[0]
    assert nh == LANES and col_off % LANES == 0
    r = nh // SSD_N_GROUPS
    L = SSD_CHUNK
    tri = (jnp.arange(L)[:, None] >= jnp.arange(L)[None, :]).astype(BF16)
    efull = (jnp.arange(nh)[:, None] == (jnp.arange(d_inner)[None, :] // SSD_HEAD_DIM)).astype(BF16)
    nct = m // L
    return pl.pallas_call(
        functools.partial(_ssd_pre_kernel, tp=tp, r=r),
        grid=(m // tp,),
        in_specs=[
            pl.BlockSpec((tp, LANES), lambda i: (i, col_off // LANES)),
            pl.BlockSpec((1, LANES), lambda i: (0, 0)),
            pl.BlockSpec((1, LANES), lambda i: (0, 0)),
            pl.BlockSpec((L, L), lambda i: (0, 0)),
            pl.BlockSpec((nh, d_inner), lambda i: (0, 0)),
        ],
        out_specs=[
            pl.BlockSpec((tp, LANES), lambda i: (i, 0)),
            pl.BlockSpec((tp, LANES), lambda i: (i, 0)),
            pl.BlockSpec((tp, LANES), lambda i: (i, 0)),
            pl.BlockSpec((SSD_N_GROUPS, tp, LANES), lambda i: (0, i, 0)),
            pl.BlockSpec((tp // L, LANES, L), lambda i: (i, 0, 0)),
            pl.BlockSpec((tp // L, 1, d_inner), lambda i: (i, 0, 0)),
        ],
        out_shape=[
            jax.ShapeDtypeStruct((m, LANES), BF16),
            jax.ShapeDtypeStruct((m, LANES), BF16),
            jax.ShapeDtypeStruct((m, LANES), BF16),
            jax.ShapeDtypeStruct((SSD_N_GROUPS, m, LANES), F32),
            jax.ShapeDtypeStruct((nct, LANES, L), F32),
            jax.ShapeDtypeStruct((nct, 1, d_inner), F32),
        ],
        compiler_params=_cparams(("parallel",)),
        name="ssd_pre",
    )(proj, dt_bias.reshape(1, nh), a_log.reshape(1, nh), tri, efull)


def _ssd_kernel(xs_ref, bm_ref, cm_ref, z_ref, dtb_ref, eacs_ref, toend_ref, acs_ref, acst_ref,
                decend_ref, cwx_ref, cbx_ref, cwbc_ref, cbbc_ref, dskip_ref,
                nw_ref, e_ref, shift_ref, o_ref,
                xin_sc, bcin_sc, state_sc, *, t_blk, r):
    L = SSD_CHUNK
    K = SSD_CONV_WIDTH
    P = SSD_HEAD_DIM
    N = SSD_D_STATE
    gw = r * P
    halo = CONV_HALO

    @pl.when(pl.program_id(2) == 0)
    def _():
        state_sc[...] = jnp.zeros_like(state_sc)
        xin_sc[0:halo, :] = jnp.zeros((halo, gw), BF16)
        bcin_sc[0:halo, :] = jnp.zeros((halo, 2 * N), BF16)

    xin_sc[halo:halo + t_blk, :] = xs_ref[...]
    bcin_sc[halo:halo + t_blk, 0:N] = bm_ref[...]
    bcin_sc[halo:halo + t_blk, N:2 * N] = cm_ref[...]

    def conv(in_sc, rows_ext, w_ref, b_ref):
        xe = in_sc[rows_ext, :]
        sh = _dot(shift_ref[...], xe)
        acc = b_ref[...] + w_ref[K - 1:K, :] * xe[halo:halo + L, :].astype(F32)
        for k in range(K - 1):
            acc = acc + w_ref[k:k + 1, :] * sh[k * L:(k + 1) * L, :]
        return _silu(acc)

    li = lax.broadcasted_iota(jnp.int32, (L, L), 0)
    si = lax.broadcasted_iota(jnp.int32, (L, L), 1)
    causal = li >= si
    lane = lax.broadcasted_iota(jnp.int32, (L, 2 * P), 1)
    lo_half = lane < P

    def chunk(c, carry):
        rows = pl.ds(pl.multiple_of(c * L, L), L)
        rows_ext = pl.ds(pl.multiple_of(c * L, L), halo + L)
        xc = conv(xin_sc, rows_ext, cwx_ref, cbx_ref)
        bc_c = conv(bcin_sc, rows_ext, cwbc_ref, cbbc_ref).astype(BF16)
        b_c = bc_c[:, 0:N]
        c_c = bc_c[:, N:2 * N]
        e = e_ref[...]
        dt_e = _dot(dtb_ref[rows, :], e)
        eacs_e = _dot(eacs_ref[rows, :], e)
        toend_e = _dot(toend_ref[rows, :], e)
        xdt = xc * dt_e
        xdt_b = xdt.astype(BF16)
        xw_b = (xdt * toend_e).astype(BF16)
        cb_b = lax.dot_general(c_c, b_c, (((1,), (1,)), ((), ())), preferred_element_type=F32).astype(BF16)
        acs = acs_ref[rows, :]
        acst = acst_ref[c]
        st = state_sc[...]
        y = _dot(c_c, st.astype(BF16)) * eacs_e
        pieces = []
        for jp in range(r // 2):
            xp = xdt_b[:, jp * 2 * P:(jp + 1) * 2 * P]
            acc = None
            for u in range(2):
                j = 2 * jp + u
                colb = jnp.broadcast_to(acs[:, j:j + 1], (L, L))
                rowb = jnp.broadcast_to(acst[j:j + 1, :], (L, L))
                seg = (colb - rowb).astype(BF16)
                m_h = cb_b * jnp.exp2(jnp.where(causal, seg, -jnp.inf))
                xm = jnp.where(lo_half if u == 0 else jnp.logical_not(lo_half), xp, jnp.zeros_like(xp))
                d = _dot(m_h, xm)
                acc = d if acc is None else acc + d
            pieces.append(acc)
        y = y + jnp.concatenate(pieces, axis=1)
        upd = lax.dot_general(b_c, xw_b, (((0,), (0,)), ((), ())), preferred_element_type=F32)
        state_sc[...] = st * decend_ref[c] + upd
        y = y + dskip_ref[...] * xc
        yg = y * _silu(z_ref[rows, :].astype(F32))
        ms = jnp.mean(yg * yg, axis=-1, keepdims=True)
        o_ref[rows, :] = (yg * lax.rsqrt(ms + NORM_EPS) * nw_ref[...]).astype(o_ref.dtype)
        return carry

    lax.fori_loop(0, t_blk // L, chunk, 0, unroll=min(SSD_CHUNK_UNROLL, t_blk // L))
    xin_sc[0:halo, :] = xin_sc[t_blk:t_blk + halo, :]
    bcin_sc[0:halo, :] = bcin_sc[t_blk:t_blk + halo, :]


def _ssd(proj, offs, pre, conv_w, conv_b, d_skip, ssd_norm_w, batch, seq, t_blk):
    m = proj.shape[0]
    d_inner = ssd_norm_w.shape[0]
    nh = d_skip.shape[0]
    G = SSD_N_GROUPS
    N = SSD_D_STATE
    L = SSD_CHUNK
    r = nh // G
    gw = r * SSD_HEAD_DIM
    assert gw % LANES == 0 and r % 2 == 0 and r % SUBLANES == 0
    dtb, eacs, toend, acsrot, acst, decend = pre
    nt = seq // t_blk
    ncb = t_blk // L
    e_all = (jnp.arange(LANES)[None, :, None]
             == (jnp.arange(G)[:, None, None] * r + jnp.arange(gw)[None, None, :] // SSD_HEAD_DIM)).astype(BF16)
    dskip_e = jnp.repeat(d_skip.astype(F32), SSD_HEAD_DIM).reshape(1, d_inner)

    def bc_pairs(a):
        b_part = a[:, d_inner:d_inner + G * N].reshape(-1, G, 1, N)
        c_part = a[:, d_inner + G * N:].reshape(-1, G, 1, N)
        return jnp.concatenate([b_part, c_part], axis=2).reshape(-1, G * 2 * N)

    cb2 = conv_b.reshape(1, -1)
    cwx, cbx = conv_w[:, :d_inner], cb2[:, :d_inner]
    cwbc, cbbc = bc_pairs(conv_w), bc_pairs(cb2)
    kw = SSD_CONV_WIDTH
    tt = jnp.arange((kw - 1) * L)
    shift = (jnp.arange(CONV_HALO + L)[None, :]
             == (CONV_HALO + tt % L - (kw - 1) + tt // L)[:, None]).astype(BF16)
    for o in (offs["xs"], offs["z"]):
        assert o % gw == 0
    row = lambda b, g, t: b * nt + t
    in_specs = [
        pl.BlockSpec((t_blk, gw), lambda b, g, t: (row(b, g, t), offs["xs"] // gw + g)),
        pl.BlockSpec((t_blk, N), lambda b, g, t: (row(b, g, t), offs["bm"] // N + g)),
        pl.BlockSpec((t_blk, N), lambda b, g, t: (row(b, g, t), offs["cm"] // N + g)),
        pl.BlockSpec((t_blk, gw), lambda b, g, t: (row(b, g, t), offs["z"] // gw + g)),
        pl.BlockSpec((t_blk, LANES), lambda b, g, t: (row(b, g, t), 0)),
        pl.BlockSpec((t_blk, LANES), lambda b, g, t: (row(b, g, t), 0)),
        pl.BlockSpec((t_blk, LANES), lambda b, g, t: (row(b, g, t), 0)),
        pl.BlockSpec((None, t_blk, LANES), lambda b, g, t: (g, row(b, g, t), 0)),
        pl.BlockSpec((ncb, r, L), lambda b, g, t: (row(b, g, t), g, 0)),
        pl.BlockSpec((ncb, 1, gw), lambda b, g, t: (row(b, g, t), 0, g)),
        pl.BlockSpec((SSD_CONV_WIDTH, gw), lambda b, g, t: (0, g)),
        pl.BlockSpec((1, gw), lambda b, g, t: (0, g)),
        pl.BlockSpec((SSD_CONV_WIDTH, 2 * N), lambda b, g, t: (0, g)),
        pl.BlockSpec((1, 2 * N), lambda b, g, t: (0, g)),
        pl.BlockSpec((1, gw), lambda b, g, t: (0, g)),
        pl.BlockSpec((1, gw), lambda b, g, t: (0, g)),
        pl.BlockSpec((None, LANES, gw), lambda b, g, t: (g, 0, 0)),
        pl.BlockSpec(((kw - 1) * L, CONV_HALO + L), lambda b, g, t: (0, 0)),
    ]
    return pl.pallas_call(
        functools.partial(_ssd_kernel, t_blk=t_blk, r=r),
        grid=(batch, G, nt),
        in_specs=in_specs,
        out_specs=pl.BlockSpec((t_blk, gw), lambda b, g, t: (row(b, g, t), g)),
        out_shape=jax.ShapeDtypeStruct((m, d_inner), BF16),
        scratch_shapes=[
            pltpu.VMEM((t_blk + CONV_HALO, gw), BF16),
            pltpu.VMEM((t_blk + CONV_HALO, 2 * N), BF16),
            pltpu.VMEM((N, gw), F32),
        ],
        compiler_params=_cparams(("parallel", "parallel", "arbitrary")),
        name="ssd_scan",
    )(proj, proj, proj, proj, dtb, eacs, toend, acsrot, acst, decend,
      cwx, cbx, cwbc, cbbc, dskip_e, ssd_norm_w.reshape(1, d_inner), e_all, shift)


def _flash_kernel(q_ref, kn_ref, kr_ref, v_ref, g_ref, o_ref, vt_sc, sa_sc, sb_sc, xa_sc, xb_sc,
                  qt0_sc, qt1_sc, m0_sc, m1_sc, l0_sc, l1_sc, acc0_sc, acc1_sc, *, tq, tk, seq):
    n_sub = tq // tk
    nq = seq // tq
    tqs = min(2 * LANES, tq)
    nch = tq // tqs
    assert n_sub == 2 and nq % 2 == 0 and nq >= 4
    buf_a, buf_b = (sa_sc, xa_sc), (sb_sc, xb_sc)
    state = ((qt0_sc, m0_sc, l0_sc, acc0_sc), (qt1_sc, m1_sc, l1_sc, acc1_sc))

    for c in range(seq // tk):
        vt_sc[c] = v_ref[c * tk:(c + 1) * tk, :].astype(F32).T.astype(BF16)

    def step(nxt_blk, cur_blk):
        nxt, cur = [], []
        if nxt_blk is not None:
            kb_next, bufs_next, diag_next, qt = nxt_blk
            rows = pl.ds(pl.multiple_of(kb_next * tk, tk), tk)
            k = jnp.concatenate([kn_ref[rows, :], kr_ref[rows, :]], axis=1)
            k0 = 0 if diag_next is None else diag_next * tk
            nxt = list(range(k0, tq, tqs))
        if cur_blk is not None:
            kb, bufs, diag, (m_sc, l_sc, acc_sc) = cur_blk
            vt = vt_sc[kb]
            cur = list(range(0 if diag is None else diag * tk, tq, tqs))
        def n_keys(blk_diag, blk_k0, c0):
            return tqs if (blk_diag is not None and c0 == blk_k0) else tk

        for i in range(max(len(nxt), len(cur))):
            if i < len(nxt):
                c0 = nxt[i]
                nk = n_keys(diag_next, k0, c0)
                st = _dot(k[0:nk], qt[c0 // tqs])
                if diag_next is not None and c0 < k0 + tk:
                    ki = lax.broadcasted_iota(jnp.int32, (nk, tqs), 0) + k0
                    qj = lax.broadcasted_iota(jnp.int32, (nk, tqs), 1) + c0
                    st = jnp.where(ki <= qj, st, -jnp.inf)
                bufs_next[0][c0 // tqs, 0:nk] = st
                bufs_next[1][c0 // tqs] = jnp.max(st, axis=0, keepdims=True)
            if i < len(cur):
                ci = cur[i] // tqs
                nk = n_keys(diag, 0 if diag is None else diag * tk, cur[i])
                st = bufs[0][ci, 0:nk]
                m_prev = m_sc[ci]
                m_new = jnp.maximum(m_prev, bufs[1][ci])
                alpha = jnp.exp2(m_prev - m_new)
                p = jnp.exp2(st - m_new)
                l_sc[ci] = alpha * l_sc[ci] + jnp.sum(p, axis=0, keepdims=True)
                acc_sc[ci] = acc_sc[ci] * alpha + _dot(vt[:, 0:nk], p.astype(BF16))
                m_sc[ci] = m_new

    def init(qi, par):
        qt, m_sc, l_sc, acc_sc = state[par]
        for ci in range(nch):
            qsub = pl.ds(pl.multiple_of(qi * tq + ci * tqs, tqs), tqs)
            qt[ci] = q_ref[qsub, :].astype(F32).T.astype(BF16)
        m_sc[...] = jnp.full(m_sc.shape, -jnp.inf, F32)
        l_sc[...] = jnp.zeros(l_sc.shape, F32)
        acc_sc[...] = jnp.zeros(acc_sc.shape, F32)

    def finalize(qi, par):
        _, _, l_sc, acc_sc = state[par]
        for ci in range(nch):
            qsub = pl.ds(pl.multiple_of(qi * tq + ci * tqs, tqs), tqs)
            o = (acc_sc[ci] / l_sc[ci]).T
            o_ref[qsub, :] = (o * _silu(g_ref[qsub, :].astype(F32))).astype(o_ref.dtype)

    def body(qi, par):
        qt, stats = state[par][0], state[par][1:]
        d0 = qi * n_sub

        def pair(j):
            step((2 * j + 1, buf_b, None, qt), (2 * j, buf_a, None, stats))
            step((2 * j + 2, buf_a, None, qt), (2 * j + 1, buf_b, None, stats))

        def quad(i, c):
            pair(2 * i)
            pair(2 * i + 1)
            return c

        n_pairs = qi - 1
        lax.fori_loop(0, n_pairs // 2, quad, 0)

        @pl.when(n_pairs % 2 == 1)
        def _():
            pair(n_pairs - 1)

        step((d0 - 1, buf_b, None, qt), (d0 - 2, buf_a, None, stats))
        step((d0, buf_a, 0, qt), (d0 - 1, buf_b, None, stats))
        step((d0 + 1, buf_b, 1, qt), (d0, buf_a, 0, stats))

    def handover(qi, par):
        init(qi + 1, 1 - par)
        step((0, buf_a, None, state[1 - par][0]), (qi * n_sub + 1, buf_b, 1, state[par][1:]))
        finalize(qi, par)

    init(0, 0)
    step((0, buf_a, 0, state[0][0]), None)
    step((1, buf_b, 1, state[0][0]), (0, buf_a, 0, state[0][1:]))
    handover(0, 0)

    def two_blocks(i, carry):
        qi = 2 * i + 1
        body(qi, 1)
        handover(qi, 1)
        body(qi + 1, 0)
        handover(qi + 1, 0)
        return carry

    lax.fori_loop(0, nq // 2 - 1, two_blocks, 0)
    body(nq - 1, 1)
    step(None, ((nq - 1) * n_sub + 1, buf_b, 1, state[1][1:]))
    finalize(nq - 1, 1)


def _flash(q, kv, kr, proj, gate_off, batch, seq, n_heads, tq, tk):
    m = q.shape[0]
    dq = 2 * LANES
    tqs = min(2 * LANES, tq)
    nch = tq // tqs
    assert gate_off % LANES == 0
    return pl.pallas_call(
        functools.partial(_flash_kernel, tq=tq, tk=tk, seq=seq),
        grid=(batch, n_heads),
        in_specs=[
            pl.BlockSpec((seq, dq), lambda b, h: (b, h)),
            pl.BlockSpec((seq, LANES), lambda b, h: (b, 2 * h)),
            pl.BlockSpec((seq, LANES), lambda b, h: (b, 0)),
            pl.BlockSpec((seq, LANES), lambda b, h: (b, 2 * h + 1)),
            pl.BlockSpec((seq, LANES), lambda b, h: (b, gate_off // LANES + h)),
        ],
        out_specs=pl.BlockSpec((seq, LANES), lambda b, h: (b, h)),
        out_shape=jax.ShapeDtypeStruct((m, n_heads * MLA_V_DIM), BF16),
        scratch_shapes=[pltpu.VMEM((seq // tk, MLA_V_DIM, tk), BF16),
                        pltpu.VMEM((nch, tk, tqs), F32), pltpu.VMEM((nch, tk, tqs), F32),
                        pltpu.VMEM((nch, 1, tqs), F32), pltpu.VMEM((nch, 1, tqs), F32)]
                       + [pltpu.VMEM((nch, dq, tqs), BF16)] * 2 + [pltpu.VMEM((nch, 1, tqs), F32)] * 4
                       + [pltpu.VMEM((nch, MLA_V_DIM, tqs), F32)] * 2,
        compiler_params=_cparams(("parallel", "parallel")),
        name="mla_flash",
    )(q, kv, kr, kv, proj)


def _ssd_out_kernel(a_ref, w_ref, g_ref, o_ref):
    y = _dot(a_ref[...], w_ref[...])
    o_ref[...] = (jax.nn.sigmoid(g_ref[...].astype(F32)) * y).astype(o_ref.dtype)


def _ssd_out(ys, w_ssd, l, proj, gate_off, tm, tn):
    m, k1 = ys.shape
    d = w_ssd.shape[2]
    assert gate_off % tn == 0
    return pl.pallas_call(
        _ssd_out_kernel,
        grid=(m // tm, d // tn),
        in_specs=[
            pl.BlockSpec((tm, k1), lambda i, j: (i, 0)),
            pl.BlockSpec((None, k1, tn), lambda i, j: (l, 0, j)),
            pl.BlockSpec((tm, tn), lambda i, j: (i, gate_off // tn + j)),
        ],
        out_specs=pl.BlockSpec((tm, tn), lambda i, j: (i, j)),
        out_shape=jax.ShapeDtypeStruct((m, d), BF16),
        compiler_params=_cparams(("parallel", "parallel")),
        name="ssd_out",
    )(ys, w_ssd, proj)


def _mla_out_kernel(a_ref, w_ref, g_ref, p_ref, o_ref):
    y = _dot(a_ref[...], w_ref[...])
    o_ref[...] = (p_ref[...].astype(F32) + jax.nn.sigmoid(g_ref[...].astype(F32)) * y).astype(o_ref.dtype)


def _mla_out(og, w_mla, l, proj, gate_off, part, tm, tn):
    m, k2 = og.shape
    d = w_mla.shape[2]
    assert gate_off % tn == 0
    return pl.pallas_call(
        _mla_out_kernel,
        grid=(m // tm, d // tn),
        in_specs=[
            pl.BlockSpec((tm, k2), lambda i, j: (i, 0)),
            pl.BlockSpec((None, k2, tn), lambda i, j: (l, 0, j)),
            pl.BlockSpec((tm, tn), lambda i, j: (i, gate_off // tn + j)),
            pl.BlockSpec((tm, tn), lambda i, j: (i, j)),
        ],
        out_specs=pl.BlockSpec((tm, tn), lambda i, j: (i, j)),
        out_shape=jax.ShapeDtypeStruct((m, d), BF16),
        compiler_params=_cparams(("parallel", "parallel")),
        name="mla_out",
    )(og, w_mla, proj, part)


def _out_kernel(a_ref, w_ref, x_ref, pw_ref, gate_ref, o_ref):
    y = _dot(a_ref[...], w_ref[...])
    yn = y * lax.rsqrt(jnp.mean(y * y, axis=-1, keepdims=True) + NORM_EPS) * pw_ref[...]
    o_ref[...] = x_ref[...] + gate_ref[...] * yn


def _out_proj(merged, w_out, l, x2, post_w, mod4, seq, tm):
    m, d = x2.shape
    per_b = seq // tm
    return pl.pallas_call(
        _out_kernel,
        grid=(m // tm,),
        in_specs=[
            pl.BlockSpec((tm, d), lambda i: (i, 0)),
            pl.BlockSpec((None, d, d), lambda i: (l, 0, 0), pipeline_mode=pl.Buffered(1)),
            pl.BlockSpec((tm, d), lambda i: (i, 0)),
            pl.BlockSpec((1, d), lambda i: (0, 0)),
            pl.BlockSpec((None, None, 1, d), lambda i: (i // per_b, 2, 0, 0)),
        ],
        out_specs=pl.BlockSpec((tm, d), lambda i: (i, 0)),
        out_shape=jax.ShapeDtypeStruct((m, d), F32),
        compiler_params=pltpu.CompilerParams(dimension_semantics=("parallel",),
                                             vmem_limit_bytes=OUT_PROJ_VMEM_LIMIT),
        name="out_proj",
    )(merged, w_out, x2, post_w.reshape(1, d), mod4)


def _proj_layout(d, d_inner, nh, n_mla_heads, q_lora, kv_lora):
    gn = SSD_N_GROUPS * SSD_D_STATE
    widths = [("z", d_inner), ("xs", d_inner), ("bm", gn), ("cm", gn),
              ("gate", n_mla_heads * MLA_V_DIM), ("merge", 2 * d), ("cq", q_lora), ("ckv", kv_lora),
              ("kr", 2 * MLA_QK_ROPE), ("dt", nh)]
    offs, o = {}, 0
    for name, w in widths:
        offs[name] = o
        o += w
    return offs, o


def _cast_kernel(w_ref, o_ref):
    o_ref[...] = w_ref[...].astype(o_ref.dtype)


def _cast_bf16(w3):
    depth, k, n = w3.shape
    tr = max(2 * SUBLANES, min(k, CAST_BLOCK_BYTES // (n * 4)))
    assert k % tr == 0
    return pl.pallas_call(
        _cast_kernel,
        grid=(depth, k // tr),
        in_specs=[pl.BlockSpec((None, tr, n), lambda l, r: (l, r, 0))],
        out_specs=pl.BlockSpec((None, tr, n), lambda l, r: (l, r, 0)),
        out_shape=jax.ShapeDtypeStruct((depth, k, n), BF16),
        compiler_params=_cparams(("parallel", "parallel")),
        name="cast_bf16",
    )(w3)


def _w_in_kernel(src_ref, mode_ref, *refs, n_chunks):
    o_ref = refs[n_chunks]
    j = pl.program_id(1)
    half = MLA_QK_ROPE // 2
    parts = []
    for c in range(n_chunks):
        blk = refs[c][...]
        mode = mode_ref[j * n_chunks + c]
        rot = jnp.concatenate([-blk[half:], blk[:half]], axis=0)
        v = jnp.where(mode == W_IN_ROTATE, rot, blk)
        parts.append(jnp.where(mode == W_IN_ZERO, 0.0, v))
    o_ref[...] = jnp.concatenate(parts, axis=0).T.astype(BF16)


def _prep_w_in(w_in3, offs, n_tot, n_pad, d, d_inner, nh, n_mla_heads, q_lora, kv_lora):
    depth, k, n_src = w_in3.shape
    gn = SSD_N_GROUPS * SSD_D_STATE
    conv_dim = d_inner + 2 * gn
    s_xbc = d_inner
    s_dt = s_xbc + conv_dim
    s_cq = s_dt + nh
    s_ckv = s_cq + q_lora
    s_kr = s_ckv + kv_lora
    s_gate = s_kr + MLA_QK_ROPE
    s_merge = s_gate + n_mla_heads * MLA_V_DIM
    assert s_merge + 2 * d == n_src and MLA_QK_ROPE == W_IN_CHUNK
    segments = (
        (offs["z"], 0, d_inner + conv_dim, W_IN_COPY),
        (offs["gate"], s_gate, n_mla_heads * MLA_V_DIM + 2 * d, W_IN_COPY),
        (offs["cq"], s_cq, q_lora + kv_lora, W_IN_COPY),
        (offs["kr"], s_kr, MLA_QK_ROPE, W_IN_COPY),
        (offs["kr"] + MLA_QK_ROPE, s_kr, MLA_QK_ROPE, W_IN_ROTATE),
        (offs["dt"], s_dt, nh, W_IN_COPY),
    )
    n_dst = n_pad // W_IN_CHUNK
    src_tab = [0] * n_dst
    mode_tab = [W_IN_ZERO] * n_dst
    for dst, src, width, mode in segments:
        assert dst % W_IN_CHUNK == 0 and src % W_IN_CHUNK == 0 and width % W_IN_CHUNK == 0
        for t in range(width // W_IN_CHUNK):
            src_tab[dst // W_IN_CHUNK + t] = src // W_IN_CHUNK + t
            mode_tab[dst // W_IN_CHUNK + t] = mode
    n_chunks = W_IN_COLS // W_IN_CHUNK
    w_t = jnp.swapaxes(w_in3, 1, 2)

    def chunk_spec(c):
        return pl.BlockSpec((None, W_IN_CHUNK, k), lambda l, j, src, mode: (l, src[j * n_chunks + c], 0))

    return pl.pallas_call(
        functools.partial(_w_in_kernel, n_chunks=n_chunks),
        grid_spec=pltpu.PrefetchScalarGridSpec(
            num_scalar_prefetch=2,
            grid=(depth, n_pad // W_IN_COLS),
            in_specs=[chunk_spec(c) for c in range(n_chunks)],
            out_specs=pl.BlockSpec((None, k, W_IN_COLS), lambda l, j, src, mode: (l, 0, j)),
        ),
        out_shape=jax.ShapeDtypeStruct((depth, k, n_pad), BF16),
        compiler_params=_cparams(("parallel", "parallel")),
        name="w_in_prep",
    )(jnp.asarray(src_tab, jnp.int32), jnp.asarray(mode_tab, jnp.int32), *([w_t] * n_chunks))


def _w_q_kernel(w_ref, o_ref, *, n_heads):
    qk = MLA_QK_NOPE + MLA_QK_ROPE
    half = MLA_QK_ROPE // 2
    scale = qk ** -0.5 * LOG2_E

    def load(src, width):
        lead = src % LANES
        v = w_ref[:, src - lead:src + width]
        return v[:, lead:] if lead else v

    for h in range(n_heads):
        nope = load(h * qk, MLA_QK_NOPE)
        rope = load(h * qk + MLA_QK_NOPE, MLA_QK_ROPE)
        blk = jnp.concatenate([nope, rope, -rope[:, half:], rope[:, :half]], axis=1)
        o_ref[:, h * 2 * LANES:(h + 1) * 2 * LANES] = (blk * scale).astype(BF16)


def _prep_w_q(w_q_up3, n_mla_heads):
    depth, k, n = w_q_up3.shape
    tr = min(k, 256)
    return pl.pallas_call(
        functools.partial(_w_q_kernel, n_heads=n_mla_heads),
        grid=(depth, k // tr),
        in_specs=[pl.BlockSpec((None, tr, n), lambda l, r: (l, r, 0))],
        out_specs=pl.BlockSpec((None, tr, n_mla_heads * 2 * LANES), lambda l, r: (l, r, 0)),
        out_shape=jax.ShapeDtypeStruct((depth, k, n_mla_heads * 2 * LANES), BF16),
        compiler_params=_cparams(("parallel", "parallel")),
        name="w_q_prep",
    )(w_q_up3)


def _round_up(v, mult):
    return (v + mult - 1) // mult * mult


def _tile_plan(m, seq, d, n_mla):
    qkv_n = n_mla * 2 * LANES
    tq = min(1024, seq)
    return dict(
        tm=min(1024, m),
        proj_tn=1024,
        q_tn=min(2048, qkv_n), kv_tn=min(4096, qkv_n),
        ssd_pre_rows=min(1024, seq), ssd_rows=min(1024, seq),
        flash_tq=tq, flash_tk=min(512, tq),
        ssd_out_tn=min(512, d), mla_out_tn=min(1024, d),
        out_tm=min(256, seq),
    )


def kernel(x, c, positions, ada_w, ada_b, pre_norm_w, post_norm_w, w_in, conv_w, conv_b, dt_bias,
           a_log, d_skip, ssd_norm_w, q_norm_w, w_q_up, kv_norm_w, w_kv_up, w_ssd_proj, w_mla_proj,
           w_out):
    batch, seq, d = x.shape
    depth = ada_w.shape[0]
    m = batch * seq
    d_inner = ssd_norm_w.shape[1]
    nh = dt_bias.shape[1]
    q_lora = q_norm_w.shape[1]
    kv_lora = kv_norm_w.shape[1]
    n_mla = w_mla_proj.shape[1] // MLA_V_DIM
    offs, n_tot = _proj_layout(d, d_inner, nh, n_mla, q_lora, kv_lora)
    t = _tile_plan(m, seq, d, n_mla)
    n_pad = _round_up(n_tot, t["proj_tn"])

    c_pad = jnp.zeros((SUBLANES, d), F32).at[:batch].set(c)
    mod = _ada(c_pad, ada_w, ada_b)
    cos, sin = _rope_tables(positions)

    w_in_b = _prep_w_in(w_in, offs, n_tot, n_pad, d, d_inner, nh, n_mla, q_lora, kv_lora)
    w_q_b = _prep_w_q(w_q_up, n_mla)
    w_kv_b = _cast_bf16(w_kv_up)
    out_side = (w_ssd_proj, w_mla_proj, w_out)

    x2 = x.reshape(m, d)
    for l in range(depth):
        mod4 = mod[l, :batch].reshape(batch, 3, 1, d)
        h = _prenorm(x2, pre_norm_w[l], mod4, seq)
        if l == 0:
            steps = (m // t["tm"]) * (n_pad // t["proj_tn"])
            ride = [_rides_along(w, steps) for w in out_side]
            proj, cast = _matmul(h, w_in_b, l, BF16, t["tm"], t["proj_tn"],
                                 side_casts=[w.reshape(-1, w.shape[-1]) for w, r in zip(out_side, ride) if r])
            cast = iter(cast)
            w_ssd_b, w_mla_b, w_out_b = (next(cast).reshape(w.shape) if r else _cast_bf16(w)
                                         for w, r in zip(out_side, ride))
        else:
            proj, _ = _matmul(h, w_in_b, l, BF16, t["tm"], t["proj_tn"])

        pre = _ssd_pre(proj, offs["dt"], dt_bias[l], a_log[l], d_inner, tp=t["ssd_pre_rows"])
        ys = _ssd(proj, offs, pre, conv_w[l], conv_b[l], d_skip[l], ssd_norm_w[l], batch, seq,
                  t_blk=t["ssd_rows"])

        q = _norm_matmul(proj, offs["cq"], q_lora, q_norm_w[l], w_q_b, l,
                         t["tm"], t["q_tn"], cos, sin, name="q_up")
        kv = _norm_matmul(proj, offs["ckv"], kv_lora, kv_norm_w[l], w_kv_b, l,
                          t["tm"], t["kv_tn"], name="kv_up")
        kr = _krope(proj, offs["kr"], cos, sin)
        og = _flash(q, kv, kr, proj, offs["gate"], batch, seq, n_mla, t["flash_tq"], t["flash_tk"])

        part = _ssd_out(ys, w_ssd_b, l, proj, offs["merge"], t["tm"], t["ssd_out_tn"])
        merged = _mla_out(og, w_mla_b, l, proj, offs["merge"] + d, part, t["tm"], t["mla_out_tn"])
        x2 = _out_proj(merged, w_out_b, l, x2, post_norm_w[l], mod4, seq, t["out_tm"])
    return x2.reshape(batch, seq, d)
```
